```python
import math
import jax, jax.numpy as jnp
from jax import lax
import numpy as np


D_MODEL = 2048
BATCH = 8
SEQ = 2048
DEPTH = 1

N_MEM = 256
EPS = 1e-6
MIX_WIDTH = D_MODEL
ATTN_WIDTH = MIX_WIDTH // 2
SSM_WIDTH = MIX_WIDTH - ATTN_WIDTH
DIFF_HEADS = 8
DIFF_VDIM = ATTN_WIDTH // DIFF_HEADS
DIFF_QKDIM = DIFF_VDIM // 2
ROT_DIM = DIFF_QKDIM // 4
ROPE_THETA = 500000.0
Q_BLOCK = 128
IN_COLS = 3 * ATTN_WIDTH + SSM_WIDTH
SSM_GROUP = 16
SSM_GROUPS = SSM_WIDTH // SSM_GROUP
SSM_STATE = 64
DT_MIN = 0.001
DT_MAX = 0.1
XATTN_HEADS = 4
XATTN_HEAD_DIM = 128
XATTN_WIDTH = XATTN_HEADS * XATTN_HEAD_DIM
PEER_HEADS = 8
PEER_KEYS = 128
PEER_EXPERTS = PEER_KEYS * PEER_KEYS
PEER_QDIM = 256
PEER_TOPK = 16
PEER_CHUNK = 128

kernel_name = 'hymba_diffattn_s5_peer_block'


def rmsnorm(x, g):
    xf = x.astype(jnp.float32)
    y = xf * lax.rsqrt(jnp.mean(xf * xf, axis=-1, keepdims=True) + EPS)
    return (y * g.astype(jnp.float32)).astype(x.dtype)


def rope_tables(positions):
    freqs = ROPE_THETA ** (-jnp.arange(0, ROT_DIM, 2, dtype=jnp.float32) / ROT_DIM)
    ang = positions.astype(jnp.float32)[..., None] * freqs
    return jnp.cos(ang)[:, :, None, None, :], jnp.sin(ang)[:, :, None, None, :]


def partial_rope(t, cos, sin):
    half = ROT_DIM // 2
    r1, r2, rest = t[..., :half], t[..., half:ROT_DIM], t[..., ROT_DIM:]
    c, s = cos.astype(t.dtype), sin.astype(t.dtype)
    return jnp.concatenate([r1 * c - r2 * s, r1 * s + r2 * c, rest], axis=-1)


def diff_attention(q, k, v, lam):
    B, S, H, _, d = q.shape
    nblk = S // Q_BLOCK
    scale = DIFF_QKDIM ** -0.5
    qb = q.reshape(B, nblk, Q_BLOCK, H, 2, d).transpose(1, 0, 2, 3, 4, 5)
    k_idx = jnp.arange(S)

    def one_block(args):
        qblk, bi = args
        s = jnp.einsum('bqhcd,bkhcd->bhcqk', qblk, k).astype(jnp.float32) * scale
        q_idx = bi * Q_BLOCK + jnp.arange(Q_BLOCK)
        mask = k_idx[None, :] <= q_idx[:, None]
        p = jax.nn.softmax(jnp.where(mask, s, -jnp.inf), axis=-1)
        a = p[:, :, 0] - lam * p[:, :, 1]
        return jnp.einsum('bhqk,bkhe->bqhe', a.astype(v.dtype), v)

    out = lax.map(one_block, (qb, jnp.arange(nblk)))
    return out.transpose(1, 0, 2, 3, 4).reshape(B, S, H, v.shape[-1])


def s5_ssm(u, a_re, a_im, log_dt, b_re, b_im, c_re, c_im, d_skip):
    f32 = jnp.float32
    dt = jnp.exp(log_dt.astype(f32))[:, None]
    lre, lim = a_re.astype(f32), a_im.astype(f32)
    mag = jnp.exp(lre * dt)
    ang = lim * dt
    ab_re, ab_im = mag * jnp.cos(ang), mag * jnp.sin(ang)
    den = lre * lre + lim * lim
    f_re = ((ab_re - 1.0) * lre + ab_im * lim) / den
    f_im = (ab_im * lre - (ab_re - 1.0) * lim) / den
    br, bim = b_re.astype(f32), b_im.astype(f32)
    bb_re = f_re[..., None] * br - f_im[..., None] * bim
    bb_im = f_re[..., None] * bim + f_im[..., None] * br
    uf = u.astype(f32)
    bu_re = jnp.einsum('bsgh,gph->bsgp', uf, bb_re)
    bu_im = jnp.einsum('bsgh,gph->bsgp', uf, bb_im)
    S = u.shape[1]
    a_full_re = jnp.broadcast_to(ab_re, (1, S) + ab_re.shape)
    a_full_im = jnp.broadcast_to(ab_im, (1, S) + ab_im.shape)

    def combine(e1, e2):
        a1r, a1i, b1r, b1i = e1
        a2r, a2i, b2r, b2i = e2
        return (a2r * a1r - a2i * a1i,
                a2r * a1i + a2i * a1r,
                a2r * b1r - a2i * b1i + b2r,
                a2r * b1i + a2i * b1r + b2i)

    _, _, xr, xi = lax.associative_scan(combine, (a_full_re, a_full_im, bu_re, bu_im), axis=1)
    y = (jnp.einsum('bsgp,ghp->bsgh', xr, c_re.astype(f32))
         - jnp.einsum('bsgp,ghp->bsgh', xi, c_im.astype(f32))
         + d_skip.astype(f32) * uf)
    return y.astype(u.dtype)


def cross_attention(h, mem, wq, wkv, wo):
    B, S, _ = h.shape
    M = mem.shape[1]
    q = (h @ wq).reshape(B, S, XATTN_HEADS, XATTN_HEAD_DIM)
    kv = (mem @ wkv).reshape(M, B, 2, XATTN_HEADS, XATTN_HEAD_DIM) if False else (mem @ wkv).reshape(B, M, 2, XATTN_HEADS, XATTN_HEAD_DIM)
    k, v = kv[:, :, 0], kv[:, :, 1]
    s = jnp.einsum('bshd,bmhd->bhsm', q, k).astype(jnp.float32) * XATTN_HEAD_DIM ** -0.5
    p = jax.nn.softmax(s, axis=-1).astype(h.dtype)
    o = jnp.einsum('bhsm,bmhd->bshd', p, v).reshape(B, S, XATTN_WIDTH)
    return o @ wo


def peer(h, wq, sub_k1, sub_k2, u_tab, v_tab):
    B, S, D = h.shape
    half = PEER_QDIM // 2
    q = (h @ wq).reshape(B, S, PEER_HEADS, PEER_QDIM)
    s1 = jnp.einsum('bshd,hnd->bshn', q[..., :half], sub_k1).astype(jnp.float32)
    s2 = jnp.einsum('bshd,hnd->bshn', q[..., half:], sub_k2).astype(jnp.float32)
    t1, i1 = lax.top_k(s1, PEER_TOPK)
    t2, i2 = lax.top_k(s2, PEER_TOPK)
    cand = (t1[..., :, None] + t2[..., None, :]).reshape(B, S, PEER_HEADS, PEER_TOPK * PEER_TOPK)
    ts, ic = lax.top_k(cand, PEER_TOPK)
    e1 = jnp.take_along_axis(i1, ic // PEER_TOPK, axis=-1)
    e2 = jnp.take_along_axis(i2, ic % PEER_TOPK, axis=-1)
    experts = e1 * PEER_KEYS + e2
    gates = jax.nn.softmax(ts, axis=-1).astype(h.dtype)
    nc = (B * S) // PEER_CHUNK
    E = PEER_HEADS * PEER_TOPK
    hx = h.reshape(nc, PEER_CHUNK, D)
    ex = experts.reshape(nc, PEER_CHUNK, E)
    gt = gates.reshape(nc, PEER_CHUNK, E)

    def chunk(args):
        xc, ec, gc = args
        act = jnp.einsum('cd,ced->ce', xc, u_tab[ec])
        w = gc * jax.nn.gelu(act, approximate=False)
        return jnp.einsum('ce,ced->cd', w, v_tab[ec])

    return lax.map(chunk, (hx, ex, gt)).reshape(B, S, D)


def setup_inputs(seed: int = 0) -> dict:
    key = jax.random.key(seed)
    ks = iter(jax.random.split(key, 48))
    L, D, f32 = DEPTH, D_MODEL, jnp.float32

    def nrm(shape, scale):
        return jax.random.normal(next(ks), shape, f32) * scale

    def gain(shape):
        return 1.0 + 0.05 * jax.random.normal(next(ks), shape, f32)

    x = nrm((BATCH, SEQ, D), 1.0)
    mem = nrm((BATCH, N_MEM, D), 1.0)
    positions = (jax.random.randint(next(ks), (BATCH, 1), 0, 4096, dtype=jnp.int32)
                 + jnp.arange(SEQ, dtype=jnp.int32)[None, :])
    G, P, Hc = SSM_GROUPS, SSM_STATE, SSM_GROUP
    n = jnp.arange(P, dtype=f32)
    ssm_a_re = -0.5 * jnp.exp(0.02 * jax.random.normal(next(ks), (L, G, P), f32))
    ssm_a_im = math.pi * n + 0.02 * jax.random.normal(next(ks), (L, G, P), f32)
    ssm_log_dt = math.log(DT_MIN) + jax.random.uniform(next(ks), (L, G), f32) * (math.log(DT_MAX) - math.log(DT_MIN))
    return {
        'x': x,
        'mem': mem,
        'positions': positions,
        'mix_norm_g': gain((L, D)),
        'w_in': nrm((L, D, IN_COLS), D ** -0.5),
        'lam_q1': nrm((L, DIFF_QKDIM), 0.1),
        'lam_k1': nrm((L, DIFF_QKDIM), 0.1),
        'lam_q2': nrm((L, DIFF_QKDIM), 0.1),
        'lam_k2': nrm((L, DIFF_QKDIM), 0.1),
        'attn_head_g': gain((L, DIFF_VDIM)),
        'ssm_a_re': ssm_a_re,
        'ssm_a_im': ssm_a_im,
        'ssm_log_dt': ssm_log_dt,
        'ssm_b_re': nrm((L, G, P, Hc), (2 * Hc) ** -0.5),
        'ssm_b_im': nrm((L, G, P, Hc), (2 * Hc) ** -0.5),
        'ssm_c_re': nrm((L, G, Hc, P), (2 * P) ** -0.5),
        'ssm_c_im': nrm((L, G, Hc, P), (2 * P) ** -0.5),
        'ssm_d': nrm((L, G, Hc), 1.0),
        'glu_w': nrm((L, SSM_WIDTH, SSM_WIDTH), SSM_WIDTH ** -0.5),
        'glu_b': nrm((L, SSM_WIDTH), 0.01),
        'ssm_out_g': gain((L, SSM_WIDTH)),
        'w_out': nrm((L, MIX_WIDTH, D), MIX_WIDTH ** -0.5),
        'xattn_norm_g': gain((L, D)),
        'mem_norm_g': gain((L, D)),
        'xattn_wq': nrm((L, D, XATTN_WIDTH), D ** -0.5),
        'xattn_wkv': nrm((L, D, 2 * XATTN_WIDTH), D ** -0.5),
        'xattn_wo': nrm((L, XATTN_WIDTH, D), XATTN_WIDTH ** -0.5),
        'ffn_norm_g': gain((L, D)),
        'peer_wq': nrm((L, D, PEER_HEADS * PEER_QDIM), D ** -0.5),
        'peer_k1': nrm((L, PEER_HEADS, PEER_KEYS, PEER_QDIM // 2), (PEER_QDIM // 2) ** -0.5),
        'peer_k2': nrm((L, PEER_HEADS, PEER_KEYS, PEER_QDIM // 2), (PEER_QDIM // 2) ** -0.5),
        'peer_u': nrm((L, PEER_EXPERTS, D), D ** -0.5),
        'peer_v': nrm((L, PEER_EXPERTS, D), 0.5),
        'final_norm_g': gain((D,)),
    }


def reference(x, mem, positions, mix_norm_g, w_in, lam_q1, lam_k1, lam_q2, lam_k2,
              attn_head_g, ssm_a_re, ssm_a_im, ssm_log_dt, ssm_b_re, ssm_b_im,
              ssm_c_re, ssm_c_im, ssm_d, glu_w, glu_b, ssm_out_g, w_out,
              xattn_norm_g, mem_norm_g, xattn_wq, xattn_wkv, xattn_wo, ffn_norm_g,
              peer_wq, peer_k1, peer_k2, peer_u, peer_v, final_norm_g):
    B, S, D = x.shape
    cos, sin = rope_tables(positions)
    h = x
    for l in range(DEPTH):
        lam_init = 0.8 - 0.6 * math.exp(-0.3 * l)
        xn = rmsnorm(h, mix_norm_g[l])
        z = xn @ w_in[l]
        q = z[..., :ATTN_WIDTH].reshape(B, S, DIFF_HEADS, 2, DIFF_QKDIM)
        k = z[..., ATTN_WIDTH:2 * ATTN_WIDTH].reshape(B, S, DIFF_HEADS, 2, DIFF_QKDIM)
        v = z[..., 2 * ATTN_WIDTH:3 * ATTN_WIDTH].reshape(B, S, DIFF_HEADS, DIFF_VDIM)
        u = z[..., 3 * ATTN_WIDTH:].reshape(B, S, SSM_GROUPS, SSM_GROUP)
        q = partial_rope(q, cos, sin)
        k = partial_rope(k, cos, sin)
        lam = (jnp.exp(jnp.sum(lam_q1[l].astype(jnp.float32) * lam_k1[l].astype(jnp.float32)))
               - jnp.exp(jnp.sum(lam_q2[l].astype(jnp.float32) * lam_k2[l].astype(jnp.float32)))
               + lam_init)
        att = diff_attention(q, k, v, lam)
        att = (rmsnorm(att, attn_head_g[l]) * (1.0 - lam_init)).reshape(B, S, ATTN_WIDTH)
        y = s5_ssm(u, ssm_a_re[l], ssm_a_im[l], ssm_log_dt[l], ssm_b_re[l], ssm_b_im[l],
                   ssm_c_re[l], ssm_c_im[l], ssm_d[l]).reshape(B, S, SSM_WIDTH)
        g = jax.nn.gelu(y, approximate=False)
        g = g * jax.nn.sigmoid(g @ glu_w[l] + glu_b[l])
        ssm_out = rmsnorm(g, ssm_out_g[l])
        h = h + jnp.concatenate([att, ssm_out], axis=-1) @ w_out[l]
        hn = rmsnorm(h, xattn_norm_g[l])
        mn = rmsnorm(mem, mem_norm_g[l])
        h = h + cross_attention(hn, mn, xattn_wq[l], xattn_wkv[l], xattn_wo[l])
        h = h + peer(rmsnorm(h, ffn_norm_g[l]), peer_wq[l], peer_k1[l], peer_k2[l],
                     peer_u[l], peer_v[l])
    return rmsnorm(h, final_norm_g)
```

```python
import dataclasses
import functools
import math

import jax
import jax.numpy as jnp
from jax import lax
from jax.experimental import pallas as pl
from jax.experimental.pallas import tpu as pltpu
from jax.experimental.pallas import tpu_sc as plsc

f32 = jnp.float32
bf16 = jnp.bfloat16
i32 = jnp.int32

LANES = 128
SUBLANES = 8
VMEM_LIMIT = 56 * 1024 * 1024
NEG = -1e30
EPS = 1e-6


@dataclasses.dataclass(frozen=True)
class Cfg:
    d_model: int = 2048
    batch: int = 8
    seq: int = 2048
    n_mem: int = 256
    diff_heads: int = 8
    ssm_group: int = 16
    ssm_state: int = 64
    xattn_heads: int = 4
    xattn_head_dim: int = 128
    peer_heads: int = 8
    peer_keys: int = 128
    peer_qdim: int = 256
    peer_topk: int = 16
    rope_theta: float = 500000.0
    lam_init: float = 0.8 - 0.6 * math.exp(-0.3 * 0)
    tm: int = 256
    tq: int = 256
    scan_chunk: int = 128
    route_tm: int = 256
    gather_window: int = 16
    peer_chunks: int = 8

    @property
    def attn_width(self):
        return self.d_model // 2

    @property
    def ssm_width(self):
        return self.d_model - self.attn_width

    @property
    def diff_vdim(self):
        return self.attn_width // self.diff_heads

    @property
    def diff_qkdim(self):
        return self.diff_vdim // 2

    @property
    def rot_dim(self):
        return self.diff_qkdim // 4

    @property
    def ssm_groups(self):
        return self.ssm_width // self.ssm_group

    @property
    def xattn_width(self):
        return self.xattn_heads * self.xattn_head_dim

    @property
    def tokens(self):
        return self.batch * self.seq

    @property
    def slots(self):
        return self.peer_heads * self.peer_topk


def _cparams(*sem):
    return pltpu.CompilerParams(dimension_semantics=sem, vmem_limit_bytes=VMEM_LIMIT)


def _resident(shape):
    nd = len(shape)
    return pl.BlockSpec(shape, lambda *_: (0,) * nd, pipeline_mode=pl.Buffered(1))


def _rms(x, g):
    return x * lax.rsqrt(jnp.mean(x * x, axis=-1, keepdims=True) + EPS) * g


def _gelu(x):
    return 0.5 * x * (1.0 + lax.erf(x * (2.0 ** -0.5)))


def _in_proj_kernel(x_ref, pos_ref, g_ref, freq_ref, w_ref, qk_ref, v_ref, u_ref, *, cfg):
    n_qk, n_v = 2 * cfg.attn_width, cfg.attn_width
    half = cfg.rot_dim // 2
    xn = _rms(x_ref[...], g_ref[...]).astype(bf16)
    ang = pos_ref[...] * freq_ref[...]
    cos, sin = jnp.cos(ang), jnp.sin(ang)
    lane = lax.broadcasted_iota(i32, (1, LANES), 1) % cfg.diff_qkdim
    sin_lo = jnp.where(lane < half, -sin, 0.0)
    sin_hi = jnp.where((lane >= half) & (lane < 2 * half), sin, 0.0)
    cw = 2 * LANES
    for c in range((n_qk + n_v + cfg.ssm_width) // cw):
        col = c * cw
        z = jnp.dot(xn, w_ref[:, col:col + cw], preferred_element_type=f32)
        if col < n_qk:
            for k in range(cw // LANES):
                zk = z[:, k * LANES:(k + 1) * LANES]
                zk = zk * cos + pltpu.roll(zk, LANES - half, 1) * sin_lo + pltpu.roll(zk, half, 1) * sin_hi
                qk_ref[:, col + k * LANES:col + (k + 1) * LANES] = zk.astype(bf16)
        elif col < n_qk + n_v:
            v_ref[:, col - n_qk:col - n_qk + cw] = z.astype(bf16)
        else:
            u_ref[:, col - n_qk - n_v:col - n_qk - n_v + cw] = z


def _in_proj(cfg, x2, pos, g, freq, w):
    T, D = x2.shape
    tm = cfg.tm
    n_qk, n_v, n_u = 2 * cfg.attn_width, cfg.attn_width, cfg.ssm_width
    row = lambda i: (i, 0)
    return pl.pallas_call(
        functools.partial(_in_proj_kernel, cfg=cfg),
        grid=(T // tm,),
        in_specs=[pl.BlockSpec((tm, D), row), pl.BlockSpec((tm, 1), row),
                  _resident((1, D)), _resident((1, LANES)), _resident(w.shape)],
        out_specs=[pl.BlockSpec((tm, n_qk), row), pl.BlockSpec((tm, n_v), row), pl.BlockSpec((tm, n_u), row)],
        out_shape=[jax.ShapeDtypeStruct((T, n_qk), bf16), jax.ShapeDtypeStruct((T, n_v), bf16),
                   jax.ShapeDtypeStruct((T, n_u), f32)],
        compiler_params=_cparams("parallel"),
        name="in_proj",
    )(x2, pos, g, freq, w)


def _diff_attn_kernel(lam_ref, q_ref, k_ref, v_ref, g_ref, o_ref, m_ref, l_ref, acc_ref, *, cfg):
    tq = cfg.tq
    d = cfg.diff_qkdim
    i = pl.program_id(2)
    lane = lax.broadcasted_iota(i32, (1, LANES), 1)
    q = q_ref[...].astype(f32) * (d ** -0.5)
    qs = (jnp.where(lane < d, q, 0.0).astype(bf16), jnp.where(lane >= d, q, 0.0).astype(bf16))
    m_ref[...] = jnp.full(m_ref.shape, NEG, f32)
    l_ref[...] = jnp.zeros(l_ref.shape, f32)
    acc_ref[...] = jnp.zeros(acc_ref.shape, f32)
    causal = (lax.broadcasted_iota(i32, (tq, tq), 1) <= lax.broadcasted_iota(i32, (tq, tq), 0))

    def step(j, masked):
        start = pl.multiple_of(j * tq, tq)
        kb = k_ref[pl.ds(start, tq), :]
        vb = v_ref[pl.ds(start, tq), :]
        for c in range(2):
            s = lax.dot_general(qs[c], kb, (((1,), (1,)), ((), ())), preferred_element_type=f32)
            if masked:
                s = jnp.where(causal, s, NEG)
            m_old = m_ref[c]
            m_new = jnp.maximum(m_old, jnp.max(s, axis=1, keepdims=True))
            alpha = jnp.exp(m_old - m_new)
            p = jnp.exp(s - m_new)
            l_ref[c] = alpha * l_ref[c] + jnp.sum(p, axis=1, keepdims=True)
            acc_ref[c] = alpha * acc_ref[c] + jnp.dot(p.astype(bf16), vb, preferred_element_type=f32)
            m_ref[c] = m_new

    def body(j, carry):
        step(j, False)
        return carry

    lax.fori_loop(0, i, body, 0)
    step(i, True)

    lp = lam_ref[...]
    lam = (jnp.exp(jnp.sum(lp[0:1] * lp[1:2], axis=1, keepdims=True))
           - jnp.exp(jnp.sum(lp[2:3] * lp[3:4], axis=1, keepdims=True)) + cfg.lam_init)
    o = acc_ref[0] / l_ref[0] - lam * (acc_ref[1] / l_ref[1])
    o_ref[...] = (_rms(o, g_ref[...]) * (1.0 - cfg.lam_init)).astype(o_ref.dtype)


def _diff_attn(cfg, lam_p, qk, v, g):
    T = qk.shape[0]
    B, S, H, tq = cfg.batch, cfg.seq, cfg.diff_heads, cfg.tq
    nq = S // tq
    return pl.pallas_call(
        functools.partial(_diff_attn_kernel, cfg=cfg),
        grid=(B, H, nq),
        in_specs=[_resident(lam_p.shape),
                  pl.BlockSpec((tq, LANES), lambda b, h, i: (b * nq + i, h)),
                  pl.BlockSpec((S, LANES), lambda b, h, i: (b, H + h)),
                  pl.BlockSpec((S, LANES), lambda b, h, i: (b, h)),
                  _resident((1, LANES))],
        out_specs=pl.BlockSpec((tq, LANES), lambda b, h, i: (b * nq + i, h)),
        out_shape=jax.ShapeDtypeStruct((T, cfg.attn_width), bf16),
        scratch_shapes=[pltpu.VMEM((2, tq, 1), f32), pltpu.VMEM((2, tq, 1), f32),
                        pltpu.VMEM((2, tq, LANES), f32)],
        compiler_params=_cparams("parallel", "parallel", "parallel"),
        name="diff_attn",
    )(lam_p, qk, qk, v, g)


def _s5_scan_kernel(u_ref, are_ref, aim_ref, ldt_ref, bre_ref, bim_ref, cre_ref, cim_ref, d_ref, y_ref,
                    ab_ref, bbar_re_ref, bbar_im_ref, st_re_ref, st_im_ref, bu_re_ref, bu_im_ref, *, cfg):
    B, L = cfg.batch, cfg.scan_chunk
    nt = bu_re_ref.shape[0]
    t = pl.program_id(1)

    @pl.when(t == 0)
    def _():
        dt = jnp.exp(ldt_ref[0])
        lre, lim = are_ref[0], aim_ref[0]
        mag = jnp.exp(lre * dt)
        ang = lim * dt
        ab_re, ab_im = mag * jnp.cos(ang), mag * jnp.sin(ang)
        den = lre * lre + lim * lim
        f_re = ((ab_re - 1.0) * lre + ab_im * lim) / den
        f_im = (ab_im * lre - (ab_re - 1.0) * lim) / den
        ab_ref[0] = jnp.broadcast_to(ab_re, ab_ref.shape[1:])
        ab_ref[1] = jnp.broadcast_to(ab_im, ab_ref.shape[1:])
        br, bi = bre_ref[0], bim_ref[0]
        bbar_re_ref[...] = (f_re * br - f_im * bi).astype(bf16)
        bbar_im_ref[...] = (f_re * bi + f_im * br).astype(bf16)
        st_re_ref[...] = jnp.zeros(st_re_ref.shape, f32)
        st_im_ref[...] = jnp.zeros(st_im_ref.shape, f32)

    u = u_ref[...].reshape(B * L, u_ref.shape[2])
    ub = u.astype(bf16)
    bu_re = jnp.dot(ub, bbar_re_ref[...], preferred_element_type=f32)
    bu_im = jnp.dot(ub, bbar_im_ref[...], preferred_element_type=f32)
    for j in range(nt):
        bu_re_ref[j] = bu_re[:, j * LANES:(j + 1) * LANES]
        bu_im_ref[j] = bu_im[:, j * LANES:(j + 1) * LANES]

    a_re = [ab_ref[0, :, j * LANES:(j + 1) * LANES] for j in range(nt)]
    a_im = [ab_ref[1, :, j * LANES:(j + 1) * LANES] for j in range(nt)]

    def body(s, carry):
        xr, xi = carry
        nr, ni = [], []
        for j in range(nt):
            rows = pl.ds(s, B, stride=L)
            r = a_re[j] * xr[j] - a_im[j] * xi[j] + bu_re_ref[j, rows, :]
            m = a_re[j] * xi[j] + a_im[j] * xr[j] + bu_im_ref[j, rows, :]
            bu_re_ref[j, rows, :] = r
            bu_im_ref[j, rows, :] = m
            nr.append(r)
            ni.append(m)
        return tuple(nr), tuple(ni)

    init = (tuple(st_re_ref[j] for j in range(nt)), tuple(st_im_ref[j] for j in range(nt)))
    xr, xi = lax.fori_loop(0, L, body, init, unroll=8)
    for j in range(nt):
        st_re_ref[j] = xr[j]
        st_im_ref[j] = xi[j]

    xs_re = jnp.concatenate([bu_re_ref[j] for j in range(nt)], axis=1).astype(bf16)
    xs_im = jnp.concatenate([bu_im_ref[j] for j in range(nt)], axis=1).astype(bf16)
    y = (jnp.dot(xs_re, cre_ref[0].astype(bf16), preferred_element_type=f32)
         - jnp.dot(xs_im, cim_ref[0].astype(bf16), preferred_element_type=f32)
         + d_ref[0] * u)
    y_ref[...] = y.reshape(y_ref.shape)


def _block_diag(w, blocks):
    G, r, c = w.shape
    w4 = w.reshape(G // blocks, blocks, r, c)
    eye = jnp.eye(blocks, dtype=w.dtype)
    out = w4[:, :, :, None, :] * eye[None, :, None, :, None]
    return out.reshape(G // blocks, blocks * r, blocks * c)


def _s5_scan(cfg, u3, a_re, a_im, log_dt, b_re, b_im, c_re, c_im, d_skip):
    B, S, W = u3.shape
    G, P, Hc, L = cfg.ssm_groups, cfg.ssm_state, cfg.ssm_group, cfg.scan_chunk
    gps = min(G, 2 * LANES // Hc)
    ns = G // gps
    wu, wn = gps * Hc, gps * P
    nt = wn // LANES
    are = a_re.reshape(ns, 1, wn)
    aim = a_im.reshape(ns, 1, wn)
    ldt = jnp.repeat(log_dt, P).reshape(ns, 1, wn)
    bre = _block_diag(b_re.transpose(0, 2, 1), gps)
    bim = _block_diag(b_im.transpose(0, 2, 1), gps)
    cre = _block_diag(c_re.transpose(0, 2, 1), gps)
    cim = _block_diag(c_im.transpose(0, 2, 1), gps)
    dsk = d_skip.reshape(ns, 1, wu)
    slab = lambda s, t: (s, 0, 0)
    return pl.pallas_call(
        functools.partial(_s5_scan_kernel, cfg=cfg),
        grid=(ns, S // L),
        in_specs=[pl.BlockSpec((B, L, wu), lambda s, t: (0, t, s)),
                  pl.BlockSpec((1, 1, wn), slab), pl.BlockSpec((1, 1, wn), slab), pl.BlockSpec((1, 1, wn), slab),
                  pl.BlockSpec((1, wu, wn), slab), pl.BlockSpec((1, wu, wn), slab),
                  pl.BlockSpec((1, wn, wu), slab), pl.BlockSpec((1, wn, wu), slab),
                  pl.BlockSpec((1, 1, wu), slab)],
        out_specs=pl.BlockSpec((B, L, wu), lambda s, t: (0, t, s)),
        out_shape=jax.ShapeDtypeStruct((B, S, W), f32),
        scratch_shapes=[pltpu.VMEM((2, SUBLANES, wn), f32),
                        pltpu.VMEM((wu, wn), bf16), pltpu.VMEM((wu, wn), bf16),
                        pltpu.VMEM((nt, B, LANES), f32), pltpu.VMEM((nt, B, LANES), f32),
                        pltpu.VMEM((nt, B * L, LANES), f32), pltpu.VMEM((nt, B * L, LANES), f32)],
        compiler_params=_cparams("arbitrary", "arbitrary"),
        name="s5_scan",
    )(u3, are, aim, ldt, bre, bim, cre, cim, dsk)


def _s5_glu_kernel(y_ref, w_ref, b_ref, g_ref, o_ref):
    g = _gelu(y_ref[...])
    z = jnp.dot(g.astype(bf16), w_ref[...], preferred_element_type=f32) + b_ref[...]
    g = g * jax.nn.sigmoid(z)
    o_ref[...] = _rms(g, g_ref[...]).astype(o_ref.dtype)


def _s5_glu(cfg, y2, w, b, g):
    T, W = y2.shape
    tm = cfg.tm
    row = lambda i: (i, 0)
    return pl.pallas_call(
        _s5_glu_kernel,
        grid=(T // tm,),
        in_specs=[pl.BlockSpec((tm, W), row), _resident(w.shape), _resident((1, W)), _resident((1, W))],
        out_specs=pl.BlockSpec((tm, W), row),
        out_shape=jax.ShapeDtypeStruct((T, W), bf16),
        compiler_params=_cparams("parallel"),
        name="s5_glu",
    )(y2, w, b, g)


def _out_proj_kernel(x_ref, att_ref, ssm_ref, w_ref, g_ref, h_ref, hn_ref, *, cfg):
    aw = cfg.attn_width
    h = (x_ref[...]
         + jnp.dot(att_ref[...], w_ref[:aw, :], preferred_element_type=f32)
         + jnp.dot(ssm_ref[...], w_ref[aw:, :], preferred_element_type=f32))
    h_ref[...] = h
    hn_ref[...] = _rms(h, g_ref[...]).astype(hn_ref.dtype)


def _out_proj(cfg, x2, att, ssm, w, g):
    T, D = x2.shape
    tm = cfg.tm
    row = lambda i: (i, 0)
    return pl.pallas_call(
        functools.partial(_out_proj_kernel, cfg=cfg),
        grid=(T // tm,),
        in_specs=[pl.BlockSpec((tm, D), row), pl.BlockSpec((tm, cfg.attn_width), row),
                  pl.BlockSpec((tm, cfg.ssm_width), row), _resident(w.shape), _resident((1, D))],
        out_specs=[pl.BlockSpec((tm, D), row), pl.BlockSpec((tm, D), row)],
        out_shape=[jax.ShapeDtypeStruct((T, D), f32), jax.ShapeDtypeStruct((T, D), bf16)],
        compiler_params=_cparams("parallel"),
        name="out_proj",
    )(x2, att, ssm, w, g)


def _mem_kv_kernel(m_ref, g_ref, w_ref, o_ref):
    mn = _rms(m_ref[...], g_ref[...]).astype(bf16)
    o_ref[...] = jnp.dot(mn, w_ref[...], preferred_element_type=f32).astype(o_ref.dtype)


def _mem_kv(cfg, mem2, g, w):
    R, D = mem2.shape
    tm = cfg.n_mem
    row = lambda i: (i, 0)
    return pl.pallas_call(
        _mem_kv_kernel,
        grid=(R // tm,),
        in_specs=[pl.BlockSpec((tm, D), row), _resident((1, D)), _resident(w.shape)],
        out_specs=pl.BlockSpec((tm, w.shape[1]), row),
        out_shape=jax.ShapeDtypeStruct((R, w.shape[1]), bf16),
        compiler_params=_cparams("parallel"),
        name="mem_kv",
    )(mem2, g, w)


def _xattn_kernel(h_ref, hn_ref, kv_ref, wq_ref, wo_ref, g_ref, pw_ref, h2_ref, qp_ref, *, cfg):
    nh, hd, xw = cfg.xattn_heads, cfg.xattn_head_dim, cfg.xattn_width
    q = jnp.dot(hn_ref[...], wq_ref[...], preferred_element_type=f32).astype(bf16)
    outs = []
    for h in range(nh):
        qh = q[:, h * hd:(h + 1) * hd]
        kh = kv_ref[:, h * hd:(h + 1) * hd]
        vh = kv_ref[:, xw + h * hd:xw + (h + 1) * hd]
        s = lax.dot_general(qh, kh, (((1,), (1,)), ((), ())), preferred_element_type=f32) * (hd ** -0.5)
        p = jnp.exp(s - jnp.max(s, axis=1, keepdims=True))
        p = p / jnp.sum(p, axis=1, keepdims=True)
        outs.append(jnp.dot(p.astype(bf16), vh, preferred_element_type=f32).astype(bf16))
    o = jnp.concatenate(outs, axis=1)
    h2 = h_ref[...] + jnp.dot(o, wo_ref[...], preferred_element_type=f32)
    h2_ref[...] = h2
    hp = _rms(h2, g_ref[...]).astype(bf16)
    qp_ref[...] = jnp.dot(hp, pw_ref[...], preferred_element_type=f32).astype(qp_ref.dtype)


def _xattn(cfg, h1, hn, kv, wq, wo, g, pw):
    T, D = h1.shape
    tm, M = cfg.tm, cfg.n_mem
    per_b = cfg.seq // tm
    row = lambda i: (i, 0)
    return pl.pallas_call(
        functools.partial(_xattn_kernel, cfg=cfg),
        grid=(T // tm,),
        in_specs=[pl.BlockSpec((tm, D), row), pl.BlockSpec((tm, D), row),
                  pl.BlockSpec((M, kv.shape[1]), lambda i: (i // per_b, 0)),
                  _resident(wq.shape), _resident(wo.shape), _resident((1, D)), _resident(pw.shape)],
        out_specs=[pl.BlockSpec((tm, D), row), pl.BlockSpec((tm, pw.shape[1]), row)],
        out_shape=[jax.ShapeDtypeStruct((T, D), f32), jax.ShapeDtypeStruct((T, pw.shape[1]), bf16)],
        compiler_params=_cparams("parallel"),
        name="xattn",
    )(h1, hn, kv, wq, wo, g, pw)


def _top_rows(s, k, payload=None):
    n = s.shape[0]
    rows = lax.broadcasted_iota(i32, s.shape, 0)
    vals, picks = [], []
    for _ in range(k):
        m = jnp.max(s, axis=0, keepdims=True)
        idx = jnp.min(jnp.where(s == m, rows, n), axis=0, keepdims=True)
        sel = rows == idx
        vals.append(m)
        picks.append(idx if payload is None else jnp.max(jnp.where(sel, payload, -1), axis=0, keepdims=True))
        s = jnp.where(sel, NEG, s)
    return vals, picks


def _peer_route_kernel(qp_ref, k1_ref, k2_ref, e_ref, g_ref, *, cfg):
    K, nk, half = cfg.peer_topk, cfg.peer_keys, cfg.peer_qdim // 2
    tm = qp_ref.shape[0]
    dn = (((1,), (1,)), ((), ()))
    s1 = lax.dot_general(k1_ref[0], qp_ref[:, :half], dn, preferred_element_type=f32)
    s2 = lax.dot_general(k2_ref[0], qp_ref[:, half:], dn, preferred_element_type=f32)
    t1, i1 = _top_rows(s1, K)
    t2, i2 = _top_rows(s2, K)
    t1, i1 = jnp.concatenate(t1, axis=0), jnp.concatenate(i1, axis=0)
    t2, i2 = jnp.concatenate(t2, axis=0), jnp.concatenate(i2, axis=0)
    cand = (t1[:, None, :] + t2[None, :, :]).reshape(K * K, tm)
    expert = (i1[:, None, :] * nk + i2[None, :, :]).reshape(K * K, tm)
    ts, es = _top_rows(cand, K, payload=expert)
    ts = jnp.concatenate(ts, axis=0)
    p = jnp.exp(ts - ts[0:1])
    g_ref[...] = p / jnp.sum(p, axis=0, keepdims=True)
    e_ref[...] = jnp.concatenate(es, axis=0)


def _peer_route(cfg, qp, k1, k2):
    T = qp.shape[0]
    tm, H, K, Q = cfg.route_tm, cfg.peer_heads, cfg.peer_topk, cfg.peer_qdim
    out = pl.BlockSpec((K, tm), lambda i, h: (h, i))
    return pl.pallas_call(
        functools.partial(_peer_route_kernel, cfg=cfg),
        grid=(T // tm, H),
        in_specs=[pl.BlockSpec((tm, Q), lambda i, h: (i, h)),
                  pl.BlockSpec((1,) + k1.shape[1:], lambda i, h: (h, 0, 0)),
                  pl.BlockSpec((1,) + k2.shape[1:], lambda i, h: (h, 0, 0))],
        out_specs=[out, out],
        out_shape=[jax.ShapeDtypeStruct((H * K, T), i32), jax.ShapeDtypeStruct((H * K, T), f32)],
        compiler_params=_cparams("parallel", "parallel"),
        name="peer_route",
    )(qp, k1, k2)


def _sc_gather(cfg, table, idx):
    n = idx.shape[0]
    d = table.shape[1]
    win = cfg.gather_window
    mesh = plsc.VectorSubcoreMesh(core_axis_name="core", subcore_axis_name="subcore")
    workers = mesh.num_cores * mesh.num_subcores
    per = n // workers
    assert per * workers == n and per % win == 0

    @functools.partial(pl.kernel, out_type=jax.ShapeDtypeStruct((n, d), table.dtype), mesh=mesh,
                       scratch_types=[pltpu.VMEM((per,), i32), pltpu.VMEM((win, d), table.dtype)])
    def gather_kernel(tab_hbm, idx_hbm, out_hbm, idx_vmem, buf):
        wid = lax.axis_index("core") * mesh.num_subcores + lax.axis_index("subcore")
        base = wid * per
        pltpu.sync_copy(idx_hbm.at[pl.ds(base, per)], idx_vmem)

        @pl.loop(0, per // win)
        def _(s):
            pltpu.sync_copy(tab_hbm.at[idx_vmem.at[pl.ds(s * win, win)]], buf)
            pltpu.sync_copy(buf, out_hbm.at[pl.ds(base + s * win, win)])

    return gather_kernel(table, idx)


def _peer_apply_kernel(h_ref, gt_ref, ug_ref, vg_ref, gn_ref, gf_ref, o_ref, act_ref, w_ref, *, cfg):
    E = cfg.slots
    R = SUBLANES
    h2 = h_ref[...]
    x = _rms(h2, gn_ref[...])

    def act_body(e, carry):
        rows = pl.ds(pl.multiple_of(e * R, R), R)
        act_ref[rows, :] = jnp.sum(ug_ref[rows, :] * x, axis=1, keepdims=True)
        return carry

    lax.fori_loop(0, E, act_body, 0, unroll=4)
    slot = lax.broadcasted_iota(i32, (E, R, LANES), 0)
    lane = lax.broadcasted_iota(i32, (E, R, LANES), 2)
    diag = slot == lane
    act_col = act_ref[...].reshape(E, R, 1)
    act = jnp.sum(jnp.where(diag, act_col, 0.0), axis=0)
    w = gt_ref[...] * _gelu(act)
    w_col = jnp.sum(jnp.where(diag, w[None, :, :], 0.0), axis=2, keepdims=True)
    w_ref[...] = w_col.reshape(E * R, 1)

    def out_body(e, acc):
        rows = pl.ds(pl.multiple_of(e * R, R), R)
        return acc + vg_ref[rows, :] * w_ref[rows, :]

    y = lax.fori_loop(0, E, out_body, jnp.zeros(h2.shape, f32), unroll=4)
    o_ref[...] = _rms(h2 + y, gf_ref[...])


def _peer_apply(cfg, h2, gates, ug, vg, gn, gf):
    T, D = h2.shape
    E, R = cfg.slots, SUBLANES
    assert E == LANES
    row = lambda i: (i, 0)
    return pl.pallas_call(
        functools.partial(_peer_apply_kernel, cfg=cfg),
        grid=(T // R,),
        in_specs=[pl.BlockSpec((R, D), row), pl.BlockSpec((R, E), row),
                  pl.BlockSpec((E * R, D), row), pl.BlockSpec((E * R, D), row),
                  _resident((1, D)), _resident((1, D))],
        out_specs=pl.BlockSpec((R, D), row),
        out_shape=jax.ShapeDtypeStruct((T, D), f32),
        scratch_shapes=[pltpu.VMEM((E * R, 1), f32), pltpu.VMEM((E * R, 1), f32)],
        compiler_params=_cparams("parallel"),
        name="peer_apply",
    )(h2, gates, ug, vg, gn, gf)


def _block(cfg, gather_fn, x, mem, positions, mix_norm_g, w_in, lam_q1, lam_k1, lam_q2, lam_k2, attn_head_g,
           ssm_a_re, ssm_a_im, ssm_log_dt, ssm_b_re, ssm_b_im, ssm_c_re, ssm_c_im, ssm_d, glu_w, glu_b,
           ssm_out_g, w_out, xattn_norm_g, mem_norm_g, xattn_wq, xattn_wkv, xattn_wo, ffn_norm_g,
           peer_wq, peer_k1, peer_k2, peer_u, peer_v, final_norm_g):
    B, S, D = x.shape
    T = B * S
    l = 0
    row = lambda a: a.reshape(1, -1)
    x2 = x.reshape(T, D)
    pos = positions.reshape(T, 1).astype(f32)
    freqs = cfg.rope_theta ** (-jnp.arange(0, cfg.rot_dim, 2, dtype=f32) / cfg.rot_dim)
    lane = jnp.arange(LANES) % cfg.diff_qkdim
    freq_row = jnp.where(lane < cfg.rot_dim, freqs[lane % (cfg.rot_dim // 2)], 0.0).reshape(1, LANES)

    qk, v, u = _in_proj(cfg, x2, pos, row(mix_norm_g[l]), freq_row, w_in[l].astype(bf16))
    lam_p = jnp.stack([lam_q1[l], lam_k1[l], lam_q2[l], lam_k2[l]])
    att = _diff_attn(cfg, lam_p, qk, v, row(attn_head_g[l]))
    y = _s5_scan(cfg, u.reshape(B, S, cfg.ssm_width), ssm_a_re[l], ssm_a_im[l], ssm_log_dt[l], ssm_b_re[l],
                 ssm_b_im[l], ssm_c_re[l], ssm_c_im[l], ssm_d[l])
    ssm = _s5_glu(cfg, y.reshape(T, cfg.ssm_width), glu_w[l].astype(bf16), row(glu_b[l]), row(ssm_out_g[l]))
    h1, hn = _out_proj(cfg, x2, att, ssm, w_out[l].astype(bf16), row(xattn_norm_g[l]))
    kv = _mem_kv(cfg, mem.reshape(B * cfg.n_mem, D), row(mem_norm_g[l]), xattn_wkv[l].astype(bf16))
    h2, qp = _xattn(cfg, h1, hn, kv, xattn_wq[l].astype(bf16), xattn_wo[l].astype(bf16), row(ffn_norm_g[l]),
                    peer_wq[l].astype(bf16))
    experts_t, gates_t = _peer_route(cfg, qp, peer_k1[l].astype(bf16), peer_k2[l].astype(bf16))

    E, R = cfg.slots, SUBLANES
    idx = experts_t.reshape(E, T // R, R).transpose(1, 0, 2).reshape(-1)
    gates = gates_t.T
    nc = cfg.peer_chunks
    tc = T // nc
    outs = []
    for c in range(nc):
        idx_c = idx[c * tc * E:(c + 1) * tc * E]
        ug = gather_fn(peer_u[l], idx_c)
        vg = gather_fn(peer_v[l], idx_c)
        outs.append(_peer_apply(cfg, h2[c * tc:(c + 1) * tc], gates[c * tc:(c + 1) * tc], ug, vg,
                                row(ffn_norm_g[l]), row(final_norm_g)))
    return jnp.concatenate(outs, axis=0).reshape(B, S, D)


def kernel(x, mem, positions, mix_norm_g, w_in, lam_q1, lam_k1, lam_q2, lam_k2, attn_head_g, ssm_a_re, ssm_a_im, ssm_log_dt, ssm_b_re, ssm_b_im, ssm_c_re, ssm_c_im, ssm_d, glu_w, glu_b, ssm_out_g, w_out, xattn_norm_g, mem_norm_g, xattn_wq, xattn_wkv, xattn_wo, ffn_norm_g, peer_wq, peer_k1, peer_k2, peer_u, peer_v, final_norm_g):
    cfg = Cfg()
    return _block(cfg, functools.partial(_sc_gather, cfg), x, mem, positions, mix_norm_g, w_in, lam_q1, lam_k1,
                  lam_q2, lam_k2, attn_head_g, ssm_a_re, ssm_a_im, ssm_log_dt, ssm_b_re, ssm_b_im, ssm_c_re,
                  ssm_c_im, ssm_d, glu_w, glu_b, ssm_out_g, w_out, xattn_norm_g, mem_norm_g, xattn_wq,
                  xattn_wkv, xattn_wo, ffn_norm_g, peer_wq, peer_k1, peer_k2, peer_u, peer_v, final_norm_g)
```

```python
import dataclasses
import functools
import math

import jax
import jax.numpy as jnp
from jax import lax
from jax.experimental import pallas as pl
from jax.experimental.pallas import tpu as pltpu
from jax.experimental.pallas import tpu_sc as plsc

f32 = jnp.float32
bf16 = jnp.bfloat16
i32 = jnp.int32

LANES = 128
SUBLANES = 8
VMEM_LIMIT = 56 * 1024 * 1024
NEG = -1e30
EPS = 1e-6


@dataclasses.dataclass(frozen=True)
class Cfg:
    d_model: int = 2048
    batch: int = 8
    seq: int = 2048
    n_mem: int = 256
    diff_heads: int = 8
    ssm_group: int = 16
    ssm_state: int = 64
    xattn_heads: int = 4
    xattn_head_dim: int = 128
    peer_heads: int = 8
    peer_keys: int = 128
    peer_qdim: int = 256
    peer_topk: int = 16
    rope_theta: float = 500000.0
    lam_init: float = 0.8 - 0.6 * math.exp(-0.3 * 0)
    tm: int = 256
    tq: int = 256
    scan_chunk: int = 128
    route_tm: int = 256
    gather_window: int = 8
    gather_bufs: int = 4
    apply_groups: int = 2
    peer_chunks: int = 8

    @property
    def attn_width(self):
        return self.d_model // 2

    @property
    def ssm_width(self):
        return self.d_model - self.attn_width

    @property
    def diff_vdim(self):
        return self.attn_width // self.diff_heads

    @property
    def diff_qkdim(self):
        return self.diff_vdim // 2

    @property
    def rot_dim(self):
        return self.diff_qkdim // 4

    @property
    def ssm_groups(self):
        return self.ssm_width // self.ssm_group

    @property
    def xattn_width(self):
        return self.xattn_heads * self.xattn_head_dim

    @property
    def tokens(self):
        return self.batch * self.seq

    @property
    def slots(self):
        return self.peer_heads * self.peer_topk


def _cparams(*sem):
    return pltpu.CompilerParams(dimension_semantics=sem, vmem_limit_bytes=VMEM_LIMIT)


def _resident(shape):
    nd = len(shape)
    return pl.BlockSpec(shape, lambda *_: (0,) * nd, pipeline_mode=pl.Buffered(1))


def _rms(x, g):
    return x * lax.rsqrt(jnp.mean(x * x, axis=-1, keepdims=True) + EPS) * g


def _gelu(x):
    return 0.5 * x * (1.0 + lax.erf(x * (2.0 ** -0.5)))


def _in_proj_kernel(x_ref, pos_ref, g_ref, freq_ref, w_ref, qk_ref, v_ref, u_ref, *, cfg):
    n_qk, n_v = 2 * cfg.attn_width, cfg.attn_width
    half = cfg.rot_dim // 2
    xn = _rms(x_ref[...], g_ref[...]).astype(bf16)
    ang = pos_ref[...] * freq_ref[...]
    cos, sin = jnp.cos(ang), jnp.sin(ang)
    lane = lax.broadcasted_iota(i32, (1, LANES), 1) % cfg.diff_qkdim
    sin_lo = jnp.where(lane < half, -sin, 0.0)
    sin_hi = jnp.where((lane >= half) & (lane < 2 * half), sin, 0.0)
    cw = 2 * LANES
    for c in range((n_qk + n_v + cfg.ssm_width) // cw):
        col = c * cw
        z = jnp.dot(xn, w_ref[:, col:col + cw], preferred_element_type=f32)
        if col < n_qk:
            for k in range(cw // LANES):
                zk = z[:, k * LANES:(k + 1) * LANES]
                zk = zk * cos + pltpu.roll(zk, LANES - half, 1) * sin_lo + pltpu.roll(zk, half, 1) * sin_hi
                qk_ref[:, col + k * LANES:col + (k + 1) * LANES] = zk.astype(bf16)
        elif col < n_qk + n_v:
            v_ref[:, col - n_qk:col - n_qk + cw] = z.astype(bf16)
        else:
            u_ref[:, col - n_qk - n_v:col - n_qk - n_v + cw] = z


def _in_proj(cfg, x2, pos, g, freq, w):
    T, D = x2.shape
    tm = cfg.tm
    n_qk, n_v, n_u = 2 * cfg.attn_width, cfg.attn_width, cfg.ssm_width
    row = lambda i: (i, 0)
    return pl.pallas_call(
        functools.partial(_in_proj_kernel, cfg=cfg),
        grid=(T // tm,),
        in_specs=[pl.BlockSpec((tm, D), row), pl.BlockSpec((tm, 1), row),
                  _resident((1, D)), _resident((1, LANES)), _resident(w.shape)],
        out_specs=[pl.BlockSpec((tm, n_qk), row), pl.BlockSpec((tm, n_v), row), pl.BlockSpec((tm, n_u), row)],
        out_shape=[jax.ShapeDtypeStruct((T, n_qk), bf16), jax.ShapeDtypeStruct((T, n_v), bf16),
                   jax.ShapeDtypeStruct((T, n_u), f32)],
        compiler_params=_cparams("parallel"),
        name="in_proj",
    )(x2, pos, g, freq, w)


def _diff_attn_kernel(lam_ref, q_ref, k_ref, v_ref, g_ref, o_ref, m_ref, l_ref, acc_ref, *, cfg):
    tq = cfg.tq
    d = cfg.diff_qkdim
    i = pl.program_id(2)
    lane = lax.broadcasted_iota(i32, (1, LANES), 1)
    q = q_ref[...].astype(f32) * (d ** -0.5)
    qs = (jnp.where(lane < d, q, 0.0).astype(bf16), jnp.where(lane >= d, q, 0.0).astype(bf16))
    m_ref[...] = jnp.full(m_ref.shape, NEG, f32)
    l_ref[...] = jnp.zeros(l_ref.shape, f32)
    acc_ref[...] = jnp.zeros(acc_ref.shape, f32)
    causal = (lax.broadcasted_iota(i32, (tq, tq), 1) <= lax.broadcasted_iota(i32, (tq, tq), 0))

    def step(j, masked):
        start = pl.multiple_of(j * tq, tq)
        kb = k_ref[pl.ds(start, tq), :]
        vb = v_ref[pl.ds(start, tq), :]
        for c in range(2):
            s = lax.dot_general(qs[c], kb, (((1,), (1,)), ((), ())), preferred_element_type=f32)
            if masked:
                s = jnp.where(causal, s, NEG)
            m_old = m_ref[c]
            m_new = jnp.maximum(m_old, jnp.max(s, axis=1, keepdims=True))
            alpha = jnp.exp(m_old - m_new)
            p = jnp.exp(s - m_new)
            l_ref[c] = alpha * l_ref[c] + jnp.sum(p, axis=1, keepdims=True)
            acc_ref[c] = alpha * acc_ref[c] + jnp.dot(p.astype(bf16), vb, preferred_element_type=f32)
            m_ref[c] = m_new

    def body(j, carry):
        step(j, False)
        return carry

    lax.fori_loop(0, i, body, 0)
    step(i, True)

    lp = lam_ref[...]
    lam = (jnp.exp(jnp.sum(lp[0:1] * lp[1:2], axis=1, keepdims=True))
           - jnp.exp(jnp.sum(lp[2:3] * lp[3:4], axis=1, keepdims=True)) + cfg.lam_init)
    o = acc_ref[0] / l_ref[0] - lam * (acc_ref[1] / l_ref[1])
    o_ref[...] = (_rms(o, g_ref[...]) * (1.0 - cfg.lam_init)).astype(o_ref.dtype)


def _diff_attn(cfg, lam_p, qk, v, g):
    T = qk.shape[0]
    B, S, H, tq = cfg.batch, cfg.seq, cfg.diff_heads, cfg.tq
    nq = S // tq
    return pl.pallas_call(
        functools.partial(_diff_attn_kernel, cfg=cfg),
        grid=(B, H, nq),
        in_specs=[_resident(lam_p.shape),
                  pl.BlockSpec((tq, LANES), lambda b, h, i: (b * nq + i, h)),
                  pl.BlockSpec((S, LANES), lambda b, h, i: (b, H + h)),
                  pl.BlockSpec((S, LANES), lambda b, h, i: (b, h)),
                  _resident((1, LANES))],
        out_specs=pl.BlockSpec((tq, LANES), lambda b, h, i: (b * nq + i, h)),
        out_shape=jax.ShapeDtypeStruct((T, cfg.attn_width), bf16),
        scratch_shapes=[pltpu.VMEM((2, tq, 1), f32), pltpu.VMEM((2, tq, 1), f32),
                        pltpu.VMEM((2, tq, LANES), f32)],
        compiler_params=_cparams("parallel", "parallel", "parallel"),
        name="diff_attn",
    )(lam_p, qk, qk, v, g)


def _s5_scan_kernel(u_ref, are_ref, aim_ref, ldt_ref, bre_ref, bim_ref, cre_ref, cim_ref, d_ref, y_ref,
                    ab_ref, bbar_re_ref, bbar_im_ref, st_re_ref, st_im_ref, bu_re_ref, bu_im_ref, *, cfg):
    B, L = cfg.batch, cfg.scan_chunk
    nt = bu_re_ref.shape[0]
    t = pl.program_id(1)

    @pl.when(t == 0)
    def _():
        dt = jnp.exp(ldt_ref[0])
        lre, lim = are_ref[0], aim_ref[0]
        mag = jnp.exp(lre * dt)
        ang = lim * dt
        ab_re, ab_im = mag * jnp.cos(ang), mag * jnp.sin(ang)
        den = lre * lre + lim * lim
        f_re = ((ab_re - 1.0) * lre + ab_im * lim) / den
        f_im = (ab_im * lre - (ab_re - 1.0) * lim) / den
        ab_ref[0] = jnp.broadcast_to(ab_re, ab_ref.shape[1:])
        ab_ref[1] = jnp.broadcast_to(ab_im, ab_ref.shape[1:])
        br, bi = bre_ref[0], bim_ref[0]
        bbar_re_ref[...] = (f_re * br - f_im * bi).astype(bf16)
        bbar_im_ref[...] = (f_re * bi + f_im * br).astype(bf16)
        st_re_ref[...] = jnp.zeros(st_re_ref.shape, f32)
        st_im_ref[...] = jnp.zeros(st_im_ref.shape, f32)

    u = u_ref[...].reshape(B * L, u_ref.shape[2])
    ub = u.astype(bf16)
    bu_re = jnp.dot(ub, bbar_re_ref[...], preferred_element_type=f32)
    bu_im = jnp.dot(ub, bbar_im_ref[...], preferred_element_type=f32)
    for j in range(nt):
        bu_re_ref[j] = bu_re[:, j * LANES:(j + 1) * LANES]
        bu_im_ref[j] = bu_im[:, j * LANES:(j + 1) * LANES]

    a_re = [ab_ref[0, :, j * LANES:(j + 1) * LANES] for j in range(nt)]
    a_im = [ab_ref[1, :, j * LANES:(j + 1) * LANES] for j in range(nt)]

    def body(s, carry):
        xr, xi = carry
        nr, ni = [], []
        for j in range(nt):
            rows = pl.ds(s, B, stride=L)
            r = a_re[j] * xr[j] - a_im[j] * xi[j] + bu_re_ref[j, rows, :]
            m = a_re[j] * xi[j] + a_im[j] * xr[j] + bu_im_ref[j, rows, :]
            bu_re_ref[j, rows, :] = r
            bu_im_ref[j, rows, :] = m
            nr.append(r)
            ni.append(m)
        return tuple(nr), tuple(ni)

    init = (tuple(st_re_ref[j] for j in range(nt)), tuple(st_im_ref[j] for j in range(nt)))
    xr, xi = lax.fori_loop(0, L, body, init, unroll=8)
    for j in range(nt):
        st_re_ref[j] = xr[j]
        st_im_ref[j] = xi[j]

    xs_re = jnp.concatenate([bu_re_ref[j] for j in range(nt)], axis=1).astype(bf16)
    xs_im = jnp.concatenate([bu_im_ref[j] for j in range(nt)], axis=1).astype(bf16)
    y = (jnp.dot(xs_re, cre_ref[0].astype(bf16), preferred_element_type=f32)
         - jnp.dot(xs_im, cim_ref[0].astype(bf16), preferred_element_type=f32)
         + d_ref[0] * u)
    y_ref[...] = y.reshape(y_ref.shape)


def _block_diag(w, blocks):
    G, r, c = w.shape
    w4 = w.reshape(G // blocks, blocks, r, c)
    eye = jnp.eye(blocks, dtype=w.dtype)
    out = w4[:, :, :, None, :] * eye[None, :, None, :, None]
    return out.reshape(G // blocks, blocks * r, blocks * c)


def _s5_scan(cfg, u3, a_re, a_im, log_dt, b_re, b_im, c_re, c_im, d_skip):
    B, S, W = u3.shape
    G, P, Hc, L = cfg.ssm_groups, cfg.ssm_state, cfg.ssm_group, cfg.scan_chunk
    gps = min(G, 2 * LANES // Hc)
    ns = G // gps
    wu, wn = gps * Hc, gps * P
    nt = wn // LANES
    are = a_re.reshape(ns, 1, wn)
    aim = a_im.reshape(ns, 1, wn)
    ldt = jnp.repeat(log_dt, P).reshape(ns, 1, wn)
    bre = _block_diag(b_re.transpose(0, 2, 1), gps)
    bim = _block_diag(b_im.transpose(0, 2, 1), gps)
    cre = _block_diag(c_re.transpose(0, 2, 1), gps)
    cim = _block_diag(c_im.transpose(0, 2, 1), gps)
    dsk = d_skip.reshape(ns, 1, wu)
    slab = lambda s, t: (s, 0, 0)
    return pl.pallas_call(
        functools.partial(_s5_scan_kernel, cfg=cfg),
        grid=(ns, S // L),
        in_specs=[pl.BlockSpec((B, L, wu), lambda s, t: (0, t, s)),
                  pl.BlockSpec((1, 1, wn), slab), pl.BlockSpec((1, 1, wn), slab), pl.BlockSpec((1, 1, wn), slab),
                  pl.BlockSpec((1, wu, wn), slab), pl.BlockSpec((1, wu, wn), slab),
                  pl.BlockSpec((1, wn, wu), slab), pl.BlockSpec((1, wn, wu), slab),
                  pl.BlockSpec((1, 1, wu), slab)],
        out_specs=pl.BlockSpec((B, L, wu), lambda s, t: (0, t, s)),
        out_shape=jax.ShapeDtypeStruct((B, S, W), f32),
        scratch_shapes=[pltpu.VMEM((2, SUBLANES, wn), f32),
                        pltpu.VMEM((wu, wn), bf16), pltpu.VMEM((wu, wn), bf16),
                        pltpu.VMEM((nt, B, LANES), f32), pltpu.VMEM((nt, B, LANES), f32),
                        pltpu.VMEM((nt, B * L, LANES), f32), pltpu.VMEM((nt, B * L, LANES), f32)],
        compiler_params=_cparams("arbitrary", "arbitrary"),
        name="s5_scan",
    )(u3, are, aim, ldt, bre, bim, cre, cim, dsk)


def _s5_glu_kernel(y_ref, w_ref, b_ref, g_ref, o_ref):
    g = _gelu(y_ref[...])
    z = jnp.dot(g.astype(bf16), w_ref[...], preferred_element_type=f32) + b_ref[...]
    g = g * jax.nn.sigmoid(z)
    o_ref[...] = _rms(g, g_ref[...]).astype(o_ref.dtype)


def _s5_glu(cfg, y2, w, b, g):
    T, W = y2.shape
    tm = cfg.tm
    row = lambda i: (i, 0)
    return pl.pallas_call(
        _s5_glu_kernel,
        grid=(T // tm,),
        in_specs=[pl.BlockSpec((tm, W), row), _resident(w.shape), _resident((1, W)), _resident((1, W))],
        out_specs=pl.BlockSpec((tm, W), row),
        out_shape=jax.ShapeDtypeStruct((T, W), bf16),
        compiler_params=_cparams("parallel"),
        name="s5_glu",
    )(y2, w, b, g)


def _out_proj_kernel(x_ref, att_ref, ssm_ref, w_ref, g_ref, h_ref, hn_ref, *, cfg):
    aw = cfg.attn_width
    h = (x_ref[...]
         + jnp.dot(att_ref[...], w_ref[:aw, :], preferred_element_type=f32)
         + jnp.dot(ssm_ref[...], w_ref[aw:, :], preferred_element_type=f32))
    h_ref[...] = h
    hn_ref[...] = _rms(h, g_ref[...]).astype(hn_ref.dtype)


def _out_proj(cfg, x2, att, ssm, w, g):
    T, D = x2.shape
    tm = cfg.tm
    row = lambda i: (i, 0)
    return pl.pallas_call(
        functools.partial(_out_proj_kernel, cfg=cfg),
        grid=(T // tm,),
        in_specs=[pl.BlockSpec((tm, D), row), pl.BlockSpec((tm, cfg.attn_width), row),
                  pl.BlockSpec((tm, cfg.ssm_width), row), _resident(w.shape), _resident((1, D))],
        out_specs=[pl.BlockSpec((tm, D), row), pl.BlockSpec((tm, D), row)],
        out_shape=[jax.ShapeDtypeStruct((T, D), f32), jax.ShapeDtypeStruct((T, D), bf16)],
        compiler_params=_cparams("parallel"),
        name="out_proj",
    )(x2, att, ssm, w, g)


def _mem_kv_kernel(m_ref, g_ref, w_ref, o_ref):
    mn = _rms(m_ref[...], g_ref[...]).astype(bf16)
    o_ref[...] = jnp.dot(mn, w_ref[...], preferred_element_type=f32).astype(o_ref.dtype)


def _mem_kv(cfg, mem2, g, w):
    R, D = mem2.shape
    tm = cfg.n_mem
    row = lambda i: (i, 0)
    return pl.pallas_call(
        _mem_kv_kernel,
        grid=(R // tm,),
        in_specs=[pl.BlockSpec((tm, D), row), _resident((1, D)), _resident(w.shape)],
        out_specs=pl.BlockSpec((tm, w.shape[1]), row),
        out_shape=jax.ShapeDtypeStruct((R, w.shape[1]), bf16),
        compiler_params=_cparams("parallel"),
        name="mem_kv",
    )(mem2, g, w)


def _xattn_kernel(h_ref, hn_ref, kv_ref, wq_ref, wo_ref, g_ref, pw_ref, h2_ref, qp_ref, *, cfg):
    nh, hd, xw = cfg.xattn_heads, cfg.xattn_head_dim, cfg.xattn_width
    q = jnp.dot(hn_ref[...], wq_ref[...], preferred_element_type=f32).astype(bf16)
    outs = []
    for h in range(nh):
        qh = q[:, h * hd:(h + 1) * hd]
        kh = kv_ref[:, h * hd:(h + 1) * hd]
        vh = kv_ref[:, xw + h * hd:xw + (h + 1) * hd]
        s = lax.dot_general(qh, kh, (((1,), (1,)), ((), ())), preferred_element_type=f32) * (hd ** -0.5)
        p = jnp.exp(s - jnp.max(s, axis=1, keepdims=True))
        p = p / jnp.sum(p, axis=1, keepdims=True)
        outs.append(jnp.dot(p.astype(bf16), vh, preferred_element_type=f32).astype(bf16))
    o = jnp.concatenate(outs, axis=1)
    h2 = h_ref[...] + jnp.dot(o, wo_ref[...], preferred_element_type=f32)
    h2_ref[...] = h2
    hp = _rms(h2, g_ref[...]).astype(bf16)
    qp_ref[...] = jnp.dot(hp, pw_ref[...], preferred_element_type=f32).astype(qp_ref.dtype)


def _xattn(cfg, h1, hn, kv, wq, wo, g, pw):
    T, D = h1.shape
    tm, M = cfg.tm, cfg.n_mem
    per_b = cfg.seq // tm
    row = lambda i: (i, 0)
    return pl.pallas_call(
        functools.partial(_xattn_kernel, cfg=cfg),
        grid=(T // tm,),
        in_specs=[pl.BlockSpec((tm, D), row), pl.BlockSpec((tm, D), row),
                  pl.BlockSpec((M, kv.shape[1]), lambda i: (i // per_b, 0)),
                  _resident(wq.shape), _resident(wo.shape), _resident((1, D)), _resident(pw.shape)],
        out_specs=[pl.BlockSpec((tm, D), row), pl.BlockSpec((tm, pw.shape[1]), row)],
        out_shape=[jax.ShapeDtypeStruct((T, D), f32), jax.ShapeDtypeStruct((T, pw.shape[1]), bf16)],
        compiler_params=_cparams("parallel"),
        name="xattn",
    )(h1, hn, kv, wq, wo, g, pw)


def _top_rows(s, k, payload=None):
    n = s.shape[0]
    rows = lax.broadcasted_iota(i32, s.shape, 0)
    vals, picks = [], []
    for _ in range(k):
        m = jnp.max(s, axis=0, keepdims=True)
        idx = jnp.min(jnp.where(s == m, rows, n), axis=0, keepdims=True)
        sel = rows == idx
        vals.append(m)
        picks.append(idx if payload is None else jnp.max(jnp.where(sel, payload, -1), axis=0, keepdims=True))
        s = jnp.where(sel, NEG, s)
    return vals, picks


def _peer_route_kernel(qp_ref, k1_ref, k2_ref, e_ref, g_ref, *, cfg):
    K, nk, half = cfg.peer_topk, cfg.peer_keys, cfg.peer_qdim // 2
    tm = qp_ref.shape[0]
    dn = (((1,), (1,)), ((), ()))
    s1 = lax.dot_general(k1_ref[0], qp_ref[:, :half], dn, preferred_element_type=f32)
    s2 = lax.dot_general(k2_ref[0], qp_ref[:, half:], dn, preferred_element_type=f32)
    t1, i1 = _top_rows(s1, K)
    t2, i2 = _top_rows(s2, K)
    t1, i1 = jnp.concatenate(t1, axis=0), jnp.concatenate(i1, axis=0)
    t2, i2 = jnp.concatenate(t2, axis=0), jnp.concatenate(i2, axis=0)
    cand = (t1[:, None, :] + t2[None, :, :]).reshape(K * K, tm)
    expert = (i1[:, None, :] * nk + i2[None, :, :]).reshape(K * K, tm)
    ts, es = _top_rows(cand, K, payload=expert)
    ts = jnp.concatenate(ts, axis=0)
    p = jnp.exp(ts - ts[0:1])
    g_ref[...] = p / jnp.sum(p, axis=0, keepdims=True)
    e_ref[...] = jnp.concatenate(es, axis=0)


def _peer_route(cfg, qp, k1, k2):
    T = qp.shape[0]
    tm, H, K, Q = cfg.route_tm, cfg.peer_heads, cfg.peer_topk, cfg.peer_qdim
    out = pl.BlockSpec((K, tm), lambda i, h: (h, i))
    return pl.pallas_call(
        functools.partial(_peer_route_kernel, cfg=cfg),
        grid=(T // tm, H),
        in_specs=[pl.BlockSpec((tm, Q), lambda i, h: (i, h)),
                  pl.BlockSpec((1,) + k1.shape[1:], lambda i, h: (h, 0, 0)),
                  pl.BlockSpec((1,) + k2.shape[1:], lambda i, h: (h, 0, 0))],
        out_specs=[out, out],
        out_shape=[jax.ShapeDtypeStruct((H * K, T), i32), jax.ShapeDtypeStruct((H * K, T), f32)],
        compiler_params=_cparams("parallel", "parallel"),
        name="peer_route",
    )(qp, k1, k2)


def _sc_gather(cfg, table, idx):
    n = idx.shape[0]
    d = table.shape[1]
    win, nb = cfg.gather_window, cfg.gather_bufs
    mesh = plsc.VectorSubcoreMesh(core_axis_name="core", subcore_axis_name="subcore")
    workers = mesh.num_cores * mesh.num_subcores
    per = n // workers
    steps = per // win
    assert per * workers == n and steps * win == per and steps % nb == 0

    @functools.partial(pl.kernel, out_type=jax.ShapeDtypeStruct((n, d), table.dtype), mesh=mesh,
                       scratch_types=[pltpu.VMEM((per,), i32)] + [pltpu.VMEM((win, d), table.dtype)] * nb
                       + [pltpu.SemaphoreType.DMA((nb,)), pltpu.SemaphoreType.DMA((nb,))])
    def gather_kernel(tab_hbm, idx_hbm, out_hbm, idx_vmem, *rest):
        bufs, gsem, wsem = rest[:nb], rest[nb], rest[nb + 1]
        wid = lax.axis_index("core") * mesh.num_subcores + lax.axis_index("subcore")
        base = wid * per
        pltpu.sync_copy(idx_hbm.at[pl.ds(base, per)], idx_vmem)

        def gather(k, b):
            return pltpu.make_async_copy(tab_hbm.at[idx_vmem.at[pl.ds(k * win, win)]], bufs[b], gsem.at[b])

        def write(k, b):
            return pltpu.make_async_copy(bufs[b], out_hbm.at[pl.ds(base + k * win, win)], wsem.at[b])

        for b in range(nb - 1):
            gather(b, b).start()

        @pl.loop(0, steps, step=nb)
        def _(k0):
            for b in range(nb):
                k = k0 + b
                pb = (b - 1) % nb
                gather(k, b).wait()
                write(k, b).start()

                @pl.when(k >= 1)
                def _():
                    write(k - 1, pb).wait()

                @pl.when(k + nb - 1 < steps)
                def _():
                    gather(k + nb - 1, pb).start()

        write(steps - 1, (steps - 1) % nb).wait()

    return gather_kernel(table, idx)


def _pack_bf16_halves(a):
    h = a.shape[1] // 2
    b = lax.bitcast_convert_type(a.astype(bf16), jnp.uint16).astype(jnp.uint32)
    return b[:, :h] | (b[:, h:] << 16)


def _unpack_bf16_halves(words):
    return pltpu.bitcast(words << 16, f32), pltpu.bitcast(words & jnp.uint32(0xFFFF0000), f32)


def _peer_apply_kernel(h_ref, gt_ref, rows_ref, gn_ref, gf_ref, o_ref, p_ref, w_ref, *, cfg):
    E = cfg.slots
    R = SUBLANES
    D = h_ref.shape[1]
    nw = D // 2
    nt = nw // LANES
    ones = jnp.ones((LANES, LANES), f32)
    diag = (lax.broadcasted_iota(i32, (E, R, LANES), 0) == lax.broadcasted_iota(i32, (E, R, LANES), 2))
    tiles = lambda a: [a[:, j * LANES:(j + 1) * LANES] for j in range(nt)]

    for gi in range(cfg.apply_groups):
        tok = slice(gi * R, (gi + 1) * R)
        base = gi * E * R
        h2 = h_ref[tok, :]
        x = _rms(h2, gn_ref[...])
        x_lo, x_hi = tiles(x[:, :nw]), tiles(x[:, nw:])

        def act_body(e, carry):
            lo, hi = _unpack_bf16_halves(rows_ref[pl.ds(pl.multiple_of(base + e * R, R), R), :nw])
            lo, hi = tiles(lo), tiles(hi)
            part = [lo[j] * x_lo[j] + hi[j] * x_hi[j] for j in range(nt)]
            while len(part) > 1:
                part = [part[i] + part[i + 1] for i in range(0, len(part), 2)]
            p_ref[pl.ds(pl.multiple_of(e * R, R), R), :] = part[0]
            return carry

        lax.fori_loop(0, E, act_body, 0, unroll=4)
        act_rep = jnp.dot(p_ref[...], ones, preferred_element_type=f32, precision=lax.Precision.HIGHEST)
        act = jnp.sum(jnp.where(diag, act_rep.reshape(E, R, LANES), 0.0), axis=0)
        w = gt_ref[tok, :] * _gelu(act)
        w_sel = jnp.where(diag, w[None, :, :], 0.0).reshape(E * R, LANES)
        w_ref[...] = jnp.dot(w_sel, ones, preferred_element_type=f32, precision=lax.Precision.HIGHEST)

        def out_body(e, acc):
            lo, hi = _unpack_bf16_halves(rows_ref[pl.ds(pl.multiple_of(base + e * R, R), R), nw:])
            lo, hi = tiles(lo), tiles(hi)
            we = w_ref[pl.ds(pl.multiple_of(e * R, R), R), :]
            return (tuple(acc[0][j] + lo[j] * we for j in range(nt)),
                    tuple(acc[1][j] + hi[j] * we for j in range(nt)))

        zero = tuple(jnp.zeros((R, LANES), f32) for _ in range(nt))
        y_lo, y_hi = lax.fori_loop(0, E, out_body, (zero, zero), unroll=4)
        y = jnp.concatenate(list(y_lo) + list(y_hi), axis=1)
        o_ref[tok, :] = _rms(h2 + y, gf_ref[...])


def _peer_apply(cfg, h2, gates, rows, gn, gf):
    T, D = h2.shape
    E, R, G = cfg.slots, SUBLANES, cfg.apply_groups
    assert E == LANES and rows.shape == (T * E, D)
    row = lambda i: (i, 0)
    return pl.pallas_call(
        functools.partial(_peer_apply_kernel, cfg=cfg),
        grid=(T // (R * G),),
        in_specs=[pl.BlockSpec((R * G, D), row), pl.BlockSpec((R * G, E), row),
                  pl.BlockSpec((G * E * R, D), row), _resident((1, D)), _resident((1, D))],
        out_specs=pl.BlockSpec((R * G, D), row),
        out_shape=jax.ShapeDtypeStruct((T, D), f32),
        scratch_shapes=[pltpu.VMEM((E * R, LANES), f32), pltpu.VMEM((E * R, LANES), f32)],
        compiler_params=_cparams("parallel"),
        name="peer_apply",
    )(h2, gates, rows, gn, gf)


def _block(cfg, gather_fn, x, mem, positions, mix_norm_g, w_in, lam_q1, lam_k1, lam_q2, lam_k2, attn_head_g,
           ssm_a_re, ssm_a_im, ssm_log_dt, ssm_b_re, ssm_b_im, ssm_c_re, ssm_c_im, ssm_d, glu_w, glu_b,
           ssm_out_g, w_out, xattn_norm_g, mem_norm_g, xattn_wq, xattn_wkv, xattn_wo, ffn_norm_g,
           peer_wq, peer_k1, peer_k2, peer_u, peer_v, final_norm_g):
    B, S, D = x.shape
    T = B * S
    l = 0
    row = lambda a: a.reshape(1, -1)
    x2 = x.reshape(T, D)
    pos = positions.reshape(T, 1).astype(f32)
    freqs = cfg.rope_theta ** (-jnp.arange(0, cfg.rot_dim, 2, dtype=f32) / cfg.rot_dim)
    lane = jnp.arange(LANES) % cfg.diff_qkdim
    freq_row = jnp.where(lane < cfg.rot_dim, freqs[lane % (cfg.rot_dim // 2)], 0.0).reshape(1, LANES)

    qk, v, u = _in_proj(cfg, x2, pos, row(mix_norm_g[l]), freq_row, w_in[l].astype(bf16))
    lam_p = jnp.stack([lam_q1[l], lam_k1[l], lam_q2[l], lam_k2[l]])
    att = _diff_attn(cfg, lam_p, qk, v, row(attn_head_g[l]))
    y = _s5_scan(cfg, u.reshape(B, S, cfg.ssm_width), ssm_a_re[l], ssm_a_im[l], ssm_log_dt[l], ssm_b_re[l],
                 ssm_b_im[l], ssm_c_re[l], ssm_c_im[l], ssm_d[l])
    ssm = _s5_glu(cfg, y.reshape(T, cfg.ssm_width), glu_w[l].astype(bf16), row(glu_b[l]), row(ssm_out_g[l]))
    h1, hn = _out_proj(cfg, x2, att, ssm, w_out[l].astype(bf16), row(xattn_norm_g[l]))
    kv = _mem_kv(cfg, mem.reshape(B * cfg.n_mem, D), row(mem_norm_g[l]), xattn_wkv[l].astype(bf16))
    h2, qp = _xattn(cfg, h1, hn, kv, xattn_wq[l].astype(bf16), xattn_wo[l].astype(bf16), row(ffn_norm_g[l]),
                    peer_wq[l].astype(bf16))
    experts_t, gates_t = _peer_route(cfg, qp, peer_k1[l].astype(bf16), peer_k2[l].astype(bf16))

    E, R = cfg.slots, SUBLANES
    idx = experts_t.reshape(E, T // R, R).transpose(1, 0, 2).reshape(-1)
    gates = gates_t.T
    table = jnp.concatenate([_pack_bf16_halves(peer_u[l]), _pack_bf16_halves(peer_v[l])], axis=1)
    nc = cfg.peer_chunks
    tc = T // nc
    outs = []
    for c in range(nc):
        rows = gather_fn(table, idx[c * tc * E:(c + 1) * tc * E])
        outs.append(_peer_apply(cfg, h2[c * tc:(c + 1) * tc], gates[c * tc:(c + 1) * tc], rows,
                                row(ffn_norm_g[l]), row(final_norm_g)))
    return jnp.concatenate(outs, axis=0).reshape(B, S, D)


def kernel(x, mem, positions, mix_norm_g, w_in, lam_q1, lam_k1, lam_q2, lam_k2, attn_head_g, ssm_a_re, ssm_a_im, ssm_log_dt, ssm_b_re, ssm_b_im, ssm_c_re, ssm_c_im, ssm_d, glu_w, glu_b, ssm_out_g, w_out, xattn_norm_g, mem_norm_g, xattn_wq, xattn_wkv, xattn_wo, ffn_norm_g, peer_wq, peer_k1, peer_k2, peer_u, peer_v, final_norm_g):
    cfg = Cfg()
    return _block(cfg, functools.partial(_sc_gather, cfg), x, mem, positions, mix_norm_g, w_in, lam_q1, lam_k1,
                  lam_q2, lam_k2, attn_head_g, ssm_a_re, ssm_a_im, ssm_log_dt, ssm_b_re, ssm_b_im, ssm_c_re,
                  ssm_c_im, ssm_d, glu_w, glu_b, ssm_out_g, w_out, xattn_norm_g, mem_norm_g, xattn_wq,
                  xattn_wkv, xattn_wo, ffn_norm_g, peer_wq, peer_k1, peer_k2, peer_u, peer_v, final_norm_g)
```

```python
import dataclasses
import functools
import math

import jax
import jax.numpy as jnp
from jax import lax
from jax.experimental import pallas as pl
from jax.experimental.pallas import tpu as pltpu
from jax.experimental.pallas import tpu_sc as plsc

f32 = jnp.float32
bf16 = jnp.bfloat16
i32 = jnp.int32

LANES = 128
SUBLANES = 8
VMEM_LIMIT = 56 * 1024 * 1024
NEG = -1e30
EPS = 1e-6


@dataclasses.dataclass(frozen=True)
class Cfg:
    d_model: int = 2048
    batch: int = 8
    seq: int = 2048
    n_mem: int = 256
    diff_heads: int = 8
    ssm_group: int = 16
    ssm_state: int = 64
    xattn_heads: int = 4
    xattn_head_dim: int = 128
    peer_heads: int = 8
    peer_keys: int = 128
    peer_qdim: int = 256
    peer_topk: int = 16
    rope_theta: float = 500000.0
    lam_init: float = 0.8 - 0.6 * math.exp(-0.3 * 0)
    chunk: int = 256
    scan_chunk: int = 128
    route_tm: int = 256
    gather_window: int = 8
    gather_bufs: int = 4
    apply_groups: int = 2

    @property
    def n_chunks(self):
        return self.seq // self.chunk

    @property
    def attn_width(self):
        return self.d_model // 2

    @property
    def ssm_width(self):
        return self.d_model - self.attn_width

    @property
    def diff_vdim(self):
        return self.attn_width // self.diff_heads

    @property
    def diff_qkdim(self):
        return self.diff_vdim // 2

    @property
    def rot_dim(self):
        return self.diff_qkdim // 4

    @property
    def ssm_groups(self):
        return self.ssm_width // self.ssm_group

    @property
    def xattn_width(self):
        return self.xattn_heads * self.xattn_head_dim

    @property
    def tokens(self):
        return self.batch * self.seq

    @property
    def slots(self):
        return self.peer_heads * self.peer_topk


def _cparams(*sem):
    return pltpu.CompilerParams(dimension_semantics=sem, vmem_limit_bytes=VMEM_LIMIT)


def _resident(shape):
    nd = len(shape)
    return pl.BlockSpec(shape, lambda *_: (0,) * nd, pipeline_mode=pl.Buffered(1))


def _rms(x, g):
    return x * lax.rsqrt(jnp.mean(x * x, axis=-1, keepdims=True) + EPS) * g


def _gelu(x):
    return 0.5 * x * (1.0 + lax.erf(x * (2.0 ** -0.5)))


def _in_proj_kernel(x_ref, pos_ref, g_ref, freq_ref, w_ref, qk_ref, v_ref, u_ref, *, cfg):
    n_qk, n_v = 2 * cfg.attn_width, cfg.attn_width
    half = cfg.rot_dim // 2
    xn = _rms(x_ref[...], g_ref[...]).astype(bf16)
    ang = pos_ref[...] * freq_ref[...]
    cos, sin = jnp.cos(ang), jnp.sin(ang)
    lane = lax.broadcasted_iota(i32, (1, LANES), 1) % cfg.diff_qkdim
    sin_lo = jnp.where(lane < half, -sin, 0.0)
    sin_hi = jnp.where((lane >= half) & (lane < 2 * half), sin, 0.0)
    cw = 2 * LANES
    for c in range((n_qk + n_v + cfg.ssm_width) // cw):
        col = c * cw
        z = jnp.dot(xn, w_ref[:, col:col + cw], preferred_element_type=f32)
        if col < n_qk:
            for k in range(cw // LANES):
                zk = z[:, k * LANES:(k + 1) * LANES]
                zk = zk * cos + pltpu.roll(zk, LANES - half, 1) * sin_lo + pltpu.roll(zk, half, 1) * sin_hi
                qk_ref[:, col + k * LANES:col + (k + 1) * LANES] = zk.astype(bf16)
        elif col < n_qk + n_v:
            v_ref[:, col - n_qk:col - n_qk + cw] = z.astype(bf16)
        else:
            u_ref[:, col - n_qk - n_v:col - n_qk - n_v + cw] = z


def _chunk_rows(cfg, c):
    return lambda b: (b * cfg.n_chunks + c, 0)


def _in_proj(cfg, c, x2, pos, g, freq, w):
    D = x2.shape[1]
    B, tm = cfg.batch, cfg.chunk
    n_qk, n_v, n_u = 2 * cfg.attn_width, cfg.attn_width, cfg.ssm_width
    row = lambda b: (b, 0)
    return pl.pallas_call(
        functools.partial(_in_proj_kernel, cfg=cfg),
        grid=(B,),
        in_specs=[pl.BlockSpec((tm, D), _chunk_rows(cfg, c)), pl.BlockSpec((tm, 1), _chunk_rows(cfg, c)),
                  _resident((1, D)), _resident((1, LANES)), _resident(w.shape)],
        out_specs=[pl.BlockSpec((tm, n_qk), row), pl.BlockSpec((tm, n_v), row), pl.BlockSpec((tm, n_u), row)],
        out_shape=[jax.ShapeDtypeStruct((B * tm, n_qk), bf16), jax.ShapeDtypeStruct((B * tm, n_v), bf16),
                   jax.ShapeDtypeStruct((B * tm, n_u), f32)],
        compiler_params=_cparams("parallel"),
        name="in_proj",
    )(x2, pos, g, freq, w)


def _diff_attn_kernel(lam_ref, q_ref, *rest, cfg, n_kv):
    k_refs, v_refs = rest[:n_kv], rest[n_kv:2 * n_kv]
    g_ref, o_ref, m_ref, l_ref, acc_ref = rest[2 * n_kv:]
    tq = cfg.chunk
    d = cfg.diff_qkdim
    lane = lax.broadcasted_iota(i32, (1, LANES), 1)
    q = q_ref[...].astype(f32) * (d ** -0.5)
    qs = (jnp.where(lane < d, q, 0.0).astype(bf16), jnp.where(lane >= d, q, 0.0).astype(bf16))
    m_ref[...] = jnp.full(m_ref.shape, NEG, f32)
    l_ref[...] = jnp.zeros(l_ref.shape, f32)
    acc_ref[...] = jnp.zeros(acc_ref.shape, f32)
    causal = (lax.broadcasted_iota(i32, (tq, tq), 1) <= lax.broadcasted_iota(i32, (tq, tq), 0))

    def step(j, masked):
        kb = k_refs[j][...]
        vb = v_refs[j][...]
        for c in range(2):
            s = lax.dot_general(qs[c], kb, (((1,), (1,)), ((), ())), preferred_element_type=f32)
            if masked:
                s = jnp.where(causal, s, NEG)
            m_old = m_ref[c]
            m_new = jnp.maximum(m_old, jnp.max(s, axis=1, keepdims=True))
            alpha = jnp.exp(m_old - m_new)
            p = jnp.exp(s - m_new)
            l_ref[c] = alpha * l_ref[c] + jnp.sum(p, axis=1, keepdims=True)
            acc_ref[c] = alpha * acc_ref[c] + jnp.dot(p.astype(bf16), vb, preferred_element_type=f32)
            m_ref[c] = m_new

    for j in range(n_kv - 1):
        step(j, False)
    step(n_kv - 1, True)

    lp = lam_ref[...]
    lam = (jnp.exp(jnp.sum(lp[0:1] * lp[1:2], axis=1, keepdims=True))
           - jnp.exp(jnp.sum(lp[2:3] * lp[3:4], axis=1, keepdims=True)) + cfg.lam_init)
    o = acc_ref[0] / l_ref[0] - lam * (acc_ref[1] / l_ref[1])
    o_ref[...] = (_rms(o, g_ref[...]) * (1.0 - cfg.lam_init)).astype(o_ref.dtype)


def _diff_attn(cfg, lam_p, qks, vs, g):
    n_kv = len(qks)
    B, H, tq = cfg.batch, cfg.diff_heads, cfg.chunk
    head = lambda b, h: (b, h)
    key = lambda b, h: (b, H + h)
    return pl.pallas_call(
        functools.partial(_diff_attn_kernel, cfg=cfg, n_kv=n_kv),
        grid=(B, H),
        in_specs=([_resident(lam_p.shape), pl.BlockSpec((tq, LANES), head)]
                  + [pl.BlockSpec((tq, LANES), key)] * n_kv + [pl.BlockSpec((tq, LANES), head)] * n_kv
                  + [_resident((1, LANES))]),
        out_specs=pl.BlockSpec((tq, LANES), head),
        out_shape=jax.ShapeDtypeStruct((B * tq, cfg.attn_width), bf16),
        scratch_shapes=[pltpu.VMEM((2, tq, 1), f32), pltpu.VMEM((2, tq, 1), f32),
                        pltpu.VMEM((2, tq, LANES), f32)],
        compiler_params=_cparams("parallel", "parallel"),
        name="diff_attn",
    )(lam_p, qks[-1], *qks, *vs, g)


def _s5_scan_kernel(u_ref, are_ref, aim_ref, ldt_ref, bre_ref, bim_ref, cre_ref, cim_ref, d_ref,
                    sin_re_ref, sin_im_ref, y_ref, sout_re_ref, sout_im_ref,
                    ab_ref, bbar_re_ref, bbar_im_ref, bu_re_ref, bu_im_ref, *, cfg):
    B, L = cfg.batch, cfg.scan_chunk
    nt = bu_re_ref.shape[0]
    t = pl.program_id(1)
    st_re_ref, st_im_ref = sout_re_ref.at[0], sout_im_ref.at[0]

    @pl.when(t == 0)
    def _():
        dt = jnp.exp(ldt_ref[0])
        lre, lim = are_ref[0], aim_ref[0]
        mag = jnp.exp(lre * dt)
        ang = lim * dt
        ab_re, ab_im = mag * jnp.cos(ang), mag * jnp.sin(ang)
        den = lre * lre + lim * lim
        f_re = ((ab_re - 1.0) * lre + ab_im * lim) / den
        f_im = (ab_im * lre - (ab_re - 1.0) * lim) / den
        ab_ref[0] = jnp.broadcast_to(ab_re, ab_ref.shape[1:])
        ab_ref[1] = jnp.broadcast_to(ab_im, ab_ref.shape[1:])
        br, bi = bre_ref[0], bim_ref[0]
        bbar_re_ref[...] = (f_re * br - f_im * bi).astype(bf16)
        bbar_im_ref[...] = (f_re * bi + f_im * br).astype(bf16)
        st_re_ref[...] = sin_re_ref[0]
        st_im_ref[...] = sin_im_ref[0]

    u = u_ref[...].reshape(B * L, u_ref.shape[2])
    ub = u.astype(bf16)
    bu_re = jnp.dot(ub, bbar_re_ref[...], preferred_element_type=f32)
    bu_im = jnp.dot(ub, bbar_im_ref[...], preferred_element_type=f32)
    for j in range(nt):
        bu_re_ref[j] = bu_re[:, j * LANES:(j + 1) * LANES]
        bu_im_ref[j] = bu_im[:, j * LANES:(j + 1) * LANES]

    a_re = [ab_ref[0, :, j * LANES:(j + 1) * LANES] for j in range(nt)]
    a_im = [ab_ref[1, :, j * LANES:(j + 1) * LANES] for j in range(nt)]

    def body(s, carry):
        xr, xi = carry
        nr, ni = [], []
        for j in range(nt):
            rows = pl.ds(s, B, stride=L)
            r = a_re[j] * xr[j] - a_im[j] * xi[j] + bu_re_ref[j, rows, :]
            m = a_re[j] * xi[j] + a_im[j] * xr[j] + bu_im_ref[j, rows, :]
            bu_re_ref[j, rows, :] = r
            bu_im_ref[j, rows, :] = m
            nr.append(r)
            ni.append(m)
        return tuple(nr), tuple(ni)

    init = (tuple(st_re_ref[j] for j in range(nt)), tuple(st_im_ref[j] for j in range(nt)))
    xr, xi = lax.fori_loop(0, L, body, init, unroll=8)
    for j in range(nt):
        st_re_ref[j] = xr[j]
        st_im_ref[j] = xi[j]

    xs_re = jnp.concatenate([bu_re_ref[j] for j in range(nt)], axis=1).astype(bf16)
    xs_im = jnp.concatenate([bu_im_ref[j] for j in range(nt)], axis=1).astype(bf16)
    y = (jnp.dot(xs_re, cre_ref[0].astype(bf16), preferred_element_type=f32)
         - jnp.dot(xs_im, cim_ref[0].astype(bf16), preferred_element_type=f32)
         + d_ref[0] * u)
    y_ref[...] = y.reshape(y_ref.shape)


def _block_diag(w, blocks):
    G, r, c = w.shape
    w4 = w.reshape(G // blocks, blocks, r, c)
    eye = jnp.eye(blocks, dtype=w.dtype)
    out = w4[:, :, :, None, :] * eye[None, :, None, :, None]
    return out.reshape(G // blocks, blocks * r, blocks * c)


def _s5_params(cfg, a_re, a_im, log_dt, b_re, b_im, c_re, c_im, d_skip):
    G, P, Hc = cfg.ssm_groups, cfg.ssm_state, cfg.ssm_group
    gps = min(G, 2 * LANES // Hc)
    ns = G // gps
    wu, wn = gps * Hc, gps * P
    return (a_re.reshape(ns, 1, wn), a_im.reshape(ns, 1, wn), jnp.repeat(log_dt, P).reshape(ns, 1, wn),
            _block_diag(b_re.transpose(0, 2, 1), gps), _block_diag(b_im.transpose(0, 2, 1), gps),
            _block_diag(c_re.transpose(0, 2, 1), gps), _block_diag(c_im.transpose(0, 2, 1), gps),
            d_skip.reshape(ns, 1, wu))


def _s5_scan(cfg, u3, params, state):
    B, steps, W = u3.shape
    L = cfg.scan_chunk
    ns, _, wn = params[0].shape
    wu = params[-1].shape[2]
    nt = wn // LANES
    slab = lambda s, t: (s, 0, 0)
    st = pl.BlockSpec((1, nt, B, LANES), lambda s, t: (s, 0, 0, 0))
    st_shape = jax.ShapeDtypeStruct((ns, nt, B, LANES), f32)
    y, s_re, s_im = pl.pallas_call(
        functools.partial(_s5_scan_kernel, cfg=cfg),
        grid=(ns, steps // L),
        in_specs=[pl.BlockSpec((B, L, wu), lambda s, t: (0, t, s)),
                  pl.BlockSpec((1, 1, wn), slab), pl.BlockSpec((1, 1, wn), slab), pl.BlockSpec((1, 1, wn), slab),
                  pl.BlockSpec((1, wu, wn), slab), pl.BlockSpec((1, wu, wn), slab),
                  pl.BlockSpec((1, wn, wu), slab), pl.BlockSpec((1, wn, wu), slab),
                  pl.BlockSpec((1, 1, wu), slab), st, st],
        out_specs=[pl.BlockSpec((B, L, wu), lambda s, t: (0, t, s)), st, st],
        out_shape=[jax.ShapeDtypeStruct((B, steps, W), f32), st_shape, st_shape],
        scratch_shapes=[pltpu.VMEM((2, SUBLANES, wn), f32),
                        pltpu.VMEM((wu, wn), bf16), pltpu.VMEM((wu, wn), bf16),
                        pltpu.VMEM((nt, B * L, LANES), f32), pltpu.VMEM((nt, B * L, LANES), f32)],
        compiler_params=_cparams("arbitrary", "arbitrary"),
        name="s5_scan",
    )(u3, *params, *state)
    return y, (s_re, s_im)


def _s5_glu_kernel(y_ref, w_ref, b_ref, g_ref, o_ref):
    g = _gelu(y_ref[...])
    z = jnp.dot(g.astype(bf16), w_ref[...], preferred_element_type=f32) + b_ref[...]
    g = g * jax.nn.sigmoid(z)
    o_ref[...] = _rms(g, g_ref[...]).astype(o_ref.dtype)


def _s5_glu(cfg, y2, w, b, g):
    T, W = y2.shape
    tm = cfg.chunk
    row = lambda i: (i, 0)
    return pl.pallas_call(
        _s5_glu_kernel,
        grid=(T // tm,),
        in_specs=[pl.BlockSpec((tm, W), row), _resident(w.shape), _resident((1, W)), _resident((1, W))],
        out_specs=pl.BlockSpec((tm, W), row),
        out_shape=jax.ShapeDtypeStruct((T, W), bf16),
        compiler_params=_cparams("parallel"),
        name="s5_glu",
    )(y2, w, b, g)


def _out_proj_kernel(x_ref, att_ref, ssm_ref, w_ref, g_ref, h_ref, hn_ref, *, cfg):
    aw = cfg.attn_width
    h = (x_ref[...]
         + jnp.dot(att_ref[...], w_ref[:aw, :], preferred_element_type=f32)
         + jnp.dot(ssm_ref[...], w_ref[aw:, :], preferred_element_type=f32))
    h_ref[...] = h
    hn_ref[...] = _rms(h, g_ref[...]).astype(hn_ref.dtype)


def _out_proj(cfg, c, x2, att, ssm, w, g):
    D = x2.shape[1]
    tm = cfg.chunk
    T = cfg.batch * tm
    row = lambda i: (i, 0)
    return pl.pallas_call(
        functools.partial(_out_proj_kernel, cfg=cfg),
        grid=(T // tm,),
        in_specs=[pl.BlockSpec((tm, D), _chunk_rows(cfg, c)), pl.BlockSpec((tm, cfg.attn_width), row),
                  pl.BlockSpec((tm, cfg.ssm_width), row), _resident(w.shape), _resident((1, D))],
        out_specs=[pl.BlockSpec((tm, D), row), pl.BlockSpec((tm, D), row)],
        out_shape=[jax.ShapeDtypeStruct((T, D), f32), jax.ShapeDtypeStruct((T, D), bf16)],
        compiler_params=_cparams("parallel"),
        name="out_proj",
    )(x2, att, ssm, w, g)


def _mem_kv_kernel(m_ref, g_ref, w_ref, o_ref):
    mn = _rms(m_ref[...], g_ref[...]).astype(bf16)
    o_ref[...] = jnp.dot(mn, w_ref[...], preferred_element_type=f32).astype(o_ref.dtype)


def _mem_kv(cfg, mem2, g, w):
    R, D = mem2.shape
    tm = cfg.n_mem
    row = lambda i: (i, 0)
    return pl.pallas_call(
        _mem_kv_kernel,
        grid=(R // tm,),
        in_specs=[pl.BlockSpec((tm, D), row), _resident((1, D)), _resident(w.shape)],
        out_specs=pl.BlockSpec((tm, w.shape[1]), row),
        out_shape=jax.ShapeDtypeStruct((R, w.shape[1]), bf16),
        compiler_params=_cparams("parallel"),
        name="mem_kv",
    )(mem2, g, w)


def _xattn_kernel(h_ref, hn_ref, kv_ref, wq_ref, wo_ref, g_ref, pw_ref, h2_ref, qp_ref, *, cfg):
    nh, hd, xw = cfg.xattn_heads, cfg.xattn_head_dim, cfg.xattn_width
    q = jnp.dot(hn_ref[...], wq_ref[...], preferred_element_type=f32).astype(bf16)
    outs = []
    for h in range(nh):
        qh = q[:, h * hd:(h + 1) * hd]
        kh = kv_ref[:, h * hd:(h + 1) * hd]
        vh = kv_ref[:, xw + h * hd:xw + (h + 1) * hd]
        s = lax.dot_general(qh, kh, (((1,), (1,)), ((), ())), preferred_element_type=f32) * (hd ** -0.5)
        p = jnp.exp(s - jnp.max(s, axis=1, keepdims=True))
        p = p / jnp.sum(p, axis=1, keepdims=True)
        outs.append(jnp.dot(p.astype(bf16), vh, preferred_element_type=f32).astype(bf16))
    o = jnp.concatenate(outs, axis=1)
    h2 = h_ref[...] + jnp.dot(o, wo_ref[...], preferred_element_type=f32)
    h2_ref[...] = h2
    hp = _rms(h2, g_ref[...]).astype(bf16)
    qp_ref[...] = jnp.dot(hp, pw_ref[...], preferred_element_type=f32).astype(qp_ref.dtype)


def _xattn(cfg, h1, hn, kv, wq, wo, g, pw):
    T, D = h1.shape
    tm, M = cfg.chunk, cfg.n_mem
    row = lambda i: (i, 0)
    return pl.pallas_call(
        functools.partial(_xattn_kernel, cfg=cfg),
        grid=(T // tm,),
        in_specs=[pl.BlockSpec((tm, D), row), pl.BlockSpec((tm, D), row),
                  pl.BlockSpec((M, kv.shape[1]), row),
                  _resident(wq.shape), _resident(wo.shape), _resident((1, D)), _resident(pw.shape)],
        out_specs=[pl.BlockSpec((tm, D), row), pl.BlockSpec((tm, pw.shape[1]), row)],
        out_shape=[jax.ShapeDtypeStruct((T, D), f32), jax.ShapeDtypeStruct((T, pw.shape[1]), bf16)],
        compiler_params=_cparams("parallel"),
        name="xattn",
    )(h1, hn, kv, wq, wo, g, pw)


def _top_rows(s, k, payload=None):
    n = s.shape[0]
    rows = lax.broadcasted_iota(i32, s.shape, 0)
    vals, picks = [], []
    for _ in range(k):
        m = jnp.max(s, axis=0, keepdims=True)
        idx = jnp.min(jnp.where(s == m, rows, n), axis=0, keepdims=True)
        sel = rows == idx
        vals.append(m)
        picks.append(idx if payload is None else jnp.max(jnp.where(sel, payload, -1), axis=0, keepdims=True))
        s = jnp.where(sel, NEG, s)
    return vals, picks


def _peer_route_kernel(qp_ref, k1_ref, k2_ref, e_ref, g_ref, *, cfg):
    K, nk, half = cfg.peer_topk, cfg.peer_keys, cfg.peer_qdim // 2
    tm = qp_ref.shape[0]
    dn = (((1,), (1,)), ((), ()))
    s1 = lax.dot_general(k1_ref[0], qp_ref[:, :half], dn, preferred_element_type=f32)
    s2 = lax.dot_general(k2_ref[0], qp_ref[:, half:], dn, preferred_element_type=f32)
    t1, i1 = _top_rows(s1, K)
    t2, i2 = _top_rows(s2, K)
    t1, i1 = jnp.concatenate(t1, axis=0), jnp.concatenate(i1, axis=0)
    t2, i2 = jnp.concatenate(t2, axis=0), jnp.concatenate(i2, axis=0)
    cand = (t1[:, None, :] + t2[None, :, :]).reshape(K * K, tm)
    expert = (i1[:, None, :] * nk + i2[None, :, :]).reshape(K * K, tm)
    ts, es = _top_rows(cand, K, payload=expert)
    ts = jnp.concatenate(ts, axis=0)
    p = jnp.exp(ts - ts[0:1])
    g_ref[...] = p / jnp.sum(p, axis=0, keepdims=True)
    e_ref[...] = jnp.concatenate(es, axis=0)


def _peer_route(cfg, qp, k1, k2):
    T = qp.shape[0]
    tm, H, K, Q = cfg.route_tm, cfg.peer_heads, cfg.peer_topk, cfg.peer_qdim
    out = pl.BlockSpec((K, tm), lambda i, h: (h, i))
    return pl.pallas_call(
        functools.partial(_peer_route_kernel, cfg=cfg),
        grid=(T // tm, H),
        in_specs=[pl.BlockSpec((tm, Q), lambda i, h: (i, h)),
                  pl.BlockSpec((1,) + k1.shape[1:], lambda i, h: (h, 0, 0)),
                  pl.BlockSpec((1,) + k2.shape[1:], lambda i, h: (h, 0, 0))],
        out_specs=[out, out],
        out_shape=[jax.ShapeDtypeStruct((H * K, T), i32), jax.ShapeDtypeStruct((H * K, T), f32)],
        compiler_params=_cparams("parallel", "parallel"),
        name="peer_route",
    )(qp, k1, k2)


def _sc_gather(cfg, table, idx):
    n = idx.shape[0]
    d = table.shape[1]
    win, nb = cfg.gather_window, cfg.gather_bufs
    mesh = plsc.VectorSubcoreMesh(core_axis_name="core", subcore_axis_name="subcore")
    workers = mesh.num_cores * mesh.num_subcores
    per = n // workers
    steps = per // win
    assert per * workers == n and steps * win == per and steps % nb == 0

    @functools.partial(pl.kernel, out_type=jax.ShapeDtypeStruct((n, d), table.dtype), mesh=mesh,
                       scratch_types=[pltpu.VMEM((per,), i32)] + [pltpu.VMEM((win, d), table.dtype)] * nb
                       + [pltpu.SemaphoreType.DMA((nb,)), pltpu.SemaphoreType.DMA((nb,))])
    def gather_kernel(tab_hbm, idx_hbm, out_hbm, idx_vmem, *rest):
        bufs, gsem, wsem = rest[:nb], rest[nb], rest[nb + 1]
        wid = lax.axis_index("core") * mesh.num_subcores + lax.axis_index("subcore")
        base = wid * per
        pltpu.sync_copy(idx_hbm.at[pl.ds(base, per)], idx_vmem)

        def gather(k, b):
            return pltpu.make_async_copy(tab_hbm.at[idx_vmem.at[pl.ds(k * win, win)]], bufs[b], gsem.at[b])

        def write(k, b):
            return pltpu.make_async_copy(bufs[b], out_hbm.at[pl.ds(base + k * win, win)], wsem.at[b])

        for b in range(nb - 1):
            gather(b, b).start()

        @pl.loop(0, steps, step=nb)
        def _(k0):
            for b in range(nb):
                k = k0 + b
                pb = (b - 1) % nb
                gather(k, b).wait()
                write(k, b).start()

                @pl.when(k >= 1)
                def _():
                    write(k - 1, pb).wait()

                @pl.when(k + nb - 1 < steps)
                def _():
                    gather(k + nb - 1, pb).start()

        write(steps - 1, (steps - 1) % nb).wait()

    return gather_kernel(table, idx)


def _pack_bf16_halves(a):
    h = a.shape[1] // 2
    b = lax.bitcast_convert_type(a.astype(bf16), jnp.uint16).astype(jnp.uint32)
    return b[:, :h] | (b[:, h:] << 16)


def _unpack_bf16_halves(words):
    return pltpu.bitcast(words << 16, f32), pltpu.bitcast(words & jnp.uint32(0xFFFF0000), f32)


def _peer_apply_kernel(h_ref, gt_ref, rows_ref, gn_ref, gf_ref, o_ref, p_ref, w_ref, *, cfg):
    E = cfg.slots
    R = SUBLANES
    D = h_ref.shape[1]
    nw = D // 2
    nt = nw // LANES
    ones = jnp.ones((LANES, LANES), f32)
    diag = (lax.broadcasted_iota(i32, (E, R, LANES), 0) == lax.broadcasted_iota(i32, (E, R, LANES), 2))
    tiles = lambda a: [a[:, j * LANES:(j + 1) * LANES] for j in range(nt)]

    for gi in range(cfg.apply_groups):
        tok = slice(gi * R, (gi + 1) * R)
        base = gi * E * R
        h2 = h_ref[tok, :]
        x = _rms(h2, gn_ref[...])
        x_lo, x_hi = tiles(x[:, :nw]), tiles(x[:, nw:])

        def act_body(e, carry):
            lo, hi = _unpack_bf16_halves(rows_ref[pl.ds(pl.multiple_of(base + e * R, R), R), :nw])
            lo, hi = tiles(lo), tiles(hi)
            part = [lo[j] * x_lo[j] + hi[j] * x_hi[j] for j in range(nt)]
            while len(part) > 1:
                part = [part[i] + part[i + 1] for i in range(0, len(part), 2)]
            p_ref[pl.ds(pl.multiple_of(e * R, R), R), :] = part[0]
            return carry

        lax.fori_loop(0, E, act_body, 0, unroll=4)
        act_rep = jnp.dot(p_ref[...], ones, preferred_element_type=f32, precision=lax.Precision.HIGHEST)
        act = jnp.sum(jnp.where(diag, act_rep.reshape(E, R, LANES), 0.0), axis=0)
        w = gt_ref[tok, :] * _gelu(act)
        w_sel = jnp.where(diag, w[None, :, :], 0.0).reshape(E * R, LANES)
        w_ref[...] = jnp.dot(w_sel, ones, preferred_element_type=f32, precision=lax.Precision.HIGHEST)

        def out_body(e, acc):
            lo, hi = _unpack_bf16_halves(rows_ref[pl.ds(pl.multiple_of(base + e * R, R), R), nw:])
            lo, hi = tiles(lo), tiles(hi)
            we = w_ref[pl.ds(pl.multiple_of(e * R, R), R), :]
            return (tuple(acc[0][j] + lo[j] * we for j in range(nt)),
                    tuple(acc[1][j] + hi[j] * we for j in range(nt)))

        zero = tuple(jnp.zeros((R, LANES), f32) for _ in range(nt))
        y_lo, y_hi = lax.fori_loop(0, E, out_body, (zero, zero), unroll=4)
        y = jnp.concatenate(list(y_lo) + list(y_hi), axis=1)
        o_ref[tok, :] = _rms(h2 + y, gf_ref[...])


def _peer_apply(cfg, h2, gates, rows, gn, gf):
    T, D = h2.shape
    E, R, G = cfg.slots, SUBLANES, cfg.apply_groups
    assert E == LANES and rows.shape == (T * E, D)
    row = lambda i: (i, 0)
    return pl.pallas_call(
        functools.partial(_peer_apply_kernel, cfg=cfg),
        grid=(T // (R * G),),
        in_specs=[pl.BlockSpec((R * G, D), row), pl.BlockSpec((R * G, E), row),
                  pl.BlockSpec((G * E * R, D), row), _resident((1, D)), _resident((1, D))],
        out_specs=pl.BlockSpec((R * G, D), row),
        out_shape=jax.ShapeDtypeStruct((T, D), f32),
        scratch_shapes=[pltpu.VMEM((E * R, LANES), f32), pltpu.VMEM((E * R, LANES), f32)],
        compiler_params=_cparams("parallel"),
        name="peer_apply",
    )(h2, gates, rows, gn, gf)


def _block(cfg, gather_fn, x, mem, positions, mix_norm_g, w_in, lam_q1, lam_k1, lam_q2, lam_k2, attn_head_g,
           ssm_a_re, ssm_a_im, ssm_log_dt, ssm_b_re, ssm_b_im, ssm_c_re, ssm_c_im, ssm_d, glu_w, glu_b,
           ssm_out_g, w_out, xattn_norm_g, mem_norm_g, xattn_wq, xattn_wkv, xattn_wo, ffn_norm_g,
           peer_wq, peer_k1, peer_k2, peer_u, peer_v, final_norm_g):
    B, S, D = x.shape
    T = B * S
    l = 0
    row = lambda a: a.reshape(1, -1)
    x2 = x.reshape(T, D)
    pos = positions.reshape(T, 1).astype(f32)
    freqs = cfg.rope_theta ** (-jnp.arange(0, cfg.rot_dim, 2, dtype=f32) / cfg.rot_dim)
    lane = jnp.arange(LANES) % cfg.diff_qkdim
    freq_row = jnp.where(lane < cfg.rot_dim, freqs[lane % (cfg.rot_dim // 2)], 0.0).reshape(1, LANES)

    lam_p = jnp.stack([lam_q1[l], lam_k1[l], lam_q2[l], lam_k2[l]])
    s5_params = _s5_params(cfg, ssm_a_re[l], ssm_a_im[l], ssm_log_dt[l], ssm_b_re[l], ssm_b_im[l],
                           ssm_c_re[l], ssm_c_im[l], ssm_d[l])
    ns, _, wn = s5_params[0].shape
    zero_state = jnp.zeros((ns, wn // LANES, B, LANES), f32)
    state = (zero_state, zero_state)
    w_in_b, glu_w_b, w_out_b = w_in[l].astype(bf16), glu_w[l].astype(bf16), w_out[l].astype(bf16)
    wq_b, wo_b, pw_b = xattn_wq[l].astype(bf16), xattn_wo[l].astype(bf16), peer_wq[l].astype(bf16)
    k1_b, k2_b = peer_k1[l].astype(bf16), peer_k2[l].astype(bf16)
    kv = _mem_kv(cfg, mem.reshape(B * cfg.n_mem, D), row(mem_norm_g[l]), xattn_wkv[l].astype(bf16))
    table = jnp.concatenate([_pack_bf16_halves(peer_u[l]), _pack_bf16_halves(peer_v[l])], axis=1)
    E, R = cfg.slots, SUBLANES
    Lc = cfg.chunk
    Tc = B * Lc

    def dense(c, qks, vs, state):
        qk, v, u = _in_proj(cfg, c, x2, pos, row(mix_norm_g[l]), freq_row, w_in_b)
        qks, vs = qks + [qk], vs + [v]
        att = _diff_attn(cfg, lam_p, qks, vs, row(attn_head_g[l]))
        y, state = _s5_scan(cfg, u.reshape(B, Lc, cfg.ssm_width), s5_params, state)
        ssm = _s5_glu(cfg, y.reshape(Tc, cfg.ssm_width), glu_w_b, row(glu_b[l]), row(ssm_out_g[l]))
        h1, hn = _out_proj(cfg, c, x2, att, ssm, w_out_b, row(xattn_norm_g[l]))
        h2, qp = _xattn(cfg, h1, hn, kv, wq_b, wo_b, row(ffn_norm_g[l]), pw_b)
        experts_t, gates_t = _peer_route(cfg, qp, k1_b, k2_b)
        idx = experts_t.reshape(E, Tc // R, R).transpose(1, 0, 2).reshape(-1)
        return qks, vs, state, (h2, gates_t.T, gather_fn(table, idx))

    def apply(item):
        h2, gates, rows = item
        return _peer_apply(cfg, h2, gates, rows, row(ffn_norm_g[l]), row(final_norm_g)).reshape(B, Lc, D)

    qks, vs, outs, pending = [], [], [], None
    for c in range(cfg.n_chunks):
        qks, vs, state, item = dense(c, qks, vs, state)
        if pending is not None:
            outs.append(apply(pending))
        pending = item
    outs.append(apply(pending))
    return jnp.concatenate(outs, axis=1)


def kernel(x, mem, positions, mix_norm_g, w_in, lam_q1, lam_k1, lam_q2, lam_k2, attn_head_g, ssm_a_re, ssm_a_im, ssm_log_dt, ssm_b_re, ssm_b_im, ssm_c_re, ssm_c_im, ssm_d, glu_w, glu_b, ssm_out_g, w_out, xattn_norm_g, mem_norm_g, xattn_wq, xattn_wkv, xattn_wo, ffn_norm_g, peer_wq, peer_k1, peer_k2, peer_u, peer_v, final_norm_g):
    cfg = Cfg()
    return _block(cfg, functools.partial(_sc_gather, cfg), x, mem, positions, mix_norm_g, w_in, lam_q1, lam_k1,
                  lam_q2, lam_k2, attn_head_g, ssm_a_re, ssm_a_im, ssm_log_dt, ssm_b_re, ssm_b_im, ssm_c_re,
                  ssm_c_im, ssm_d, glu_w, glu_b, ssm_out_g, w_out, xattn_norm_g, mem_norm_g, xattn_wq,
                  xattn_wkv, xattn_wo, ffn_norm_g, peer_wq, peer_k1, peer_k2, peer_u, peer_v, final_norm_g)
```

```python
import dataclasses
import functools
import math

import jax
import jax.numpy as jnp
from jax import lax
from jax.experimental import pallas as pl
from jax.experimental.pallas import tpu as pltpu
from jax.experimental.pallas import tpu_sc as plsc

f32 = jnp.float32
bf16 = jnp.bfloat16
i32 = jnp.int32

LANES = 128
SUBLANES = 8
VMEM_LIMIT = 56 * 1024 * 1024
NEG = -1e30
EPS = 1e-6


@dataclasses.dataclass(frozen=True)
class Cfg:
    d_model: int = 2048
    batch: int = 8
    seq: int = 2048
    n_mem: int = 256
    diff_heads: int = 8
    ssm_group: int = 16
    ssm_state: int = 64
    xattn_heads: int = 4
    xattn_head_dim: int = 128
    peer_heads: int = 8
    peer_keys: int = 128
    peer_qdim: int = 256
    peer_topk: int = 16
    rope_theta: float = 500000.0
    lam_init: float = 0.8 - 0.6 * math.exp(-0.3 * 0)
    chunk: int = 256
    scan_chunk: int = 128
    route_tm: int = 256
    gather_window: int = 8
    gather_bufs: int = 4
    apply_groups: int = 2

    @property
    def n_chunks(self):
        return self.seq // self.chunk

    @property
    def attn_width(self):
        return self.d_model // 2

    @property
    def ssm_width(self):
        return self.d_model - self.attn_width

    @property
    def diff_vdim(self):
        return self.attn_width // self.diff_heads

    @property
    def diff_qkdim(self):
        return self.diff_vdim // 2

    @property
    def rot_dim(self):
        return self.diff_qkdim // 4

    @property
    def ssm_groups(self):
        return self.ssm_width // self.ssm_group

    @property
    def xattn_width(self):
        return self.xattn_heads * self.xattn_head_dim

    @property
    def tokens(self):
        return self.batch * self.seq

    @property
    def slots(self):
        return self.peer_heads * self.peer_topk


def _cparams(*sem):
    return pltpu.CompilerParams(dimension_semantics=sem, vmem_limit_bytes=VMEM_LIMIT)


def _resident(shape):
    nd = len(shape)
    return pl.BlockSpec(shape, lambda *_: (0,) * nd, pipeline_mode=pl.Buffered(1))


def _rms(x, g):
    return x * lax.rsqrt(jnp.mean(x * x, axis=-1, keepdims=True) + EPS) * g


def _gelu(x):
    return 0.5 * x * (1.0 + lax.erf(x * (2.0 ** -0.5)))


def _in_proj_kernel(x_ref, pos_ref, g_ref, freq_ref, w_ref, qk_ref, v_ref, u_ref, *, cfg):
    n_qk, n_v = 2 * cfg.attn_width, cfg.attn_width
    half = cfg.rot_dim // 2
    xn = _rms(x_ref[...], g_ref[...]).astype(bf16)
    ang = pos_ref[...] * freq_ref[...]
    cos, sin = jnp.cos(ang), jnp.sin(ang)
    lane = lax.broadcasted_iota(i32, (1, LANES), 1) % cfg.diff_qkdim
    sin_lo = jnp.where(lane < half, -sin, 0.0)
    sin_hi = jnp.where((lane >= half) & (lane < 2 * half), sin, 0.0)
    cw = 2 * LANES
    for c in range((n_qk + n_v + cfg.ssm_width) // cw):
        col = c * cw
        z = jnp.dot(xn, w_ref[:, col:col + cw], preferred_element_type=f32)
        if col < n_qk:
            for k in range(cw // LANES):
                zk = z[:, k * LANES:(k + 1) * LANES]
                zk = zk * cos + pltpu.roll(zk, LANES - half, 1) * sin_lo + pltpu.roll(zk, half, 1) * sin_hi
                qk_ref[:, col + k * LANES:col + (k + 1) * LANES] = zk.astype(bf16)
        elif col < n_qk + n_v:
            v_ref[:, col - n_qk:col - n_qk + cw] = z.astype(bf16)
        else:
            u_ref[:, col - n_qk - n_v:col - n_qk - n_v + cw] = z


def _chunk_rows(cfg, c):
    return lambda b: (b * cfg.n_chunks + c, 0)


def _in_proj(cfg, c, x2, pos, g, freq, w):
    D = x2.shape[1]
    B, tm = cfg.batch, cfg.chunk
    n_qk, n_v, n_u = 2 * cfg.attn_width, cfg.attn_width, cfg.ssm_width
    row = lambda b: (b, 0)
    return pl.pallas_call(
        functools.partial(_in_proj_kernel, cfg=cfg),
        grid=(B,),
        in_specs=[pl.BlockSpec((tm, D), _chunk_rows(cfg, c)), pl.BlockSpec((tm, 1), _chunk_rows(cfg, c)),
                  _resident((1, D)), _resident((1, LANES)), _resident(w.shape)],
        out_specs=[pl.BlockSpec((tm, n_qk), row), pl.BlockSpec((tm, n_v), row), pl.BlockSpec((tm, n_u), row)],
        out_shape=[jax.ShapeDtypeStruct((B * tm, n_qk), bf16), jax.ShapeDtypeStruct((B * tm, n_v), bf16),
                   jax.ShapeDtypeStruct((B * tm, n_u), f32)],
        compiler_params=_cparams("parallel"),
        name="in_proj",
    )(x2, pos, g, freq, w)


def _diff_attn_kernel(lam_ref, q_ref, *rest, cfg, n_kv):
    k_refs, v_refs = rest[:n_kv], rest[n_kv:2 * n_kv]
    g_ref, o_ref, m_ref, l_ref, acc_ref = rest[2 * n_kv:]
    tq = cfg.chunk
    d = cfg.diff_qkdim
    lane = lax.broadcasted_iota(i32, (1, LANES), 1)
    q = q_ref[...].astype(f32) * (d ** -0.5)
    qs = (jnp.where(lane < d, q, 0.0).astype(bf16), jnp.where(lane >= d, q, 0.0).astype(bf16))
    m_ref[...] = jnp.full(m_ref.shape, NEG, f32)
    l_ref[...] = jnp.zeros(l_ref.shape, f32)
    acc_ref[...] = jnp.zeros(acc_ref.shape, f32)
    causal = (lax.broadcasted_iota(i32, (tq, tq), 1) <= lax.broadcasted_iota(i32, (tq, tq), 0))

    def step(j, masked):
        kb = k_refs[j][...]
        vb = v_refs[j][...]
        for c in range(2):
            s = lax.dot_general(qs[c], kb, (((1,), (1,)), ((), ())), preferred_element_type=f32)
            if masked:
                s = jnp.where(causal, s, NEG)
            m_old = m_ref[c]
            m_new = jnp.maximum(m_old, jnp.max(s, axis=1, keepdims=True))
            alpha = jnp.exp(m_old - m_new)
            p = jnp.exp(s - m_new)
            l_ref[c] = alpha * l_ref[c] + jnp.sum(p, axis=1, keepdims=True)
            acc_ref[c] = alpha * acc_ref[c] + jnp.dot(p.astype(bf16), vb, preferred_element_type=f32)
            m_ref[c] = m_new

    for j in range(n_kv - 1):
        step(j, False)
    step(n_kv - 1, True)

    lp = lam_ref[...]
    lam = (jnp.exp(jnp.sum(lp[0:1] * lp[1:2], axis=1, keepdims=True))
           - jnp.exp(jnp.sum(lp[2:3] * lp[3:4], axis=1, keepdims=True)) + cfg.lam_init)
    o = acc_ref[0] / l_ref[0] - lam * (acc_ref[1] / l_ref[1])
    o_ref[...] = (_rms(o, g_ref[...]) * (1.0 - cfg.lam_init)).astype(o_ref.dtype)


def _diff_attn(cfg, lam_p, qks, vs, g):
    n_kv = len(qks)
    B, H, tq = cfg.batch, cfg.diff_heads, cfg.chunk
    head = lambda b, h: (b, h)
    key = lambda b, h: (b, H + h)
    return pl.pallas_call(
        functools.partial(_diff_attn_kernel, cfg=cfg, n_kv=n_kv),
        grid=(B, H),
        in_specs=([_resident(lam_p.shape), pl.BlockSpec((tq, LANES), head)]
                  + [pl.BlockSpec((tq, LANES), key)] * n_kv + [pl.BlockSpec((tq, LANES), head)] * n_kv
                  + [_resident((1, LANES))]),
        out_specs=pl.BlockSpec((tq, LANES), head),
        out_shape=jax.ShapeDtypeStruct((B * tq, cfg.attn_width), bf16),
        scratch_shapes=[pltpu.VMEM((2, tq, 1), f32), pltpu.VMEM((2, tq, 1), f32),
                        pltpu.VMEM((2, tq, LANES), f32)],
        compiler_params=_cparams("parallel", "parallel"),
        name="diff_attn",
    )(lam_p, qks[-1], *qks, *vs, g)


def _s5_scan_kernel(u_ref, are_ref, aim_ref, ldt_ref, bre_ref, bim_ref, cre_ref, cim_ref, d_ref,
                    sin_re_ref, sin_im_ref, y_ref, sout_re_ref, sout_im_ref,
                    ab_ref, bbar_re_ref, bbar_im_ref, bu_re_ref, bu_im_ref, *, cfg):
    B, L = cfg.batch, cfg.scan_chunk
    nt = bu_re_ref.shape[0]
    t = pl.program_id(1)
    st_re_ref, st_im_ref = sout_re_ref.at[0], sout_im_ref.at[0]

    @pl.when(t == 0)
    def _():
        dt = jnp.exp(ldt_ref[0])
        lre, lim = are_ref[0], aim_ref[0]
        mag = jnp.exp(lre * dt)
        ang = lim * dt
        ab_re, ab_im = mag * jnp.cos(ang), mag * jnp.sin(ang)
        den = lre * lre + lim * lim
        f_re = ((ab_re - 1.0) * lre + ab_im * lim) / den
        f_im = (ab_im * lre - (ab_re - 1.0) * lim) / den
        ab_ref[0] = jnp.broadcast_to(ab_re, ab_ref.shape[1:])
        ab_ref[1] = jnp.broadcast_to(ab_im, ab_ref.shape[1:])
        br, bi = bre_ref[0], bim_ref[0]
        bbar_re_ref[...] = (f_re * br - f_im * bi).astype(bf16)
        bbar_im_ref[...] = (f_re * bi + f_im * br).astype(bf16)
        st_re_ref[...] = sin_re_ref[0]
        st_im_ref[...] = sin_im_ref[0]

    u = u_ref[...].reshape(B * L, u_ref.shape[2])
    ub = u.astype(bf16)
    bu_re = jnp.dot(ub, bbar_re_ref[...], preferred_element_type=f32)
    bu_im = jnp.dot(ub, bbar_im_ref[...], preferred_element_type=f32)
    for j in range(nt):
        bu_re_ref[j] = bu_re[:, j * LANES:(j + 1) * LANES]
        bu_im_ref[j] = bu_im[:, j * LANES:(j + 1) * LANES]

    a_re = [ab_ref[0, :, j * LANES:(j + 1) * LANES] for j in range(nt)]
    a_im = [ab_ref[1, :, j * LANES:(j + 1) * LANES] for j in range(nt)]

    def body(s, carry):
        xr, xi = carry
        nr, ni = [], []
        for j in range(nt):
            rows = pl.ds(s, B, stride=L)
            r = a_re[j] * xr[j] - a_im[j] * xi[j] + bu_re_ref[j, rows, :]
            m = a_re[j] * xi[j] + a_im[j] * xr[j] + bu_im_ref[j, rows, :]
            bu_re_ref[j, rows, :] = r
            bu_im_ref[j, rows, :] = m
            nr.append(r)
            ni.append(m)
        return tuple(nr), tuple(ni)

    init = (tuple(st_re_ref[j] for j in range(nt)), tuple(st_im_ref[j] for j in range(nt)))
    xr, xi = lax.fori_loop(0, L, body, init, unroll=8)
    for j in range(nt):
        st_re_ref[j] = xr[j]
        st_im_ref[j] = xi[j]

    xs_re = jnp.concatenate([bu_re_ref[j] for j in range(nt)], axis=1).astype(bf16)
    xs_im = jnp.concatenate([bu_im_ref[j] for j in range(nt)], axis=1).astype(bf16)
    y = (jnp.dot(xs_re, cre_ref[0].astype(bf16), preferred_element_type=f32)
         - jnp.dot(xs_im, cim_ref[0].astype(bf16), preferred_element_type=f32)
         + d_ref[0] * u)
    y_ref[...] = y.reshape(y_ref.shape)


def _block_diag(w, blocks):
    G, r, c = w.shape
    w4 = w.reshape(G // blocks, blocks, r, c)
    eye = jnp.eye(blocks, dtype=w.dtype)
    out = w4[:, :, :, None, :] * eye[None, :, None, :, None]
    return out.reshape(G // blocks, blocks * r, blocks * c)


def _s5_params(cfg, a_re, a_im, log_dt, b_re, b_im, c_re, c_im, d_skip):
    G, P, Hc = cfg.ssm_groups, cfg.ssm_state, cfg.ssm_group
    gps = min(G, 2 * LANES // Hc)
    ns = G // gps
    wu, wn = gps * Hc, gps * P
    return (a_re.reshape(ns, 1, wn), a_im.reshape(ns, 1, wn), jnp.repeat(log_dt, P).reshape(ns, 1, wn),
            _block_diag(b_re.transpose(0, 2, 1), gps), _block_diag(b_im.transpose(0, 2, 1), gps),
            _block_diag(c_re.transpose(0, 2, 1), gps), _block_diag(c_im.transpose(0, 2, 1), gps),
            d_skip.reshape(ns, 1, wu))


def _s5_scan(cfg, u3, params, state):
    B, steps, W = u3.shape
    L = cfg.scan_chunk
    ns, _, wn = params[0].shape
    wu = params[-1].shape[2]
    nt = wn // LANES
    slab = lambda s, t: (s, 0, 0)
    st = pl.BlockSpec((1, nt, B, LANES), lambda s, t: (s, 0, 0, 0))
    st_shape = jax.ShapeDtypeStruct((ns, nt, B, LANES), f32)
    y, s_re, s_im = pl.pallas_call(
        functools.partial(_s5_scan_kernel, cfg=cfg),
        grid=(ns, steps // L),
        in_specs=[pl.BlockSpec((B, L, wu), lambda s, t: (0, t, s)),
                  pl.BlockSpec((1, 1, wn), slab), pl.BlockSpec((1, 1, wn), slab), pl.BlockSpec((1, 1, wn), slab),
                  pl.BlockSpec((1, wu, wn), slab), pl.BlockSpec((1, wu, wn), slab),
                  pl.BlockSpec((1, wn, wu), slab), pl.BlockSpec((1, wn, wu), slab),
                  pl.BlockSpec((1, 1, wu), slab), st, st],
        out_specs=[pl.BlockSpec((B, L, wu), lambda s, t: (0, t, s)), st, st],
        out_shape=[jax.ShapeDtypeStruct((B, steps, W), f32), st_shape, st_shape],
        scratch_shapes=[pltpu.VMEM((2, SUBLANES, wn), f32),
                        pltpu.VMEM((wu, wn), bf16), pltpu.VMEM((wu, wn), bf16),
                        pltpu.VMEM((nt, B * L, LANES), f32), pltpu.VMEM((nt, B * L, LANES), f32)],
        compiler_params=_cparams("arbitrary", "arbitrary"),
        name="s5_scan",
    )(u3, *params, *state)
    return y, (s_re, s_im)


def _s5_glu_kernel(y_ref, w_ref, b_ref, g_ref, o_ref):
    g = _gelu(y_ref[...])
    z = jnp.dot(g.astype(bf16), w_ref[...], preferred_element_type=f32) + b_ref[...]
    g = g * jax.nn.sigmoid(z)
    o_ref[...] = _rms(g, g_ref[...]).astype(o_ref.dtype)


def _s5_glu(cfg, y2, w, b, g):
    T, W = y2.shape
    tm = cfg.chunk
    row = lambda i: (i, 0)
    return pl.pallas_call(
        _s5_glu_kernel,
        grid=(T // tm,),
        in_specs=[pl.BlockSpec((tm, W), row), _resident(w.shape), _resident((1, W)), _resident((1, W))],
        out_specs=pl.BlockSpec((tm, W), row),
        out_shape=jax.ShapeDtypeStruct((T, W), bf16),
        compiler_params=_cparams("parallel"),
        name="s5_glu",
    )(y2, w, b, g)


def _out_proj_kernel(x_ref, att_ref, ssm_ref, w_ref, g_ref, h_ref, hn_ref, *, cfg):
    aw = cfg.attn_width
    h = (x_ref[...]
         + jnp.dot(att_ref[...], w_ref[:aw, :], preferred_element_type=f32)
         + jnp.dot(ssm_ref[...], w_ref[aw:, :], preferred_element_type=f32))
    h_ref[...] = h
    hn_ref[...] = _rms(h, g_ref[...]).astype(hn_ref.dtype)


def _out_proj(cfg, c, x2, att, ssm, w, g):
    D = x2.shape[1]
    tm = cfg.chunk
    T = cfg.batch * tm
    row = lambda i: (i, 0)
    return pl.pallas_call(
        functools.partial(_out_proj_kernel, cfg=cfg),
        grid=(T // tm,),
        in_specs=[pl.BlockSpec((tm, D), _chunk_rows(cfg, c)), pl.BlockSpec((tm, cfg.attn_width), row),
                  pl.BlockSpec((tm, cfg.ssm_width), row), _resident(w.shape), _resident((1, D))],
        out_specs=[pl.BlockSpec((tm, D), row), pl.BlockSpec((tm, D), row)],
        out_shape=[jax.ShapeDtypeStruct((T, D), f32), jax.ShapeDtypeStruct((T, D), bf16)],
        compiler_params=_cparams("parallel"),
        name="out_proj",
    )(x2, att, ssm, w, g)


def _mem_kv_kernel(m_ref, g_ref, w_ref, o_ref):
    mn = _rms(m_ref[...], g_ref[...]).astype(bf16)
    o_ref[...] = jnp.dot(mn, w_ref[...], preferred_element_type=f32).astype(o_ref.dtype)


def _mem_kv(cfg, mem2, g, w):
    R, D = mem2.shape
    tm = cfg.n_mem
    row = lambda i: (i, 0)
    return pl.pallas_call(
        _mem_kv_kernel,
        grid=(R // tm,),
        in_specs=[pl.BlockSpec((tm, D), row), _resident((1, D)), _resident(w.shape)],
        out_specs=pl.BlockSpec((tm, w.shape[1]), row),
        out_shape=jax.ShapeDtypeStruct((R, w.shape[1]), bf16),
        compiler_params=_cparams("parallel"),
        name="mem_kv",
    )(mem2, g, w)


def _xattn_kernel(h_ref, hn_ref, kv_ref, wq_ref, wo_ref, g_ref, pw_ref, h2_ref, qp_ref, *, cfg):
    nh, hd, xw = cfg.xattn_heads, cfg.xattn_head_dim, cfg.xattn_width
    q = jnp.dot(hn_ref[...], wq_ref[...], preferred_element_type=f32).astype(bf16)
    outs = []
    for h in range(nh):
        qh = q[:, h * hd:(h + 1) * hd]
        kh = kv_ref[:, h * hd:(h + 1) * hd]
        vh = kv_ref[:, xw + h * hd:xw + (h + 1) * hd]
        s = lax.dot_general(qh, kh, (((1,), (1,)), ((), ())), preferred_element_type=f32) * (hd ** -0.5)
        p = jnp.exp(s - jnp.max(s, axis=1, keepdims=True))
        p = p / jnp.sum(p, axis=1, keepdims=True)
        outs.append(jnp.dot(p.astype(bf16), vh, preferred_element_type=f32).astype(bf16))
    o = jnp.concatenate(outs, axis=1)
    h2 = h_ref[...] + jnp.dot(o, wo_ref[...], preferred_element_type=f32)
    h2_ref[...] = h2
    hp = _rms(h2, g_ref[...]).astype(bf16)
    qp_ref[...] = jnp.dot(hp, pw_ref[...], preferred_element_type=f32).astype(qp_ref.dtype)


def _xattn(cfg, h1, hn, kv, wq, wo, g, pw):
    T, D = h1.shape
    tm, M = cfg.chunk, cfg.n_mem
    row = lambda i: (i, 0)
    return pl.pallas_call(
        functools.partial(_xattn_kernel, cfg=cfg),
        grid=(T // tm,),
        in_specs=[pl.BlockSpec((tm, D), row), pl.BlockSpec((tm, D), row),
                  pl.BlockSpec((M, kv.shape[1]), row),
                  _resident(wq.shape), _resident(wo.shape), _resident((1, D)), _resident(pw.shape)],
        out_specs=[pl.BlockSpec((tm, D), row), pl.BlockSpec((tm, pw.shape[1]), row)],
        out_shape=[jax.ShapeDtypeStruct((T, D), f32), jax.ShapeDtypeStruct((T, pw.shape[1]), bf16)],
        compiler_params=_cparams("parallel"),
        name="xattn",
    )(h1, hn, kv, wq, wo, g, pw)


def _top_rows(s, k, payload=None):
    n = s.shape[0]
    rows = lax.broadcasted_iota(i32, s.shape, 0)
    vals, picks = [], []
    for _ in range(k):
        m = jnp.max(s, axis=0, keepdims=True)
        idx = jnp.min(jnp.where(s == m, rows, n), axis=0, keepdims=True)
        sel = rows == idx
        vals.append(m)
        picks.append(idx if payload is None else jnp.max(jnp.where(sel, payload, -1), axis=0, keepdims=True))
        s = jnp.where(sel, NEG, s)
    return vals, picks


def _peer_route_kernel(qp_ref, k1_ref, k2_ref, e_ref, g_ref, *, cfg):
    K, nk, half = cfg.peer_topk, cfg.peer_keys, cfg.peer_qdim // 2
    tm = qp_ref.shape[0]
    dn = (((1,), (1,)), ((), ()))
    s1 = lax.dot_general(k1_ref[0], qp_ref[:, :half], dn, preferred_element_type=f32)
    s2 = lax.dot_general(k2_ref[0], qp_ref[:, half:], dn, preferred_element_type=f32)
    t1, i1 = _top_rows(s1, K)
    t2, i2 = _top_rows(s2, K)
    t1, i1 = jnp.concatenate(t1, axis=0), jnp.concatenate(i1, axis=0)
    t2, i2 = jnp.concatenate(t2, axis=0), jnp.concatenate(i2, axis=0)
    cand = (t1[:, None, :] + t2[None, :, :]).reshape(K * K, tm)
    expert = (i1[:, None, :] * nk + i2[None, :, :]).reshape(K * K, tm)
    ts, es = _top_rows(cand, K, payload=expert)
    ts = jnp.concatenate(ts, axis=0)
    p = jnp.exp(ts - ts[0:1])
    g_ref[...] = p / jnp.sum(p, axis=0, keepdims=True)
    e_ref[...] = jnp.concatenate(es, axis=0)


def _peer_route(cfg, qp, k1, k2):
    T = qp.shape[0]
    tm, H, K, Q = cfg.route_tm, cfg.peer_heads, cfg.peer_topk, cfg.peer_qdim
    out = pl.BlockSpec((K, tm), lambda i, h: (h, i))
    return pl.pallas_call(
        functools.partial(_peer_route_kernel, cfg=cfg),
        grid=(T // tm, H),
        in_specs=[pl.BlockSpec((tm, Q), lambda i, h: (i, h)),
                  pl.BlockSpec((1,) + k1.shape[1:], lambda i, h: (h, 0, 0)),
                  pl.BlockSpec((1,) + k2.shape[1:], lambda i, h: (h, 0, 0))],
        out_specs=[out, out],
        out_shape=[jax.ShapeDtypeStruct((H * K, T), i32), jax.ShapeDtypeStruct((H * K, T), f32)],
        compiler_params=_cparams("parallel", "parallel"),
        name="peer_route",
    )(qp, k1, k2)


def _sc_gather(cfg, table, idx):
    n = idx.shape[0]
    d = table.shape[1]
    win, nb = cfg.gather_window, cfg.gather_bufs
    mesh = plsc.VectorSubcoreMesh(core_axis_name="core", subcore_axis_name="subcore")
    workers = mesh.num_cores * mesh.num_subcores
    per = n // workers
    steps = per // win
    assert per * workers == n and steps * win == per and steps % nb == 0

    @functools.partial(pl.kernel, out_type=jax.ShapeDtypeStruct((n, d), table.dtype), mesh=mesh,
                       scratch_types=[pltpu.VMEM((per,), i32)] + [pltpu.VMEM((win, d), table.dtype)] * nb
                       + [pltpu.SemaphoreType.DMA((nb,)), pltpu.SemaphoreType.DMA((nb,))])
    def gather_kernel(tab_hbm, idx_hbm, out_hbm, idx_vmem, *rest):
        bufs, gsem, wsem = rest[:nb], rest[nb], rest[nb + 1]
        wid = lax.axis_index("core") * mesh.num_subcores + lax.axis_index("subcore")
        base = wid * per
        pltpu.sync_copy(idx_hbm.at[pl.ds(base, per)], idx_vmem)

        def gather(k, b):
            return pltpu.make_async_copy(tab_hbm.at[idx_vmem.at[pl.ds(k * win, win)]], bufs[b], gsem.at[b])

        def write(k, b):
            return pltpu.make_async_copy(bufs[b], out_hbm.at[pl.ds(base + k * win, win)], wsem.at[b])

        for b in range(nb - 1):
            gather(b, b).start()

        @pl.loop(0, steps, step=nb)
        def _(k0):
            for b in range(nb):
                k = k0 + b
                pb = (b - 1) % nb
                gather(k, b).wait()
                write(k, b).start()

                @pl.when(k >= 1)
                def _():
                    write(k - 1, pb).wait()

                @pl.when(k + nb - 1 < steps)
                def _():
                    gather(k + nb - 1, pb).start()

        write(steps - 1, (steps - 1) % nb).wait()

    return gather_kernel(table, idx)


def _pack_bf16_halves(a):
    h = a.shape[1] // 2
    b = lax.bitcast_convert_type(a.astype(bf16), jnp.uint16).astype(jnp.uint32)
    return b[:, :h] | (b[:, h:] << 16)


def _unpack_bf16_halves(words):
    return pltpu.bitcast(words << 16, f32), pltpu.bitcast(words & jnp.uint32(0xFFFF0000), f32)


def _split_bf16(a):
    hi = a.astype(bf16)
    lo = (a - hi.astype(f32)).astype(bf16)
    return jnp.concatenate([hi, lo], axis=0)


def _fold_rows(a):
    n = a.shape[0] // 2
    return a[:n] + a[n:]


def _peer_apply_kernel(h_ref, gt_ref, rows_ref, sel_ref, selt_ref, gn_ref, gf_ref, o_ref, *, cfg):
    E = cfg.slots
    R = SUBLANES
    D = h_ref.shape[1]
    nw = D // 2
    nz = 2 * R * E
    lane = lax.broadcasted_iota(i32, (2 * R, nz), 1)
    rowi = lax.broadcasted_iota(i32, (2 * R, nz), 0)
    mine = ((lane // 2) % R == rowi % R) & (lane % 2 == rowi // R)
    nt_dims = (((1,), (1,)), ((), ()))

    for gi in range(cfg.apply_groups):
        tok = slice(gi * R, (gi + 1) * R)
        zu = pltpu.bitcast(rows_ref[gi * E * R:(gi + 1) * E * R, :nw], bf16)
        zv = pltpu.bitcast(rows_ref[gi * E * R:(gi + 1) * E * R, nw:], bf16)
        h2 = h_ref[tok, :]
        x = _rms(h2, gn_ref[...])
        xs = _split_bf16(jnp.concatenate([x[:, :nw], x[:, nw:]], axis=0))
        a = _fold_rows(lax.dot_general(xs, zu, nt_dims, preferred_element_type=f32))
        a = jnp.where(mine, a, 0.0)
        a = a[:R] + a[R:]
        act = _fold_rows(jnp.dot(_split_bf16(a), sel_ref[...], preferred_element_type=f32))
        w = gt_ref[tok, :] * _gelu(act)
        w_rows = _fold_rows(jnp.dot(_split_bf16(w), selt_ref[...], preferred_element_type=f32))
        ws = _split_bf16(jnp.where(mine, jnp.concatenate([w_rows, w_rows], axis=0), 0.0))
        y2 = _fold_rows(jnp.dot(ws, zv, preferred_element_type=f32))
        y = jnp.concatenate([y2[:R], y2[R:]], axis=1)
        o_ref[tok, :] = _rms(h2 + y, gf_ref[...])


def _peer_apply(cfg, h2, gates, rows, gn, gf):
    T, D = h2.shape
    E, R, G = cfg.slots, SUBLANES, cfg.apply_groups
    assert E == LANES and rows.shape == (T * E, D)
    nz = 2 * R * E
    sel = (jnp.arange(nz)[:, None] // (2 * R) == jnp.arange(E)[None, :]).astype(bf16)
    row = lambda i: (i, 0)
    return pl.pallas_call(
        functools.partial(_peer_apply_kernel, cfg=cfg),
        grid=(T // (R * G),),
        in_specs=[pl.BlockSpec((R * G, D), row), pl.BlockSpec((R * G, E), row),
                  pl.BlockSpec((G * E * R, D), row), _resident((nz, E)), _resident((E, nz)),
                  _resident((1, D)), _resident((1, D))],
        out_specs=pl.BlockSpec((R * G, D), row),
        out_shape=jax.ShapeDtypeStruct((T, D), f32),
        compiler_params=_cparams("parallel"),
        name="peer_apply",
    )(h2, gates, rows, sel, sel.T, gn, gf)


def _block(cfg, gather_fn, x, mem, positions, mix_norm_g, w_in, lam_q1, lam_k1, lam_q2, lam_k2, attn_head_g,
           ssm_a_re, ssm_a_im, ssm_log_dt, ssm_b_re, ssm_b_im, ssm_c_re, ssm_c_im, ssm_d, glu_w, glu_b,
           ssm_out_g, w_out, xattn_norm_g, mem_norm_g, xattn_wq, xattn_wkv, xattn_wo, ffn_norm_g,
           peer_wq, peer_k1, peer_k2, peer_u, peer_v, final_norm_g):
    B, S, D = x.shape
    T = B * S
    l = 0
    row = lambda a: a.reshape(1, -1)
    x2 = x.reshape(T, D)
    pos = positions.reshape(T, 1).astype(f32)
    freqs = cfg.rope_theta ** (-jnp.arange(0, cfg.rot_dim, 2, dtype=f32) / cfg.rot_dim)
    lane = jnp.arange(LANES) % cfg.diff_qkdim
    freq_row = jnp.where(lane < cfg.rot_dim, freqs[lane % (cfg.rot_dim // 2)], 0.0).reshape(1, LANES)

    lam_p = jnp.stack([lam_q1[l], lam_k1[l], lam_q2[l], lam_k2[l]])
    s5_params = _s5_params(cfg, ssm_a_re[l], ssm_a_im[l], ssm_log_dt[l], ssm_b_re[l], ssm_b_im[l],
                           ssm_c_re[l], ssm_c_im[l], ssm_d[l])
    ns, _, wn = s5_params[0].shape
    zero_state = jnp.zeros((ns, wn // LANES, B, LANES), f32)
    state = (zero_state, zero_state)
    w_in_b, glu_w_b, w_out_b = w_in[l].astype(bf16), glu_w[l].astype(bf16), w_out[l].astype(bf16)
    wq_b, wo_b, pw_b = xattn_wq[l].astype(bf16), xattn_wo[l].astype(bf16), peer_wq[l].astype(bf16)
    k1_b, k2_b = peer_k1[l].astype(bf16), peer_k2[l].astype(bf16)
    kv = _mem_kv(cfg, mem.reshape(B * cfg.n_mem, D), row(mem_norm_g[l]), xattn_wkv[l].astype(bf16))
    table = jnp.concatenate([_pack_bf16_halves(peer_u[l]), _pack_bf16_halves(peer_v[l])], axis=1)
    E, R = cfg.slots, SUBLANES
    Lc = cfg.chunk
    Tc = B * Lc

    def dense(c, qks, vs, state):
        qk, v, u = _in_proj(cfg, c, x2, pos, row(mix_norm_g[l]), freq_row, w_in_b)
        qks, vs = qks + [qk], vs + [v]
        att = _diff_attn(cfg, lam_p, qks, vs, row(attn_head_g[l]))
        y, state = _s5_scan(cfg, u.reshape(B, Lc, cfg.ssm_width), s5_params, state)
        ssm = _s5_glu(cfg, y.reshape(Tc, cfg.ssm_width), glu_w_b, row(glu_b[l]), row(ssm_out_g[l]))
        h1, hn = _out_proj(cfg, c, x2, att, ssm, w_out_b, row(xattn_norm_g[l]))
        h2, qp = _xattn(cfg, h1, hn, kv, wq_b, wo_b, row(ffn_norm_g[l]), pw_b)
        experts_t, gates_t = _peer_route(cfg, qp, k1_b, k2_b)
        idx = experts_t.reshape(E, Tc // R, R).transpose(1, 0, 2).reshape(-1)
        return qks, vs, state, (h2, gates_t.T, gather_fn(table, idx))

    def apply(item):
        h2, gates, rows = item
        return _peer_apply(cfg, h2, gates, rows, row(ffn_norm_g[l]), row(final_norm_g)).reshape(B, Lc, D)

    qks, vs, outs, pending = [], [], [], None
    for c in range(cfg.n_chunks):
        qks, vs, state, item = dense(c, qks, vs, state)
        if pending is not None:
            outs.append(apply(pending))
        pending = item
    outs.append(apply(pending))
    return jnp.concatenate(outs, axis=1)


def kernel(x, mem, positions, mix_norm_g, w_in, lam_q1, lam_k1, lam_q2, lam_k2, attn_head_g, ssm_a_re, ssm_a_im, ssm_log_dt, ssm_b_re, ssm_b_im, ssm_c_re, ssm_c_im, ssm_d, glu_w, glu_b, ssm_out_g, w_out, xattn_norm_g, mem_norm_g, xattn_wq, xattn_wkv, xattn_wo, ffn_norm_g, peer_wq, peer_k1, peer_k2, peer_u, peer_v, final_norm_g):
    cfg = Cfg()
    return _block(cfg, functools.partial(_sc_gather, cfg), x, mem, positions, mix_norm_g, w_in, lam_q1, lam_k1,
                  lam_q2, lam_k2, attn_head_g, ssm_a_re, ssm_a_im, ssm_log_dt, ssm_b_re, ssm_b_im, ssm_c_re,
                  ssm_c_im, ssm_d, glu_w, glu_b, ssm_out_g, w_out, xattn_norm_g, mem_norm_g, xattn_wq,
                  xattn_wkv, xattn_wo, ffn_norm_g, peer_wq, peer_k1, peer_k2, peer_u, peer_v, final_norm_g)
```

```python
import dataclasses
import functools
import math

import jax
import jax.numpy as jnp
from jax import lax
from jax.experimental import pallas as pl
from jax.experimental.pallas import tpu as pltpu
from jax.experimental.pallas import tpu_sc as plsc

f32 = jnp.float32
bf16 = jnp.bfloat16
i32 = jnp.int32

LANES = 128
SUBLANES = 8
VMEM_LIMIT = 56 * 1024 * 1024
NEG = -1e30
EPS = 1e-6


@dataclasses.dataclass(frozen=True)
class Cfg:
    d_model: int = 2048
    batch: int = 8
    seq: int = 2048
    n_mem: int = 256
    diff_heads: int = 8
    ssm_group: int = 16
    ssm_state: int = 64
    xattn_heads: int = 4
    xattn_head_dim: int = 128
    peer_heads: int = 8
    peer_keys: int = 128
    peer_qdim: int = 256
    peer_topk: int = 16
    rope_theta: float = 500000.0
    lam_init: float = 0.8 - 0.6 * math.exp(-0.3 * 0)
    chunk: int = 256
    scan_chunk: int = 128
    route_tm: int = 256
    gather_window: int = 8
    gather_bufs: int = 7
    gather_writes: int = 2
    apply_groups: int = 2

    @property
    def n_chunks(self):
        return self.seq // self.chunk

    @property
    def attn_width(self):
        return self.d_model // 2

    @property
    def ssm_width(self):
        return self.d_model - self.attn_width

    @property
    def diff_vdim(self):
        return self.attn_width // self.diff_heads

    @property
    def diff_qkdim(self):
        return self.diff_vdim // 2

    @property
    def rot_dim(self):
        return self.diff_qkdim // 4

    @property
    def ssm_groups(self):
        return self.ssm_width // self.ssm_group

    @property
    def xattn_width(self):
        return self.xattn_heads * self.xattn_head_dim

    @property
    def tokens(self):
        return self.batch * self.seq

    @property
    def slots(self):
        return self.peer_heads * self.peer_topk


def _cparams(*sem):
    return pltpu.CompilerParams(dimension_semantics=sem, vmem_limit_bytes=VMEM_LIMIT)


def _resident(shape):
    nd = len(shape)
    return pl.BlockSpec(shape, lambda *_: (0,) * nd, pipeline_mode=pl.Buffered(1))


def _rms(x, g):
    return x * lax.rsqrt(jnp.mean(x * x, axis=-1, keepdims=True) + EPS) * g


def _gelu(x):
    return 0.5 * x * (1.0 + lax.erf(x * (2.0 ** -0.5)))


def _in_proj_kernel(x_ref, pos_ref, g_ref, freq_ref, w_ref, qk_ref, v_ref, u_ref, *, cfg):
    n_qk, n_v = 2 * cfg.attn_width, cfg.attn_width
    half = cfg.rot_dim // 2
    xn = _rms(x_ref[...], g_ref[...]).astype(bf16)
    ang = pos_ref[...] * freq_ref[...]
    cos, sin = jnp.cos(ang), jnp.sin(ang)
    lane = lax.broadcasted_iota(i32, (1, LANES), 1) % cfg.diff_qkdim
    sin_lo = jnp.where(lane < half, -sin, 0.0)
    sin_hi = jnp.where((lane >= half) & (lane < 2 * half), sin, 0.0)
    cw = 2 * LANES
    for c in range((n_qk + n_v + cfg.ssm_width) // cw):
        col = c * cw
        z = jnp.dot(xn, w_ref[:, col:col + cw], preferred_element_type=f32)
        if col < n_qk:
            for k in range(cw // LANES):
                zk = z[:, k * LANES:(k + 1) * LANES]
                zk = zk * cos + pltpu.roll(zk, LANES - half, 1) * sin_lo + pltpu.roll(zk, half, 1) * sin_hi
                qk_ref[:, col + k * LANES:col + (k + 1) * LANES] = zk.astype(bf16)
        elif col < n_qk + n_v:
            v_ref[:, col - n_qk:col - n_qk + cw] = z.astype(bf16)
        else:
            u_ref[:, col - n_qk - n_v:col - n_qk - n_v + cw] = z


def _chunk_rows(cfg, c):
    return lambda b: (b * cfg.n_chunks + c, 0)


def _in_proj(cfg, c, x2, pos, g, freq, w):
    D = x2.shape[1]
    B, tm = cfg.batch, cfg.chunk
    n_qk, n_v, n_u = 2 * cfg.attn_width, cfg.attn_width, cfg.ssm_width
    row = lambda b: (b, 0)
    return pl.pallas_call(
        functools.partial(_in_proj_kernel, cfg=cfg),
        grid=(B,),
        in_specs=[pl.BlockSpec((tm, D), _chunk_rows(cfg, c)), pl.BlockSpec((tm, 1), _chunk_rows(cfg, c)),
                  _resident((1, D)), _resident((1, LANES)), _resident(w.shape)],
        out_specs=[pl.BlockSpec((tm, n_qk), row), pl.BlockSpec((tm, n_v), row), pl.BlockSpec((tm, n_u), row)],
        out_shape=[jax.ShapeDtypeStruct((B * tm, n_qk), bf16), jax.ShapeDtypeStruct((B * tm, n_v), bf16),
                   jax.ShapeDtypeStruct((B * tm, n_u), f32)],
        compiler_params=_cparams("parallel"),
        name="in_proj",
    )(x2, pos, g, freq, w)


def _diff_attn_kernel(lam_ref, q_ref, *rest, cfg, n_kv):
    k_refs, v_refs = rest[:n_kv], rest[n_kv:2 * n_kv]
    g_ref, o_ref, m_ref, l_ref, acc_ref = rest[2 * n_kv:]
    tq = cfg.chunk
    d = cfg.diff_qkdim
    lane = lax.broadcasted_iota(i32, (1, LANES), 1)
    q = q_ref[...].astype(f32) * (d ** -0.5)
    qs = (jnp.where(lane < d, q, 0.0).astype(bf16), jnp.where(lane >= d, q, 0.0).astype(bf16))
    m_ref[...] = jnp.full(m_ref.shape, NEG, f32)
    l_ref[...] = jnp.zeros(l_ref.shape, f32)
    acc_ref[...] = jnp.zeros(acc_ref.shape, f32)
    causal = (lax.broadcasted_iota(i32, (tq, tq), 1) <= lax.broadcasted_iota(i32, (tq, tq), 0))

    def step(j, masked):
        kb = k_refs[j][...]
        vb = v_refs[j][...]
        for c in range(2):
            s = lax.dot_general(qs[c], kb, (((1,), (1,)), ((), ())), preferred_element_type=f32)
            if masked:
                s = jnp.where(causal, s, NEG)
            m_old = m_ref[c]
            m_new = jnp.maximum(m_old, jnp.max(s, axis=1, keepdims=True))
            alpha = jnp.exp(m_old - m_new)
            p = jnp.exp(s - m_new)
            l_ref[c] = alpha * l_ref[c] + jnp.sum(p, axis=1, keepdims=True)
            acc_ref[c] = alpha * acc_ref[c] + jnp.dot(p.astype(bf16), vb, preferred_element_type=f32)
            m_ref[c] = m_new

    for j in range(n_kv - 1):
        step(j, False)
    step(n_kv - 1, True)

    lp = lam_ref[...]
    lam = (jnp.exp(jnp.sum(lp[0:1] * lp[1:2], axis=1, keepdims=True))
           - jnp.exp(jnp.sum(lp[2:3] * lp[3:4], axis=1, keepdims=True)) + cfg.lam_init)
    o = acc_ref[0] / l_ref[0] - lam * (acc_ref[1] / l_ref[1])
    o_ref[...] = (_rms(o, g_ref[...]) * (1.0 - cfg.lam_init)).astype(o_ref.dtype)


def _diff_attn(cfg, lam_p, qks, vs, g):
    n_kv = len(qks)
    B, H, tq = cfg.batch, cfg.diff_heads, cfg.chunk
    head = lambda b, h: (b, h)
    key = lambda b, h: (b, H + h)
    return pl.pallas_call(
        functools.partial(_diff_attn_kernel, cfg=cfg, n_kv=n_kv),
        grid=(B, H),
        in_specs=([_resident(lam_p.shape), pl.BlockSpec((tq, LANES), head)]
                  + [pl.BlockSpec((tq, LANES), key)] * n_kv + [pl.BlockSpec((tq, LANES), head)] * n_kv
                  + [_resident((1, LANES))]),
        out_specs=pl.BlockSpec((tq, LANES), head),
        out_shape=jax.ShapeDtypeStruct((B * tq, cfg.attn_width), bf16),
        scratch_shapes=[pltpu.VMEM((2, tq, 1), f32), pltpu.VMEM((2, tq, 1), f32),
                        pltpu.VMEM((2, tq, LANES), f32)],
        compiler_params=_cparams("parallel", "parallel"),
        name="diff_attn",
    )(lam_p, qks[-1], *qks, *vs, g)


def _s5_scan_kernel(u_ref, are_ref, aim_ref, ldt_ref, bre_ref, bim_ref, cre_ref, cim_ref, d_ref,
                    sin_re_ref, sin_im_ref, y_ref, sout_re_ref, sout_im_ref,
                    ab_ref, bbar_re_ref, bbar_im_ref, bu_re_ref, bu_im_ref, *, cfg):
    B, L = cfg.batch, cfg.scan_chunk
    nt = bu_re_ref.shape[0]
    t = pl.program_id(1)
    st_re_ref, st_im_ref = sout_re_ref.at[0], sout_im_ref.at[0]

    @pl.when(t == 0)
    def _():
        dt = jnp.exp(ldt_ref[0])
        lre, lim = are_ref[0], aim_ref[0]
        mag = jnp.exp(lre * dt)
        ang = lim * dt
        ab_re, ab_im = mag * jnp.cos(ang), mag * jnp.sin(ang)
        den = lre * lre + lim * lim
        f_re = ((ab_re - 1.0) * lre + ab_im * lim) / den
        f_im = (ab_im * lre - (ab_re - 1.0) * lim) / den
        ab_ref[0] = jnp.broadcast_to(ab_re, ab_ref.shape[1:])
        ab_ref[1] = jnp.broadcast_to(ab_im, ab_ref.shape[1:])
        br, bi = bre_ref[0], bim_ref[0]
        bbar_re_ref[...] = (f_re * br - f_im * bi).astype(bf16)
        bbar_im_ref[...] = (f_re * bi + f_im * br).astype(bf16)
        st_re_ref[...] = sin_re_ref[0]
        st_im_ref[...] = sin_im_ref[0]

    u = u_ref[...].reshape(B * L, u_ref.shape[2])
    ub = u.astype(bf16)
    bu_re = jnp.dot(ub, bbar_re_ref[...], preferred_element_type=f32)
    bu_im = jnp.dot(ub, bbar_im_ref[...], preferred_element_type=f32)
    for j in range(nt):
        bu_re_ref[j] = bu_re[:, j * LANES:(j + 1) * LANES]
        bu_im_ref[j] = bu_im[:, j * LANES:(j + 1) * LANES]

    a_re = [ab_ref[0, :, j * LANES:(j + 1) * LANES] for j in range(nt)]
    a_im = [ab_ref[1, :, j * LANES:(j + 1) * LANES] for j in range(nt)]

    def body(s, carry):
        xr, xi = carry
        nr, ni = [], []
        for j in range(nt):
            rows = pl.ds(s, B, stride=L)
            r = a_re[j] * xr[j] - a_im[j] * xi[j] + bu_re_ref[j, rows, :]
            m = a_re[j] * xi[j] + a_im[j] * xr[j] + bu_im_ref[j, rows, :]
            bu_re_ref[j, rows, :] = r
            bu_im_ref[j, rows, :] = m
            nr.append(r)
            ni.append(m)
        return tuple(nr), tuple(ni)

    init = (tuple(st_re_ref[j] for j in range(nt)), tuple(st_im_ref[j] for j in range(nt)))
    xr, xi = lax.fori_loop(0, L, body, init, unroll=8)
    for j in range(nt):
        st_re_ref[j] = xr[j]
        st_im_ref[j] = xi[j]

    xs_re = jnp.concatenate([bu_re_ref[j] for j in range(nt)], axis=1).astype(bf16)
    xs_im = jnp.concatenate([bu_im_ref[j] for j in range(nt)], axis=1).astype(bf16)
    y = (jnp.dot(xs_re, cre_ref[0].astype(bf16), preferred_element_type=f32)
         - jnp.dot(xs_im, cim_ref[0].astype(bf16), preferred_element_type=f32)
         + d_ref[0] * u)
    y_ref[...] = y.reshape(y_ref.shape)


def _block_diag(w, blocks):
    G, r, c = w.shape
    w4 = w.reshape(G // blocks, blocks, r, c)
    eye = jnp.eye(blocks, dtype=w.dtype)
    out = w4[:, :, :, None, :] * eye[None, :, None, :, None]
    return out.reshape(G // blocks, blocks * r, blocks * c)


def _s5_params(cfg, a_re, a_im, log_dt, b_re, b_im, c_re, c_im, d_skip):
    G, P, Hc = cfg.ssm_groups, cfg.ssm_state, cfg.ssm_group
    gps = min(G, 2 * LANES // Hc)
    ns = G // gps
    wu, wn = gps * Hc, gps * P
    return (a_re.reshape(ns, 1, wn), a_im.reshape(ns, 1, wn), jnp.repeat(log_dt, P).reshape(ns, 1, wn),
            _block_diag(b_re.transpose(0, 2, 1), gps), _block_diag(b_im.transpose(0, 2, 1), gps),
            _block_diag(c_re.transpose(0, 2, 1), gps), _block_diag(c_im.transpose(0, 2, 1), gps),
            d_skip.reshape(ns, 1, wu))


def _s5_scan(cfg, u3, params, state):
    B, steps, W = u3.shape
    L = cfg.scan_chunk
    ns, _, wn = params[0].shape
    wu = params[-1].shape[2]
    nt = wn // LANES
    slab = lambda s, t: (s, 0, 0)
    st = pl.BlockSpec((1, nt, B, LANES), lambda s, t: (s, 0, 0, 0))
    st_shape = jax.ShapeDtypeStruct((ns, nt, B, LANES), f32)
    y, s_re, s_im = pl.pallas_call(
        functools.partial(_s5_scan_kernel, cfg=cfg),
        grid=(ns, steps // L),
        in_specs=[pl.BlockSpec((B, L, wu), lambda s, t: (0, t, s)),
                  pl.BlockSpec((1, 1, wn), slab), pl.BlockSpec((1, 1, wn), slab), pl.BlockSpec((1, 1, wn), slab),
                  pl.BlockSpec((1, wu, wn), slab), pl.BlockSpec((1, wu, wn), slab),
                  pl.BlockSpec((1, wn, wu), slab), pl.BlockSpec((1, wn, wu), slab),
                  pl.BlockSpec((1, 1, wu), slab), st, st],
        out_specs=[pl.BlockSpec((B, L, wu), lambda s, t: (0, t, s)), st, st],
        out_shape=[jax.ShapeDtypeStruct((B, steps, W), f32), st_shape, st_shape],
        scratch_shapes=[pltpu.VMEM((2, SUBLANES, wn), f32),
                        pltpu.VMEM((wu, wn), bf16), pltpu.VMEM((wu, wn), bf16),
                        pltpu.VMEM((nt, B * L, LANES), f32), pltpu.VMEM((nt, B * L, LANES), f32)],
        compiler_params=_cparams("arbitrary", "arbitrary"),
        name="s5_scan",
    )(u3, *params, *state)
    return y, (s_re, s_im)


def _s5_glu_kernel(y_ref, w_ref, b_ref, g_ref, o_ref):
    g = _gelu(y_ref[...])
    z = jnp.dot(g.astype(bf16), w_ref[...], preferred_element_type=f32) + b_ref[...]
    g = g * jax.nn.sigmoid(z)
    o_ref[...] = _rms(g, g_ref[...]).astype(o_ref.dtype)


def _s5_glu(cfg, y2, w, b, g):
    T, W = y2.shape
    tm = cfg.chunk
    row = lambda i: (i, 0)
    return pl.pallas_call(
        _s5_glu_kernel,
        grid=(T // tm,),
        in_specs=[pl.BlockSpec((tm, W), row), _resident(w.shape), _resident((1, W)), _resident((1, W))],
        out_specs=pl.BlockSpec((tm, W), row),
        out_shape=jax.ShapeDtypeStruct((T, W), bf16),
        compiler_params=_cparams("parallel"),
        name="s5_glu",
    )(y2, w, b, g)


def _out_proj_kernel(x_ref, att_ref, ssm_ref, w_ref, g_ref, h_ref, hn_ref, *, cfg):
    aw = cfg.attn_width
    h = (x_ref[...]
         + jnp.dot(att_ref[...], w_ref[:aw, :], preferred_element_type=f32)
         + jnp.dot(ssm_ref[...], w_ref[aw:, :], preferred_element_type=f32))
    h_ref[...] = h
    hn_ref[...] = _rms(h, g_ref[...]).astype(hn_ref.dtype)


def _out_proj(cfg, c, x2, att, ssm, w, g):
    D = x2.shape[1]
    tm = cfg.chunk
    T = cfg.batch * tm
    row = lambda i: (i, 0)
    return pl.pallas_call(
        functools.partial(_out_proj_kernel, cfg=cfg),
        grid=(T // tm,),
        in_specs=[pl.BlockSpec((tm, D), _chunk_rows(cfg, c)), pl.BlockSpec((tm, cfg.attn_width), row),
                  pl.BlockSpec((tm, cfg.ssm_width), row), _resident(w.shape), _resident((1, D))],
        out_specs=[pl.BlockSpec((tm, D), row), pl.BlockSpec((tm, D), row)],
        out_shape=[jax.ShapeDtypeStruct((T, D), f32), jax.ShapeDtypeStruct((T, D), bf16)],
        compiler_params=_cparams("parallel"),
        name="out_proj",
    )(x2, att, ssm, w, g)


def _mem_kv_kernel(m_ref, g_ref, w_ref, o_ref):
    mn = _rms(m_ref[...], g_ref[...]).astype(bf16)
    o_ref[...] = jnp.dot(mn, w_ref[...], preferred_element_type=f32).astype(o_ref.dtype)


def _mem_kv(cfg, mem2, g, w):
    R, D = mem2.shape
    tm = cfg.n_mem
    row = lambda i: (i, 0)
    return pl.pallas_call(
        _mem_kv_kernel,
        grid=(R // tm,),
        in_specs=[pl.BlockSpec((tm, D), row), _resident((1, D)), _resident(w.shape)],
        out_specs=pl.BlockSpec((tm, w.shape[1]), row),
        out_shape=jax.ShapeDtypeStruct((R, w.shape[1]), bf16),
        compiler_params=_cparams("parallel"),
        name="mem_kv",
    )(mem2, g, w)


def _xattn_kernel(h_ref, hn_ref, kv_ref, wq_ref, wo_ref, g_ref, pw_ref, h2_ref, qp_ref, *, cfg):
    nh, hd, xw = cfg.xattn_heads, cfg.xattn_head_dim, cfg.xattn_width
    q = jnp.dot(hn_ref[...], wq_ref[...], preferred_element_type=f32).astype(bf16)
    outs = []
    for h in range(nh):
        qh = q[:, h * hd:(h + 1) * hd]
        kh = kv_ref[:, h * hd:(h + 1) * hd]
        vh = kv_ref[:, xw + h * hd:xw + (h + 1) * hd]
        s = lax.dot_general(qh, kh, (((1,), (1,)), ((), ())), preferred_element_type=f32) * (hd ** -0.5)
        p = jnp.exp(s - jnp.max(s, axis=1, keepdims=True))
        p = p / jnp.sum(p, axis=1, keepdims=True)
        outs.append(jnp.dot(p.astype(bf16), vh, preferred_element_type=f32).astype(bf16))
    o = jnp.concatenate(outs, axis=1)
    h2 = h_ref[...] + jnp.dot(o, wo_ref[...], preferred_element_type=f32)
    h2_ref[...] = h2
    hp = _rms(h2, g_ref[...]).astype(bf16)
    qp_ref[...] = jnp.dot(hp, pw_ref[...], preferred_element_type=f32).astype(qp_ref.dtype)


def _xattn(cfg, h1, hn, kv, wq, wo, g, pw):
    T, D = h1.shape
    tm, M = cfg.chunk, cfg.n_mem
    row = lambda i: (i, 0)
    return pl.pallas_call(
        functools.partial(_xattn_kernel, cfg=cfg),
        grid=(T // tm,),
        in_specs=[pl.BlockSpec((tm, D), row), pl.BlockSpec((tm, D), row),
                  pl.BlockSpec((M, kv.shape[1]), row),
                  _resident(wq.shape), _resident(wo.shape), _resident((1, D)), _resident(pw.shape)],
        out_specs=[pl.BlockSpec((tm, D), row), pl.BlockSpec((tm, pw.shape[1]), row)],
        out_shape=[jax.ShapeDtypeStruct((T, D), f32), jax.ShapeDtypeStruct((T, pw.shape[1]), bf16)],
        compiler_params=_cparams("parallel"),
        name="xattn",
    )(h1, hn, kv, wq, wo, g, pw)


def _top_rows(s, k, payload=None):
    n = s.shape[0]
    rows = lax.broadcasted_iota(i32, s.shape, 0)
    vals, picks = [], []
    for _ in range(k):
        m = jnp.max(s, axis=0, keepdims=True)
        idx = jnp.min(jnp.where(s == m, rows, n), axis=0, keepdims=True)
        sel = rows == idx
        vals.append(m)
        picks.append(idx if payload is None else jnp.max(jnp.where(sel, payload, -1), axis=0, keepdims=True))
        s = jnp.where(sel, NEG, s)
    return vals, picks


def _peer_route_kernel(qp_ref, k1_ref, k2_ref, e_ref, g_ref, *, cfg):
    K, nk, half = cfg.peer_topk, cfg.peer_keys, cfg.peer_qdim // 2
    tm = qp_ref.shape[0]
    dn = (((1,), (1,)), ((), ()))
    s1 = lax.dot_general(k1_ref[0], qp_ref[:, :half], dn, preferred_element_type=f32)
    s2 = lax.dot_general(k2_ref[0], qp_ref[:, half:], dn, preferred_element_type=f32)
    t1, i1 = _top_rows(s1, K)
    t2, i2 = _top_rows(s2, K)
    t1, i1 = jnp.concatenate(t1, axis=0), jnp.concatenate(i1, axis=0)
    t2, i2 = jnp.concatenate(t2, axis=0), jnp.concatenate(i2, axis=0)
    cand = (t1[:, None, :] + t2[None, :, :]).reshape(K * K, tm)
    expert = (i1[:, None, :] * nk + i2[None, :, :]).reshape(K * K, tm)
    ts, es = _top_rows(cand, K, payload=expert)
    ts = jnp.concatenate(ts, axis=0)
    p = jnp.exp(ts - ts[0:1])
    g_ref[...] = p / jnp.sum(p, axis=0, keepdims=True)
    e_ref[...] = jnp.concatenate(es, axis=0)


def _peer_route(cfg, qp, k1, k2):
    T = qp.shape[0]
    tm, H, K, Q = cfg.route_tm, cfg.peer_heads, cfg.peer_topk, cfg.peer_qdim
    out = pl.BlockSpec((K, tm), lambda i, h: (h, i))
    return pl.pallas_call(
        functools.partial(_peer_route_kernel, cfg=cfg),
        grid=(T // tm, H),
        in_specs=[pl.BlockSpec((tm, Q), lambda i, h: (i, h)),
                  pl.BlockSpec((1,) + k1.shape[1:], lambda i, h: (h, 0, 0)),
                  pl.BlockSpec((1,) + k2.shape[1:], lambda i, h: (h, 0, 0))],
        out_specs=[out, out],
        out_shape=[jax.ShapeDtypeStruct((H * K, T), i32), jax.ShapeDtypeStruct((H * K, T), f32)],
        compiler_params=_cparams("parallel", "parallel"),
        name="peer_route",
    )(qp, k1, k2)


def _sc_gather(cfg, table, idx):
    n = idx.shape[0]
    d = table.shape[1]
    win, nb, nwr = cfg.gather_window, cfg.gather_bufs, cfg.gather_writes
    mesh = plsc.VectorSubcoreMesh(core_axis_name="core", subcore_axis_name="subcore")
    workers = mesh.num_cores * mesh.num_subcores
    per = n // workers
    steps = per // win
    assert per * workers == n and steps * win == per and steps >= nb > nwr

    @functools.partial(pl.kernel, out_type=jax.ShapeDtypeStruct((n, d), table.dtype), mesh=mesh,
                       scratch_types=[pltpu.VMEM((per,), i32)] + [pltpu.VMEM((win, d), table.dtype)] * nb
                       + [pltpu.SemaphoreType.DMA((nb,)), pltpu.SemaphoreType.DMA((nb,))])
    def gather_kernel(tab_hbm, idx_hbm, out_hbm, idx_vmem, *rest):
        bufs, gsem, wsem = rest[:nb], rest[nb], rest[nb + 1]
        wid = lax.axis_index("core") * mesh.num_subcores + lax.axis_index("subcore")
        base = wid * per
        pltpu.sync_copy(idx_hbm.at[pl.ds(base, per)], idx_vmem)

        def gather(k, b):
            return pltpu.make_async_copy(tab_hbm.at[idx_vmem.at[pl.ds(k * win, win)]], bufs[b], gsem.at[b])

        def write(k, b):
            return pltpu.make_async_copy(bufs[b], out_hbm.at[pl.ds(base + k * win, win)], wsem.at[b])

        for b in range(nb - nwr):
            gather(b, b).start()

        def step(k, b, when):
            pb = (b - nwr) % nb
            gather(k, b).wait()
            write(k, b).start()
            when(k >= nwr, lambda: write(k - nwr, pb).wait())
            when(k + nb - nwr < steps, lambda: gather(k + nb - nwr, pb).start())

        main = steps // nb * nb

        @pl.loop(0, main, step=nb)
        def _(k0):
            for b in range(nb):
                step(k0 + b, b, lambda cond, fn: pl.when(cond)(fn))

        for k in range(main, steps):
            step(k, k % nb, lambda cond, fn: fn() if cond else None)

        for j in range(nwr):
            k = steps - nwr + j
            write(k, k % nb).wait()

    return gather_kernel(table, idx)


def _pack_bf16_halves(a):
    h = a.shape[1] // 2
    b = lax.bitcast_convert_type(a.astype(bf16), jnp.uint16).astype(jnp.uint32)
    return b[:, :h] | (b[:, h:] << 16)


def _unpack_bf16_halves(words):
    return pltpu.bitcast(words << 16, f32), pltpu.bitcast(words & jnp.uint32(0xFFFF0000), f32)


def _split_bf16(a):
    hi = a.astype(bf16)
    lo = (a - hi.astype(f32)).astype(bf16)
    return jnp.concatenate([hi, lo], axis=0)


def _fold_rows(a):
    n = a.shape[0] // 2
    return a[:n] + a[n:]


def _peer_apply_kernel(h_ref, gt_ref, rows_ref, sel_ref, selt_ref, gn_ref, gf_ref, o_ref, *, cfg):
    E = cfg.slots
    R = SUBLANES
    D = h_ref.shape[1]
    nw = D // 2
    nz = 2 * R * E
    lane = lax.broadcasted_iota(i32, (2 * R, nz), 1)
    rowi = lax.broadcasted_iota(i32, (2 * R, nz), 0)
    mine = ((lane // 2) % R == rowi % R) & (lane % 2 == rowi // R)
    nt_dims = (((1,), (1,)), ((), ()))

    for gi in range(cfg.apply_groups):
        tok = slice(gi * R, (gi + 1) * R)
        zu = pltpu.bitcast(rows_ref[gi * E * R:(gi + 1) * E * R, :nw], bf16)
        zv = pltpu.bitcast(rows_ref[gi * E * R:(gi + 1) * E * R, nw:], bf16)
        h2 = h_ref[tok, :]
        x = _rms(h2, gn_ref[...])
        xs = _split_bf16(jnp.concatenate([x[:, :nw], x[:, nw:]], axis=0))
        a = _fold_rows(lax.dot_general(xs, zu, nt_dims, preferred_element_type=f32))
        a = jnp.where(mine, a, 0.0)
        a = a[:R] + a[R:]
        act = _fold_rows(jnp.dot(_split_bf16(a), sel_ref[...], preferred_element_type=f32))
        w = gt_ref[tok, :] * _gelu(act)
        w_rows = _fold_rows(jnp.dot(_split_bf16(w), selt_ref[...], preferred_element_type=f32))
        ws = _split_bf16(jnp.where(mine, jnp.concatenate([w_rows, w_rows], axis=0), 0.0))
        y2 = _fold_rows(jnp.dot(ws, zv, preferred_element_type=f32))
        y = jnp.concatenate([y2[:R], y2[R:]], axis=1)
        o_ref[tok, :] = _rms(h2 + y, gf_ref[...])


def _peer_apply(cfg, h2, gates, rows, gn, gf):
    T, D = h2.shape
    E, R, G = cfg.slots, SUBLANES, cfg.apply_groups
    assert E == LANES and rows.shape == (T * E, D)
    nz = 2 * R * E
    sel = (jnp.arange(nz)[:, None] // (2 * R) == jnp.arange(E)[None, :]).astype(bf16)
    row = lambda i: (i, 0)
    return pl.pallas_call(
        functools.partial(_peer_apply_kernel, cfg=cfg),
        grid=(T // (R * G),),
        in_specs=[pl.BlockSpec((R * G, D), row), pl.BlockSpec((R * G, E), row),
                  pl.BlockSpec((G * E * R, D), row), _resident((nz, E)), _resident((E, nz)),
                  _resident((1, D)), _resident((1, D))],
        out_specs=pl.BlockSpec((R * G, D), row),
        out_shape=jax.ShapeDtypeStruct((T, D), f32),
        compiler_params=_cparams("parallel"),
        name="peer_apply",
    )(h2, gates, rows, sel, sel.T, gn, gf)


def _block(cfg, gather_fn, x, mem, positions, mix_norm_g, w_in, lam_q1, lam_k1, lam_q2, lam_k2, attn_head_g,
           ssm_a_re, ssm_a_im, ssm_log_dt, ssm_b_re, ssm_b_im, ssm_c_re, ssm_c_im, ssm_d, glu_w, glu_b,
           ssm_out_g, w_out, xattn_norm_g, mem_norm_g, xattn_wq, xattn_wkv, xattn_wo, ffn_norm_g,
           peer_wq, peer_k1, peer_k2, peer_u, peer_v, final_norm_g):
    B, S, D = x.shape
    T = B * S
    l = 0
    row = lambda a: a.reshape(1, -1)
    x2 = x.reshape(T, D)
    pos = positions.reshape(T, 1).astype(f32)
    freqs = cfg.rope_theta ** (-jnp.arange(0, cfg.rot_dim, 2, dtype=f32) / cfg.rot_dim)
    lane = jnp.arange(LANES) % cfg.diff_qkdim
    freq_row = jnp.where(lane < cfg.rot_dim, freqs[lane % (cfg.rot_dim // 2)], 0.0).reshape(1, LANES)

    lam_p = jnp.stack([lam_q1[l], lam_k1[l], lam_q2[l], lam_k2[l]])
    s5_params = _s5_params(cfg, ssm_a_re[l], ssm_a_im[l], ssm_log_dt[l], ssm_b_re[l], ssm_b_im[l],
                           ssm_c_re[l], ssm_c_im[l], ssm_d[l])
    ns, _, wn = s5_params[0].shape
    zero_state = jnp.zeros((ns, wn // LANES, B, LANES), f32)
    state = (zero_state, zero_state)
    w_in_b, glu_w_b, w_out_b = w_in[l].astype(bf16), glu_w[l].astype(bf16), w_out[l].astype(bf16)
    wq_b, wo_b, pw_b = xattn_wq[l].astype(bf16), xattn_wo[l].astype(bf16), peer_wq[l].astype(bf16)
    k1_b, k2_b = peer_k1[l].astype(bf16), peer_k2[l].astype(bf16)
    kv = _mem_kv(cfg, mem.reshape(B * cfg.n_mem, D), row(mem_norm_g[l]), xattn_wkv[l].astype(bf16))
    table = jnp.concatenate([_pack_bf16_halves(peer_u[l]), _pack_bf16_halves(peer_v[l])], axis=1)
    E, R = cfg.slots, SUBLANES
    Lc = cfg.chunk
    Tc = B * Lc

    def after(a, tokens):
        return lax.optimization_barrier((a,) + tuple(tokens))[0] if tokens else a

    def dense(c, qks, vs, state, tokens):
        qk, v, u = _in_proj(cfg, c, x2, pos, after(row(mix_norm_g[l]), tokens), freq_row, w_in_b)
        qks, vs = qks + [qk], vs + [v]
        att = _diff_attn(cfg, lam_p, qks, vs, row(attn_head_g[l]))
        y, state = _s5_scan(cfg, u.reshape(B, Lc, cfg.ssm_width), s5_params, state)
        ssm = _s5_glu(cfg, y.reshape(Tc, cfg.ssm_width), glu_w_b, row(glu_b[l]), row(ssm_out_g[l]))
        h1, hn = _out_proj(cfg, c, x2, att, ssm, w_out_b, row(xattn_norm_g[l]))
        h2, qp = _xattn(cfg, h1, hn, kv, wq_b, wo_b, row(ffn_norm_g[l]), pw_b)
        experts_t, gates_t = _peer_route(cfg, qp, k1_b, k2_b)
        idx = experts_t.reshape(E, Tc // R, R).transpose(1, 0, 2).reshape(-1)
        return qks, vs, state, experts_t, (h2, gates_t.T, gather_fn(table, idx))

    def apply(item, tokens):
        h2, gates, rows = item
        return _peer_apply(cfg, h2, gates, rows, after(row(ffn_norm_g[l]), tokens),
                           row(final_norm_g)).reshape(B, Lc, D)

    lead = 2
    nc = cfg.n_chunks
    qks, vs, experts, items, outs = [], [], [], [], []
    for c in range(nc):
        tokens = experts[c - 1:c] + (outs[c - lead - 1:c - lead] if c > lead else [])
        qks, vs, state, e, item = dense(c, qks, vs, state, tokens)
        experts.append(e)
        items.append(item)
        if c >= lead:
            outs.append(apply(items[c - lead], [e]))
    for c in range(max(nc - lead, 0), nc):
        outs.append(apply(items[c], []))
    return jnp.concatenate(outs, axis=1)


def kernel(x, mem, positions, mix_norm_g, w_in, lam_q1, lam_k1, lam_q2, lam_k2, attn_head_g, ssm_a_re, ssm_a_im, ssm_log_dt, ssm_b_re, ssm_b_im, ssm_c_re, ssm_c_im, ssm_d, glu_w, glu_b, ssm_out_g, w_out, xattn_norm_g, mem_norm_g, xattn_wq, xattn_wkv, xattn_wo, ffn_norm_g, peer_wq, peer_k1, peer_k2, peer_u, peer_v, final_norm_g):
    cfg = Cfg()
    return _block(cfg, functools.partial(_sc_gather, cfg), x, mem, positions, mix_norm_g, w_in, lam_q1, lam_k1,
                  lam_q2, lam_k2, attn_head_g, ssm_a_re, ssm_a_im, ssm_log_dt, ssm_b_re, ssm_b_im, ssm_c_re,
                  ssm_c_im, ssm_d, glu_w, glu_b, ssm_out_g, w_out, xattn_norm_g, mem_norm_g, xattn_wq,
                  xattn_wkv, xattn_wo, ffn_norm_g, peer_wq, peer_k1, peer_k2, peer_u, peer_v, final_norm_g)
```

```python
import dataclasses
import functools
import math

import jax
import jax.numpy as jnp
from jax import lax
from jax.experimental import pallas as pl
from jax.experimental.pallas import tpu as pltpu
from jax.experimental.pallas import tpu_sc as plsc

f32 = jnp.float32
bf16 = jnp.bfloat16
i32 = jnp.int32

LANES = 128
SUBLANES = 8
VMEM_LIMIT = 56 * 1024 * 1024
NEG = -1e30
EPS = 1e-6


@dataclasses.dataclass(frozen=True)
class Cfg:
    d_model: int = 2048
    batch: int = 8
    seq: int = 2048
    n_mem: int = 256
    diff_heads: int = 8
    ssm_group: int = 16
    ssm_state: int = 64
    xattn_heads: int = 4
    xattn_head_dim: int = 128
    peer_heads: int = 8
    peer_keys: int = 128
    peer_qdim: int = 256
    peer_topk: int = 16
    rope_theta: float = 500000.0
    lam_init: float = 0.8 - 0.6 * math.exp(-0.3 * 0)
    chunk: int = 256
    scan_chunk: int = 128
    route_tm: int = 256
    gather_window: int = 8
    gather_bufs: int = 7
    gather_writes: int = 2
    apply_groups: int = 2
    direct_eighths: int = 8

    @property
    def n_chunks(self):
        return self.seq // self.chunk

    @property
    def attn_width(self):
        return self.d_model // 2

    @property
    def ssm_width(self):
        return self.d_model - self.attn_width

    @property
    def diff_vdim(self):
        return self.attn_width // self.diff_heads

    @property
    def diff_qkdim(self):
        return self.diff_vdim // 2

    @property
    def rot_dim(self):
        return self.diff_qkdim // 4

    @property
    def ssm_groups(self):
        return self.ssm_width // self.ssm_group

    @property
    def xattn_width(self):
        return self.xattn_heads * self.xattn_head_dim

    @property
    def tokens(self):
        return self.batch * self.seq

    @property
    def slots(self):
        return self.peer_heads * self.peer_topk


def _cparams(*sem):
    return pltpu.CompilerParams(dimension_semantics=sem, vmem_limit_bytes=VMEM_LIMIT)


def _resident(shape):
    nd = len(shape)
    return pl.BlockSpec(shape, lambda *_: (0,) * nd, pipeline_mode=pl.Buffered(1))


def _rms(x, g):
    return x * lax.rsqrt(jnp.mean(x * x, axis=-1, keepdims=True) + EPS) * g


def _gelu(x):
    return 0.5 * x * (1.0 + lax.erf(x * (2.0 ** -0.5)))


def _in_proj_kernel(x_ref, pos_ref, g_ref, freq_ref, w_ref, qk_ref, v_ref, u_ref, *, cfg):
    n_qk, n_v = 2 * cfg.attn_width, cfg.attn_width
    half = cfg.rot_dim // 2
    xn = _rms(x_ref[...], g_ref[...]).astype(bf16)
    ang = pos_ref[...] * freq_ref[...]
    cos, sin = jnp.cos(ang), jnp.sin(ang)
    lane = lax.broadcasted_iota(i32, (1, LANES), 1) % cfg.diff_qkdim
    sin_lo = jnp.where(lane < half, -sin, 0.0)
    sin_hi = jnp.where((lane >= half) & (lane < 2 * half), sin, 0.0)
    cw = 2 * LANES
    for c in range((n_qk + n_v + cfg.ssm_width) // cw):
        col = c * cw
        z = jnp.dot(xn, w_ref[:, col:col + cw], preferred_element_type=f32)
        if col < n_qk:
            for k in range(cw // LANES):
                zk = z[:, k * LANES:(k + 1) * LANES]
                zk = zk * cos + pltpu.roll(zk, LANES - half, 1) * sin_lo + pltpu.roll(zk, half, 1) * sin_hi
                qk_ref[:, col + k * LANES:col + (k + 1) * LANES] = zk.astype(bf16)
        elif col < n_qk + n_v:
            v_ref[:, col - n_qk:col - n_qk + cw] = z.astype(bf16)
        else:
            u_ref[:, col - n_qk - n_v:col - n_qk - n_v + cw] = z


def _chunk_rows(cfg, c):
    return lambda b: (b * cfg.n_chunks + c, 0)


def _in_proj(cfg, c, x2, pos, g, freq, w):
    D = x2.shape[1]
    B, tm = cfg.batch, cfg.chunk
    n_qk, n_v, n_u = 2 * cfg.attn_width, cfg.attn_width, cfg.ssm_width
    row = lambda b: (b, 0)
    return pl.pallas_call(
        functools.partial(_in_proj_kernel, cfg=cfg),
        grid=(B,),
        in_specs=[pl.BlockSpec((tm, D), _chunk_rows(cfg, c)), pl.BlockSpec((tm, 1), _chunk_rows(cfg, c)),
                  _resident((1, D)), _resident((1, LANES)), _resident(w.shape)],
        out_specs=[pl.BlockSpec((tm, n_qk), row), pl.BlockSpec((tm, n_v), row), pl.BlockSpec((tm, n_u), row)],
        out_shape=[jax.ShapeDtypeStruct((B * tm, n_qk), bf16), jax.ShapeDtypeStruct((B * tm, n_v), bf16),
                   jax.ShapeDtypeStruct((B * tm, n_u), f32)],
        compiler_params=_cparams("parallel"),
        name="in_proj",
    )(x2, pos, g, freq, w)


def _diff_attn_kernel(lam_ref, q_ref, *rest, cfg, n_kv):
    k_refs, v_refs = rest[:n_kv], rest[n_kv:2 * n_kv]
    g_ref, o_ref, m_ref, l_ref, acc_ref = rest[2 * n_kv:]
    tq = cfg.chunk
    d = cfg.diff_qkdim
    lane = lax.broadcasted_iota(i32, (1, LANES), 1)
    q = q_ref[...].astype(f32) * (d ** -0.5)
    qs = (jnp.where(lane < d, q, 0.0).astype(bf16), jnp.where(lane >= d, q, 0.0).astype(bf16))
    m_ref[...] = jnp.full(m_ref.shape, NEG, f32)
    l_ref[...] = jnp.zeros(l_ref.shape, f32)
    acc_ref[...] = jnp.zeros(acc_ref.shape, f32)
    causal = (lax.broadcasted_iota(i32, (tq, tq), 1) <= lax.broadcasted_iota(i32, (tq, tq), 0))

    def step(j, masked):
        kb = k_refs[j][...]
        vb = v_refs[j][...]
        for c in range(2):
            s = lax.dot_general(qs[c], kb, (((1,), (1,)), ((), ())), preferred_element_type=f32)
            if masked:
                s = jnp.where(causal, s, NEG)
            m_old = m_ref[c]
            m_new = jnp.maximum(m_old, jnp.max(s, axis=1, keepdims=True))
            alpha = jnp.exp(m_old - m_new)
            p = jnp.exp(s - m_new)
            l_ref[c] = alpha * l_ref[c] + jnp.sum(p, axis=1, keepdims=True)
            acc_ref[c] = alpha * acc_ref[c] + jnp.dot(p.astype(bf16), vb, preferred_element_type=f32)
            m_ref[c] = m_new

    for j in range(n_kv - 1):
        step(j, False)
    step(n_kv - 1, True)

    lp = lam_ref[...]
    lam = (jnp.exp(jnp.sum(lp[0:1] * lp[1:2], axis=1, keepdims=True))
           - jnp.exp(jnp.sum(lp[2:3] * lp[3:4], axis=1, keepdims=True)) + cfg.lam_init)
    o = acc_ref[0] / l_ref[0] - lam * (acc_ref[1] / l_ref[1])
    o_ref[...] = (_rms(o, g_ref[...]) * (1.0 - cfg.lam_init)).astype(o_ref.dtype)


def _diff_attn(cfg, lam_p, qks, vs, g):
    n_kv = len(qks)
    B, H, tq = cfg.batch, cfg.diff_heads, cfg.chunk
    head = lambda b, h: (b, h)
    key = lambda b, h: (b, H + h)
    return pl.pallas_call(
        functools.partial(_diff_attn_kernel, cfg=cfg, n_kv=n_kv),
        grid=(B, H),
        in_specs=([_resident(lam_p.shape), pl.BlockSpec((tq, LANES), head)]
                  + [pl.BlockSpec((tq, LANES), key)] * n_kv + [pl.BlockSpec((tq, LANES), head)] * n_kv
                  + [_resident((1, LANES))]),
        out_specs=pl.BlockSpec((tq, LANES), head),
        out_shape=jax.ShapeDtypeStruct((B * tq, cfg.attn_width), bf16),
        scratch_shapes=[pltpu.VMEM((2, tq, 1), f32), pltpu.VMEM((2, tq, 1), f32),
                        pltpu.VMEM((2, tq, LANES), f32)],
        compiler_params=_cparams("parallel", "parallel"),
        name="diff_attn",
    )(lam_p, qks[-1], *qks, *vs, g)


def _s5_scan_kernel(u_ref, are_ref, aim_ref, ldt_ref, bre_ref, bim_ref, cre_ref, cim_ref, d_ref,
                    sin_re_ref, sin_im_ref, y_ref, sout_re_ref, sout_im_ref,
                    ab_ref, bbar_re_ref, bbar_im_ref, bu_re_ref, bu_im_ref, *, cfg):
    B, L = cfg.batch, cfg.scan_chunk
    nt = bu_re_ref.shape[0]
    t = pl.program_id(1)
    st_re_ref, st_im_ref = sout_re_ref.at[0], sout_im_ref.at[0]

    @pl.when(t == 0)
    def _():
        dt = jnp.exp(ldt_ref[0])
        lre, lim = are_ref[0], aim_ref[0]
        mag = jnp.exp(lre * dt)
        ang = lim * dt
        ab_re, ab_im = mag * jnp.cos(ang), mag * jnp.sin(ang)
        den = lre * lre + lim * lim
        f_re = ((ab_re - 1.0) * lre + ab_im * lim) / den
        f_im = (ab_im * lre - (ab_re - 1.0) * lim) / den
        ab_ref[0] = jnp.broadcast_to(ab_re, ab_ref.shape[1:])
        ab_ref[1] = jnp.broadcast_to(ab_im, ab_ref.shape[1:])
        br, bi = bre_ref[0], bim_ref[0]
        bbar_re_ref[...] = (f_re * br - f_im * bi).astype(bf16)
        bbar_im_ref[...] = (f_re * bi + f_im * br).astype(bf16)
        st_re_ref[...] = sin_re_ref[0]
        st_im_ref[...] = sin_im_ref[0]

    u = u_ref[...].reshape(B * L, u_ref.shape[2])
    ub = u.astype(bf16)
    bu_re = jnp.dot(ub, bbar_re_ref[...], preferred_element_type=f32)
    bu_im = jnp.dot(ub, bbar_im_ref[...], preferred_element_type=f32)
    for j in range(nt):
        bu_re_ref[j] = bu_re[:, j * LANES:(j + 1) * LANES]
        bu_im_ref[j] = bu_im[:, j * LANES:(j + 1) * LANES]

    a_re = [ab_ref[0, :, j * LANES:(j + 1) * LANES] for j in range(nt)]
    a_im = [ab_ref[1, :, j * LANES:(j + 1) * LANES] for j in range(nt)]

    def body(s, carry):
        xr, xi = carry
        nr, ni = [], []
        for j in range(nt):
            rows = pl.ds(s, B, stride=L)
            r = a_re[j] * xr[j] - a_im[j] * xi[j] + bu_re_ref[j, rows, :]
            m = a_re[j] * xi[j] + a_im[j] * xr[j] + bu_im_ref[j, rows, :]
            bu_re_ref[j, rows, :] = r
            bu_im_ref[j, rows, :] = m
            nr.append(r)
            ni.append(m)
        return tuple(nr), tuple(ni)

    init = (tuple(st_re_ref[j] for j in range(nt)), tuple(st_im_ref[j] for j in range(nt)))
    xr, xi = lax.fori_loop(0, L, body, init, unroll=8)
    for j in range(nt):
        st_re_ref[j] = xr[j]
        st_im_ref[j] = xi[j]

    xs_re = jnp.concatenate([bu_re_ref[j] for j in range(nt)], axis=1).astype(bf16)
    xs_im = jnp.concatenate([bu_im_ref[j] for j in range(nt)], axis=1).astype(bf16)
    y = (jnp.dot(xs_re, cre_ref[0].astype(bf16), preferred_element_type=f32)
         - jnp.dot(xs_im, cim_ref[0].astype(bf16), preferred_element_type=f32)
         + d_ref[0] * u)
    y_ref[...] = y.reshape(y_ref.shape)


def _block_diag(w, blocks):
    G, r, c = w.shape
    w4 = w.reshape(G // blocks, blocks, r, c)
    eye = jnp.eye(blocks, dtype=w.dtype)
    out = w4[:, :, :, None, :] * eye[None, :, None, :, None]
    return out.reshape(G // blocks, blocks * r, blocks * c)


def _s5_params(cfg, a_re, a_im, log_dt, b_re, b_im, c_re, c_im, d_skip):
    G, P, Hc = cfg.ssm_groups, cfg.ssm_state, cfg.ssm_group
    gps = min(G, 2 * LANES // Hc)
    ns = G // gps
    wu, wn = gps * Hc, gps * P
    return (a_re.reshape(ns, 1, wn), a_im.reshape(ns, 1, wn), jnp.repeat(log_dt, P).reshape(ns, 1, wn),
            _block_diag(b_re.transpose(0, 2, 1), gps), _block_diag(b_im.transpose(0, 2, 1), gps),
            _block_diag(c_re.transpose(0, 2, 1), gps), _block_diag(c_im.transpose(0, 2, 1), gps),
            d_skip.reshape(ns, 1, wu))


def _s5_scan(cfg, u3, params, state):
    B, steps, W = u3.shape
    L = cfg.scan_chunk
    ns, _, wn = params[0].shape
    wu = params[-1].shape[2]
    nt = wn // LANES
    slab = lambda s, t: (s, 0, 0)
    st = pl.BlockSpec((1, nt, B, LANES), lambda s, t: (s, 0, 0, 0))
    st_shape = jax.ShapeDtypeStruct((ns, nt, B, LANES), f32)
    y, s_re, s_im = pl.pallas_call(
        functools.partial(_s5_scan_kernel, cfg=cfg),
        grid=(ns, steps // L),
        in_specs=[pl.BlockSpec((B, L, wu), lambda s, t: (0, t, s)),
                  pl.BlockSpec((1, 1, wn), slab), pl.BlockSpec((1, 1, wn), slab), pl.BlockSpec((1, 1, wn), slab),
                  pl.BlockSpec((1, wu, wn), slab), pl.BlockSpec((1, wu, wn), slab),
                  pl.BlockSpec((1, wn, wu), slab), pl.BlockSpec((1, wn, wu), slab),
                  pl.BlockSpec((1, 1, wu), slab), st, st],
        out_specs=[pl.BlockSpec((B, L, wu), lambda s, t: (0, t, s)), st, st],
        out_shape=[jax.ShapeDtypeStruct((B, steps, W), f32), st_shape, st_shape],
        scratch_shapes=[pltpu.VMEM((2, SUBLANES, wn), f32),
                        pltpu.VMEM((wu, wn), bf16), pltpu.VMEM((wu, wn), bf16),
                        pltpu.VMEM((nt, B * L, LANES), f32), pltpu.VMEM((nt, B * L, LANES), f32)],
        compiler_params=_cparams("arbitrary", "arbitrary"),
        name="s5_scan",
    )(u3, *params, *state)
    return y, (s_re, s_im)


def _s5_glu_kernel(y_ref, w_ref, b_ref, g_ref, o_ref):
    g = _gelu(y_ref[...])
    z = jnp.dot(g.astype(bf16), w_ref[...], preferred_element_type=f32) + b_ref[...]
    g = g * jax.nn.sigmoid(z)
    o_ref[...] = _rms(g, g_ref[...]).astype(o_ref.dtype)


def _s5_glu(cfg, y2, w, b, g):
    T, W = y2.shape
    tm = cfg.chunk
    row = lambda i: (i, 0)
    return pl.pallas_call(
        _s5_glu_kernel,
        grid=(T // tm,),
        in_specs=[pl.BlockSpec((tm, W), row), _resident(w.shape), _resident((1, W)), _resident((1, W))],
        out_specs=pl.BlockSpec((tm, W), row),
        out_shape=jax.ShapeDtypeStruct((T, W), bf16),
        compiler_params=_cparams("parallel"),
        name="s5_glu",
    )(y2, w, b, g)


def _out_proj_kernel(x_ref, att_ref, ssm_ref, w_ref, g_ref, h_ref, hn_ref, *, cfg):
    aw = cfg.attn_width
    h = (x_ref[...]
         + jnp.dot(att_ref[...], w_ref[:aw, :], preferred_element_type=f32)
         + jnp.dot(ssm_ref[...], w_ref[aw:, :], preferred_element_type=f32))
    h_ref[...] = h
    hn_ref[...] = _rms(h, g_ref[...]).astype(hn_ref.dtype)


def _out_proj(cfg, c, x2, att, ssm, w, g):
    D = x2.shape[1]
    tm = cfg.chunk
    T = cfg.batch * tm
    row = lambda i: (i, 0)
    return pl.pallas_call(
        functools.partial(_out_proj_kernel, cfg=cfg),
        grid=(T // tm,),
        in_specs=[pl.BlockSpec((tm, D), _chunk_rows(cfg, c)), pl.BlockSpec((tm, cfg.attn_width), row),
                  pl.BlockSpec((tm, cfg.ssm_width), row), _resident(w.shape), _resident((1, D))],
        out_specs=[pl.BlockSpec((tm, D), row), pl.BlockSpec((tm, D), row)],
        out_shape=[jax.ShapeDtypeStruct((T, D), f32), jax.ShapeDtypeStruct((T, D), bf16)],
        compiler_params=_cparams("parallel"),
        name="out_proj",
    )(x2, att, ssm, w, g)


def _mem_kv_kernel(m_ref, g_ref, w_ref, o_ref):
    mn = _rms(m_ref[...], g_ref[...]).astype(bf16)
    o_ref[...] = jnp.dot(mn, w_ref[...], preferred_element_type=f32).astype(o_ref.dtype)


def _mem_kv(cfg, mem2, g, w):
    R, D = mem2.shape
    tm = cfg.n_mem
    row = lambda i: (i, 0)
    return pl.pallas_call(
        _mem_kv_kernel,
        grid=(R // tm,),
        in_specs=[pl.BlockSpec((tm, D), row), _resident((1, D)), _resident(w.shape)],
        out_specs=pl.BlockSpec((tm, w.shape[1]), row),
        out_shape=jax.ShapeDtypeStruct((R, w.shape[1]), bf16),
        compiler_params=_cparams("parallel"),
        name="mem_kv",
    )(mem2, g, w)


def _xattn_kernel(h_ref, hn_ref, kv_ref, wq_ref, wo_ref, g_ref, pw_ref, h2_ref, qp_ref, *, cfg):
    nh, hd, xw = cfg.xattn_heads, cfg.xattn_head_dim, cfg.xattn_width
    q = jnp.dot(hn_ref[...], wq_ref[...], preferred_element_type=f32).astype(bf16)
    outs = []
    for h in range(nh):
        qh = q[:, h * hd:(h + 1) * hd]
        kh = kv_ref[:, h * hd:(h + 1) * hd]
        vh = kv_ref[:, xw + h * hd:xw + (h + 1) * hd]
        s = lax.dot_general(qh, kh, (((1,), (1,)), ((), ())), preferred_element_type=f32) * (hd ** -0.5)
        p = jnp.exp(s - jnp.max(s, axis=1, keepdims=True))
        p = p / jnp.sum(p, axis=1, keepdims=True)
        outs.append(jnp.dot(p.astype(bf16), vh, preferred_element_type=f32).astype(bf16))
    o = jnp.concatenate(outs, axis=1)
    h2 = h_ref[...] + jnp.dot(o, wo_ref[...], preferred_element_type=f32)
    h2_ref[...] = h2
    hp = _rms(h2, g_ref[...]).astype(bf16)
    qp_ref[...] = jnp.dot(hp, pw_ref[...], preferred_element_type=f32).astype(qp_ref.dtype)


def _xattn(cfg, h1, hn, kv, wq, wo, g, pw):
    T, D = h1.shape
    tm, M = cfg.chunk, cfg.n_mem
    row = lambda i: (i, 0)
    return pl.pallas_call(
        functools.partial(_xattn_kernel, cfg=cfg),
        grid=(T // tm,),
        in_specs=[pl.BlockSpec((tm, D), row), pl.BlockSpec((tm, D), row),
                  pl.BlockSpec((M, kv.shape[1]), row),
                  _resident(wq.shape), _resident(wo.shape), _resident((1, D)), _resident(pw.shape)],
        out_specs=[pl.BlockSpec((tm, D), row), pl.BlockSpec((tm, pw.shape[1]), row)],
        out_shape=[jax.ShapeDtypeStruct((T, D), f32), jax.ShapeDtypeStruct((T, pw.shape[1]), bf16)],
        compiler_params=_cparams("parallel"),
        name="xattn",
    )(h1, hn, kv, wq, wo, g, pw)


def _top_rows(s, k, payload=None):
    n = s.shape[0]
    rows = lax.broadcasted_iota(i32, s.shape, 0)
    vals, picks = [], []
    for _ in range(k):
        m = jnp.max(s, axis=0, keepdims=True)
        idx = jnp.min(jnp.where(s == m, rows, n), axis=0, keepdims=True)
        sel = rows == idx
        vals.append(m)
        picks.append(idx if payload is None else jnp.max(jnp.where(sel, payload, -1), axis=0, keepdims=True))
        s = jnp.where(sel, NEG, s)
    return vals, picks


def _peer_route_kernel(qp_ref, k1_ref, k2_ref, e_ref, g_ref, *, cfg):
    K, nk, half = cfg.peer_topk, cfg.peer_keys, cfg.peer_qdim // 2
    tm = qp_ref.shape[0]
    dn = (((1,), (1,)), ((), ()))
    s1 = lax.dot_general(k1_ref[0], qp_ref[:, :half], dn, preferred_element_type=f32)
    s2 = lax.dot_general(k2_ref[0], qp_ref[:, half:], dn, preferred_element_type=f32)
    t1, i1 = _top_rows(s1, K)
    t2, i2 = _top_rows(s2, K)
    t1, i1 = jnp.concatenate(t1, axis=0), jnp.concatenate(i1, axis=0)
    t2, i2 = jnp.concatenate(t2, axis=0), jnp.concatenate(i2, axis=0)
    cand = (t1[:, None, :] + t2[None, :, :]).reshape(K * K, tm)
    expert = (i1[:, None, :] * nk + i2[None, :, :]).reshape(K * K, tm)
    ts, es = _top_rows(cand, K, payload=expert)
    ts = jnp.concatenate(ts, axis=0)
    p = jnp.exp(ts - ts[0:1])
    g_ref[...] = p / jnp.sum(p, axis=0, keepdims=True)
    e_ref[...] = jnp.concatenate(es, axis=0)


def _peer_route(cfg, qp, k1, k2):
    T = qp.shape[0]
    tm, H, K, Q = cfg.route_tm, cfg.peer_heads, cfg.peer_topk, cfg.peer_qdim
    out = pl.BlockSpec((K, tm), lambda i, h: (h, i))
    return pl.pallas_call(
        functools.partial(_peer_route_kernel, cfg=cfg),
        grid=(T // tm, H),
        in_specs=[pl.BlockSpec((tm, Q), lambda i, h: (i, h)),
                  pl.BlockSpec((1,) + k1.shape[1:], lambda i, h: (h, 0, 0)),
                  pl.BlockSpec((1,) + k2.shape[1:], lambda i, h: (h, 0, 0))],
        out_specs=[out, out],
        out_shape=[jax.ShapeDtypeStruct((H * K, T), i32), jax.ShapeDtypeStruct((H * K, T), f32)],
        compiler_params=_cparams("parallel", "parallel"),
        name="peer_route",
    )(qp, k1, k2)


def _sc_gather(cfg, table, idx):
    n = idx.shape[0]
    d = table.shape[1]
    win, nb, nwr = cfg.gather_window, cfg.gather_bufs, cfg.gather_writes
    mesh = plsc.VectorSubcoreMesh(core_axis_name="core", subcore_axis_name="subcore")
    workers = mesh.num_cores * mesh.num_subcores
    per = n // workers
    steps = per // win
    assert per * workers == n and steps * win == per and steps >= nb > nwr

    @functools.partial(pl.kernel, out_type=jax.ShapeDtypeStruct((n, d), table.dtype), mesh=mesh,
                       scratch_types=[pltpu.VMEM((per,), i32)] + [pltpu.VMEM((win, d), table.dtype)] * nb
                       + [pltpu.SemaphoreType.DMA((nb,)), pltpu.SemaphoreType.DMA((nb,))])
    def gather_kernel(tab_hbm, idx_hbm, out_hbm, idx_vmem, *rest):
        bufs, gsem, wsem = rest[:nb], rest[nb], rest[nb + 1]
        wid = lax.axis_index("core") * mesh.num_subcores + lax.axis_index("subcore")
        base = wid * per
        pltpu.sync_copy(idx_hbm.at[pl.ds(base, per)], idx_vmem)

        def gather(k, b):
            return pltpu.make_async_copy(tab_hbm.at[idx_vmem.at[pl.ds(k * win, win)]], bufs[b], gsem.at[b])

        def write(k, b):
            return pltpu.make_async_copy(bufs[b], out_hbm.at[pl.ds(base + k * win, win)], wsem.at[b])

        for b in range(nb - nwr):
            gather(b, b).start()

        def step(k, b, when):
            pb = (b - nwr) % nb
            gather(k, b).wait()
            write(k, b).start()
            when(k >= nwr, lambda: write(k - nwr, pb).wait())
            when(k + nb - nwr < steps, lambda: gather(k + nb - nwr, pb).start())

        main = steps // nb * nb

        @pl.loop(0, main, step=nb)
        def _(k0):
            for b in range(nb):
                step(k0 + b, b, lambda cond, fn: pl.when(cond)(fn))

        for k in range(main, steps):
            step(k, k % nb, lambda cond, fn: fn() if cond else None)

        for j in range(nwr):
            k = steps - nwr + j
            write(k, k % nb).wait()

    return gather_kernel(table, idx)


def _pack_bf16_halves(a):
    h = a.shape[1] // 2
    b = lax.bitcast_convert_type(a.astype(bf16), jnp.uint16).astype(jnp.uint32)
    return b[:, :h] | (b[:, h:] << 16)


def _unpack_bf16_halves(words):
    return pltpu.bitcast(words << 16, f32), pltpu.bitcast(words & jnp.uint32(0xFFFF0000), f32)


def _split_bf16(a):
    hi = a.astype(bf16)
    lo = (a - hi.astype(f32)).astype(bf16)
    return jnp.concatenate([hi, lo], axis=0)


def _fold_rows(a):
    n = a.shape[0] // 2
    return a[:n] + a[n:]


def _apply_group(h2, gates, words, sel, selt, gn, gf):
    R, D = h2.shape
    nw = D // 2
    nz = 2 * words.shape[0]
    lane = lax.broadcasted_iota(i32, (2 * R, nz), 1)
    rowi = lax.broadcasted_iota(i32, (2 * R, nz), 0)
    mine = ((lane // 2) % R == rowi % R) & (lane % 2 == rowi // R)
    zu = pltpu.bitcast(words[:, :nw], bf16)
    zv = pltpu.bitcast(words[:, nw:], bf16)
    x = _rms(h2, gn)
    xs = _split_bf16(jnp.concatenate([x[:, :nw], x[:, nw:]], axis=0))
    a = _fold_rows(lax.dot_general(xs, zu, (((1,), (1,)), ((), ())), preferred_element_type=f32))
    a = jnp.where(mine, a, 0.0)
    a = a[:R] + a[R:]
    act = _fold_rows(jnp.dot(_split_bf16(a), sel, preferred_element_type=f32))
    w = gates * _gelu(act)
    w_rows = _fold_rows(jnp.dot(_split_bf16(w), selt, preferred_element_type=f32))
    ws = _split_bf16(jnp.where(mine, jnp.concatenate([w_rows, w_rows], axis=0), 0.0))
    y2 = _fold_rows(jnp.dot(ws, zv, preferred_element_type=f32))
    y = jnp.concatenate([y2[:R], y2[R:]], axis=1)
    return _rms(h2 + y, gf)


def _peer_apply_kernel(h_ref, gt_ref, rows_ref, sel_ref, selt_ref, gn_ref, gf_ref, o_ref, *, cfg):
    E, R = cfg.slots, SUBLANES
    for gi in range(cfg.apply_groups):
        tok = slice(gi * R, (gi + 1) * R)
        o_ref[tok, :] = _apply_group(h_ref[tok, :], gt_ref[tok, :], rows_ref[gi * E * R:(gi + 1) * E * R, :],
                                     sel_ref[...], selt_ref[...], gn_ref[...], gf_ref[...])


def _peer_direct_kernel(idx_ref, idxn_ref, h_ref, gt_ref, tab_ref, sel_ref, selt_ref, gn_ref, gf_ref, o_ref,
                        buf_a, buf_b, sem_ref, *, cfg):
    E, R = cfg.slots, SUBLANES
    n_rows = E * R
    i = pl.program_id(0)

    def row_copy(src_idx_ref, base, r, buf, s):
        return pltpu.make_async_copy(tab_ref.at[pl.ds(src_idx_ref[base + r], 1)], buf.at[pl.ds(r, 1)],
                                     sem_ref.at[s])

    def wait_rows(buf, s):
        pltpu.make_async_copy(tab_ref.at[pl.ds(0, n_rows)], buf, sem_ref.at[s]).wait()

    def group(gi, buf):
        tok = slice(gi * R, (gi + 1) * R)
        o_ref[tok, :] = _apply_group(h_ref[tok, :], gt_ref[tok, :], buf[...], sel_ref[...], selt_ref[...],
                                     gn_ref[...], gf_ref[...])

    @pl.when(i == 0)
    def _():
        def body(r, carry):
            row_copy(idx_ref, 0, r, buf_a, 0).start()
            return carry
        lax.fori_loop(0, n_rows, body, 0, unroll=8)

    wait_rows(buf_a, 0)
    for r in range(n_rows):
        row_copy(idx_ref, n_rows, r, buf_b, 1).start()
    group(0, buf_a)
    wait_rows(buf_b, 1)
    for r in range(n_rows):
        row_copy(idxn_ref, 0, r, buf_a, 0).start()
    group(1, buf_b)

    @pl.when(i == pl.num_programs(0) - 1)
    def _():
        wait_rows(buf_a, 0)


def _peer_direct(cfg, h2, gates, idx, table, gn, gf):
    T, D = h2.shape
    E, R = cfg.slots, SUBLANES
    n_rows = E * R
    nz = 2 * n_rows
    n = T // (2 * R)
    sel = (jnp.arange(nz)[:, None] // (2 * R) == jnp.arange(E)[None, :]).astype(bf16)
    row = lambda i: (i, 0)
    smem = functools.partial(pl.BlockSpec, (2 * n_rows,), memory_space=pltpu.SMEM)
    return pl.pallas_call(
        functools.partial(_peer_direct_kernel, cfg=cfg),
        grid=(n,),
        in_specs=[smem(lambda i: (i,)), smem(lambda i: (jnp.minimum(i + 1, n - 1),)),
                  pl.BlockSpec((2 * R, D), row), pl.BlockSpec((2 * R, E), row), pl.BlockSpec(memory_space=pl.ANY),
                  _resident((nz, E)), _resident((E, nz)), _resident((1, D)), _resident((1, D))],
        out_specs=pl.BlockSpec((2 * R, D), row),
        out_shape=jax.ShapeDtypeStruct((T, D), f32),
        scratch_shapes=[pltpu.VMEM((n_rows, D), table.dtype), pltpu.VMEM((n_rows, D), table.dtype),
                        pltpu.SemaphoreType.DMA((2,))],
        compiler_params=pltpu.CompilerParams(dimension_semantics=("arbitrary",), vmem_limit_bytes=VMEM_LIMIT,
                                             disable_bounds_checks=True),
        name="peer_direct",
    )(idx, idx, h2, gates, table, sel, sel.T, gn, gf)


def _peer_apply(cfg, h2, gates, rows, gn, gf):
    T, D = h2.shape
    E, R, G = cfg.slots, SUBLANES, cfg.apply_groups
    assert E == LANES and rows.shape == (T * E, D)
    nz = 2 * R * E
    sel = (jnp.arange(nz)[:, None] // (2 * R) == jnp.arange(E)[None, :]).astype(bf16)
    row = lambda i: (i, 0)
    return pl.pallas_call(
        functools.partial(_peer_apply_kernel, cfg=cfg),
        grid=(T // (R * G),),
        in_specs=[pl.BlockSpec((R * G, D), row), pl.BlockSpec((R * G, E), row),
                  pl.BlockSpec((G * E * R, D), row), _resident((nz, E)), _resident((E, nz)),
                  _resident((1, D)), _resident((1, D))],
        out_specs=pl.BlockSpec((R * G, D), row),
        out_shape=jax.ShapeDtypeStruct((T, D), f32),
        compiler_params=_cparams("parallel"),
        name="peer_apply",
    )(h2, gates, rows, sel, sel.T, gn, gf)


def _block(cfg, gather_fn, x, mem, positions, mix_norm_g, w_in, lam_q1, lam_k1, lam_q2, lam_k2, attn_head_g,
           ssm_a_re, ssm_a_im, ssm_log_dt, ssm_b_re, ssm_b_im, ssm_c_re, ssm_c_im, ssm_d, glu_w, glu_b,
           ssm_out_g, w_out, xattn_norm_g, mem_norm_g, xattn_wq, xattn_wkv, xattn_wo, ffn_norm_g,
           peer_wq, peer_k1, peer_k2, peer_u, peer_v, final_norm_g):
    B, S, D = x.shape
    T = B * S
    l = 0
    row = lambda a: a.reshape(1, -1)
    x2 = x.reshape(T, D)
    pos = positions.reshape(T, 1).astype(f32)
    freqs = cfg.rope_theta ** (-jnp.arange(0, cfg.rot_dim, 2, dtype=f32) / cfg.rot_dim)
    lane = jnp.arange(LANES) % cfg.diff_qkdim
    freq_row = jnp.where(lane < cfg.rot_dim, freqs[lane % (cfg.rot_dim // 2)], 0.0).reshape(1, LANES)

    lam_p = jnp.stack([lam_q1[l], lam_k1[l], lam_q2[l], lam_k2[l]])
    s5_params = _s5_params(cfg, ssm_a_re[l], ssm_a_im[l], ssm_log_dt[l], ssm_b_re[l], ssm_b_im[l],
                           ssm_c_re[l], ssm_c_im[l], ssm_d[l])
    ns, _, wn = s5_params[0].shape
    zero_state = jnp.zeros((ns, wn // LANES, B, LANES), f32)
    state = (zero_state, zero_state)
    w_in_b, glu_w_b, w_out_b = w_in[l].astype(bf16), glu_w[l].astype(bf16), w_out[l].astype(bf16)
    wq_b, wo_b, pw_b = xattn_wq[l].astype(bf16), xattn_wo[l].astype(bf16), peer_wq[l].astype(bf16)
    k1_b, k2_b = peer_k1[l].astype(bf16), peer_k2[l].astype(bf16)
    kv = _mem_kv(cfg, mem.reshape(B * cfg.n_mem, D), row(mem_norm_g[l]), xattn_wkv[l].astype(bf16))
    table = jnp.concatenate([_pack_bf16_halves(peer_u[l]), _pack_bf16_halves(peer_v[l])], axis=1)
    E, R = cfg.slots, SUBLANES
    Lc = cfg.chunk
    Tc = B * Lc
    td = Tc * cfg.direct_eighths // 8

    def after(a, tokens):
        return lax.optimization_barrier((a,) + tuple(tokens))[0] if tokens else a

    def dense(c, qks, vs, state, tokens):
        qk, v, u = _in_proj(cfg, c, x2, pos, after(row(mix_norm_g[l]), tokens), freq_row, w_in_b)
        qks, vs = qks + [qk], vs + [v]
        att = _diff_attn(cfg, lam_p, qks, vs, row(attn_head_g[l]))
        y, state = _s5_scan(cfg, u.reshape(B, Lc, cfg.ssm_width), s5_params, state)
        ssm = _s5_glu(cfg, y.reshape(Tc, cfg.ssm_width), glu_w_b, row(glu_b[l]), row(ssm_out_g[l]))
        h1, hn = _out_proj(cfg, c, x2, att, ssm, w_out_b, row(xattn_norm_g[l]))
        h2, qp = _xattn(cfg, h1, hn, kv, wq_b, wo_b, row(ffn_norm_g[l]), pw_b)
        experts_t, gates_t = _peer_route(cfg, qp, k1_b, k2_b)
        idx = experts_t.reshape(E, Tc // R, R).transpose(1, 0, 2).reshape(-1)
        gates = gates_t.T
        out_d = (_peer_direct(cfg, h2[:td], gates[:td], idx[:td * E], table, row(ffn_norm_g[l]),
                              row(final_norm_g)) if td else None)
        rows = gather_fn(table, idx[td * E:]) if td < Tc else None
        return qks, vs, state, experts_t, (h2[td:], gates[td:], rows, out_d)

    def apply(item, tokens):
        h2, gates, rows, out_d = item
        outs_c = [] if out_d is None else [out_d]
        if rows is not None:
            outs_c.append(_peer_apply(cfg, h2, gates, rows, after(row(ffn_norm_g[l]), tokens),
                                      row(final_norm_g)))
        return jnp.concatenate(outs_c, axis=0).reshape(B, Lc, D)

    lead = 2
    nc = cfg.n_chunks
    qks, vs, experts, items, outs = [], [], [], [], []
    for c in range(nc):
        tokens = experts[c - 1:c] + (outs[c - lead - 1:c - lead] if c > lead else [])
        qks, vs, state, e, item = dense(c, qks, vs, state, tokens)
        experts.append(e)
        items.append(item)
        if c >= lead:
            outs.append(apply(items[c - lead], [e]))
    for c in range(max(nc - lead, 0), nc):
        outs.append(apply(items[c], []))
    return jnp.concatenate(outs, axis=1)


def kernel(x, mem, positions, mix_norm_g, w_in, lam_q1, lam_k1, lam_q2, lam_k2, attn_head_g, ssm_a_re, ssm_a_im, ssm_log_dt, ssm_b_re, ssm_b_im, ssm_c_re, ssm_c_im, ssm_d, glu_w, glu_b, ssm_out_g, w_out, xattn_norm_g, mem_norm_g, xattn_wq, xattn_wkv, xattn_wo, ffn_norm_g, peer_wq, peer_k1, peer_k2, peer_u, peer_v, final_norm_g):
    cfg = Cfg()
    return _block(cfg, functools.partial(_sc_gather, cfg), x, mem, positions, mix_norm_g, w_in, lam_q1, lam_k1,
                  lam_q2, lam_k2, attn_head_g, ssm_a_re, ssm_a_im, ssm_log_dt, ssm_b_re, ssm_b_im, ssm_c_re,
                  ssm_c_im, ssm_d, glu_w, glu_b, ssm_out_g, w_out, xattn_norm_g, mem_norm_g, xattn_wq,
                  xattn_wkv, xattn_wo, ffn_norm_g, peer_wq, peer_k1, peer_k2, peer_u, peer_v, final_norm_g)
```

```python
import dataclasses
import functools
import math

import jax
import jax.numpy as jnp
from jax import lax
from jax.experimental import pallas as pl
from jax.experimental.pallas import tpu as pltpu
from jax.experimental.pallas import tpu_sc as plsc

f32 = jnp.float32
bf16 = jnp.bfloat16
i32 = jnp.int32

LANES = 128
SUBLANES = 8
VMEM_LIMIT = 56 * 1024 * 1024
NEG = -1e30
EPS = 1e-6


@dataclasses.dataclass(frozen=True)
class Cfg:
    d_model: int = 2048
    batch: int = 8
    seq: int = 2048
    n_mem: int = 256
    diff_heads: int = 8
    ssm_group: int = 16
    ssm_state: int = 64
    xattn_heads: int = 4
    xattn_head_dim: int = 128
    peer_heads: int = 8
    peer_keys: int = 128
    peer_qdim: int = 256
    peer_topk: int = 16
    rope_theta: float = 500000.0
    lam_init: float = 0.8 - 0.6 * math.exp(-0.3 * 0)
    chunk: int = 256
    scan_chunk: int = 128
    route_tm: int = 256
    gather_window: int = 8
    gather_bufs: int = 7
    gather_writes: int = 2
    apply_groups: int = 2
    direct_eighths: int = 8

    @property
    def n_chunks(self):
        return self.seq // self.chunk

    @property
    def attn_width(self):
        return self.d_model // 2

    @property
    def ssm_width(self):
        return self.d_model - self.attn_width

    @property
    def diff_vdim(self):
        return self.attn_width // self.diff_heads

    @property
    def diff_qkdim(self):
        return self.diff_vdim // 2

    @property
    def rot_dim(self):
        return self.diff_qkdim // 4

    @property
    def ssm_groups(self):
        return self.ssm_width // self.ssm_group

    @property
    def xattn_width(self):
        return self.xattn_heads * self.xattn_head_dim

    @property
    def tokens(self):
        return self.batch * self.seq

    @property
    def slots(self):
        return self.peer_heads * self.peer_topk


def _cparams(*sem):
    return pltpu.CompilerParams(dimension_semantics=sem, vmem_limit_bytes=VMEM_LIMIT)


def _resident(shape):
    nd = len(shape)
    return pl.BlockSpec(shape, lambda *_: (0,) * nd, pipeline_mode=pl.Buffered(1))


def _rms(x, g):
    return x * lax.rsqrt(jnp.mean(x * x, axis=-1, keepdims=True) + EPS) * g


def _gelu(x):
    return 0.5 * x * (1.0 + lax.erf(x * (2.0 ** -0.5)))


def _in_proj_kernel(x_ref, pos_ref, g_ref, freq_ref, w_ref, qk_ref, v_ref, u_ref, *, cfg):
    n_qk, n_v = 2 * cfg.attn_width, cfg.attn_width
    half = cfg.rot_dim // 2
    xn = _rms(x_ref[...], g_ref[...]).astype(bf16)
    ang = pos_ref[...] * freq_ref[...]
    cos, sin = jnp.cos(ang), jnp.sin(ang)
    lane = lax.broadcasted_iota(i32, (1, LANES), 1) % cfg.diff_qkdim
    sin_lo = jnp.where(lane < half, -sin, 0.0)
    sin_hi = jnp.where((lane >= half) & (lane < 2 * half), sin, 0.0)
    cw = 2 * LANES
    for c in range((n_qk + n_v + cfg.ssm_width) // cw):
        col = c * cw
        z = jnp.dot(xn, w_ref[:, col:col + cw], preferred_element_type=f32)
        if col < n_qk:
            for k in range(cw // LANES):
                zk = z[:, k * LANES:(k + 1) * LANES]
                zk = zk * cos + pltpu.roll(zk, LANES - half, 1) * sin_lo + pltpu.roll(zk, half, 1) * sin_hi
                qk_ref[:, col + k * LANES:col + (k + 1) * LANES] = zk.astype(bf16)
        elif col < n_qk + n_v:
            v_ref[:, col - n_qk:col - n_qk + cw] = z.astype(bf16)
        else:
            u_ref[:, col - n_qk - n_v:col - n_qk - n_v + cw] = z


def _chunk_rows(cfg, c):
    return lambda b: (b * cfg.n_chunks + c, 0)


def _in_proj(cfg, c, x2, pos, g, freq, w):
    D = x2.shape[1]
    B, tm = cfg.batch, cfg.chunk
    n_qk, n_v, n_u = 2 * cfg.attn_width, cfg.attn_width, cfg.ssm_width
    row = lambda b: (b, 0)
    return pl.pallas_call(
        functools.partial(_in_proj_kernel, cfg=cfg),
        grid=(B,),
        in_specs=[pl.BlockSpec((tm, D), _chunk_rows(cfg, c)), pl.BlockSpec((tm, 1), _chunk_rows(cfg, c)),
                  _resident((1, D)), _resident((1, LANES)), _resident(w.shape)],
        out_specs=[pl.BlockSpec((tm, n_qk), row), pl.BlockSpec((tm, n_v), row),
                   pl.BlockSpec((tm, n_u), lambda b: (0, b))],
        out_shape=[jax.ShapeDtypeStruct((B * tm, n_qk), bf16), jax.ShapeDtypeStruct((B * tm, n_v), bf16),
                   jax.ShapeDtypeStruct((tm, B * n_u), f32)],
        compiler_params=_cparams("parallel"),
        name="in_proj",
    )(x2, pos, g, freq, w)


def _diff_attn_kernel(lam_ref, q_ref, *rest, cfg, n_kv):
    k_refs, v_refs = rest[:n_kv], rest[n_kv:2 * n_kv]
    g_ref, o_ref = rest[2 * n_kv:]
    tq = cfg.chunk
    d = cfg.diff_qkdim
    lane = lax.broadcasted_iota(i32, (1, LANES), 1)
    q = q_ref[...].astype(f32) * (d ** -0.5)
    qs = (jnp.where(lane < d, q, 0.0).astype(bf16), jnp.where(lane >= d, q, 0.0).astype(bf16))
    causal = (lax.broadcasted_iota(i32, (tq, tq), 1) <= lax.broadcasted_iota(i32, (tq, tq), 0))
    tiles = lambda a: [a[:, t * LANES:(t + 1) * LANES] for t in range(tq // LANES)]

    def scores(c, j):
        s = lax.dot_general(qs[c], k_refs[j][...], (((1,), (1,)), ((), ())), preferred_element_type=f32)
        return jnp.where(causal, s, NEG) if j == n_kv - 1 else s

    outs = []
    for c in range(2):
        mm = jnp.full((tq, LANES), NEG, f32)
        for j in range(n_kv):
            for st in tiles(scores(c, j)):
                mm = jnp.maximum(mm, st)
        mb = jnp.broadcast_to(jnp.max(mm, axis=1, keepdims=True), (tq, LANES))
        ls = jnp.zeros((tq, LANES), f32)
        acc = jnp.zeros((tq, LANES), f32)
        for j in range(n_kv):
            ps = [jnp.exp(st - mb) for st in tiles(scores(c, j))]
            for p in ps:
                ls = ls + p
            acc = acc + jnp.dot(jnp.concatenate(ps, axis=1).astype(bf16), v_refs[j][...],
                                preferred_element_type=f32)
        outs.append(acc / jnp.sum(ls, axis=1, keepdims=True))

    lp = lam_ref[...]
    lam = (jnp.exp(jnp.sum(lp[0:1] * lp[1:2], axis=1, keepdims=True))
           - jnp.exp(jnp.sum(lp[2:3] * lp[3:4], axis=1, keepdims=True)) + cfg.lam_init)
    o = outs[0] - lam * outs[1]
    o_ref[...] = (_rms(o, g_ref[...]) * (1.0 - cfg.lam_init)).astype(o_ref.dtype)


def _diff_attn(cfg, lam_p, qks, vs, g):
    n_kv = len(qks)
    B, H, tq = cfg.batch, cfg.diff_heads, cfg.chunk
    head = lambda b, h: (b, h)
    key = lambda b, h: (b, H + h)
    return pl.pallas_call(
        functools.partial(_diff_attn_kernel, cfg=cfg, n_kv=n_kv),
        grid=(B, H),
        in_specs=([_resident(lam_p.shape), pl.BlockSpec((tq, LANES), head)]
                  + [pl.BlockSpec((tq, LANES), key)] * n_kv + [pl.BlockSpec((tq, LANES), head)] * n_kv
                  + [_resident((1, LANES))]),
        out_specs=pl.BlockSpec((tq, LANES), head),
        out_shape=jax.ShapeDtypeStruct((B * tq, cfg.attn_width), bf16),
        compiler_params=_cparams("parallel", "parallel"),
        name="diff_attn",
    )(lam_p, qks[-1], *qks, *vs, g)


def _s5_scan_kernel(u_ref, are_ref, aim_ref, ldt_ref, bre_ref, bim_ref, cre_ref, cim_ref, d_ref,
                    sin_re_ref, sin_im_ref, y_ref, sout_re_ref, sout_im_ref,
                    ab_ref, bbar_re_ref, bbar_im_ref, bu_re_ref, bu_im_ref, *, cfg):
    B, L = cfg.batch, cfg.scan_chunk
    t = pl.program_id(1)
    st_re_ref, st_im_ref = sout_re_ref.at[0], sout_im_ref.at[0]

    @pl.when(t == 0)
    def _():
        dt = jnp.exp(ldt_ref[0])
        lre, lim = are_ref[0], aim_ref[0]
        mag = jnp.exp(lre * dt)
        ang = lim * dt
        ab_re, ab_im = mag * jnp.cos(ang), mag * jnp.sin(ang)
        den = lre * lre + lim * lim
        f_re = ((ab_re - 1.0) * lre + ab_im * lim) / den
        f_im = (ab_im * lre - (ab_re - 1.0) * lim) / den
        ab_ref[0] = jnp.broadcast_to(ab_re, ab_ref.shape[1:])
        ab_ref[1] = jnp.broadcast_to(ab_im, ab_ref.shape[1:])
        br, bi = bre_ref[0], bim_ref[0]
        bbar_re_ref[...] = (f_re * br - f_im * bi).astype(bf16)
        bbar_im_ref[...] = (f_re * bi + f_im * br).astype(bf16)
        st_re_ref[...] = sin_re_ref[0]
        st_im_ref[...] = sin_im_ref[0]

    u = u_ref[...]
    ub = u.astype(bf16)
    bu_re_ref[...] = jnp.dot(ub, bbar_re_ref[...], preferred_element_type=f32)
    bu_im_ref[...] = jnp.dot(ub, bbar_im_ref[...], preferred_element_type=f32)
    a_re, a_im = ab_ref[0], ab_ref[1]

    def body(s, carry):
        xr, xi = carry
        rows = pl.ds(pl.multiple_of(s * B, B), B)
        nr = a_re * xr - a_im * xi + bu_re_ref[rows, :]
        ni = a_re * xi + a_im * xr + bu_im_ref[rows, :]
        bu_re_ref[rows, :] = nr
        bu_im_ref[rows, :] = ni
        return nr, ni

    xr, xi = lax.fori_loop(0, L, body, (st_re_ref[...], st_im_ref[...]), unroll=8)
    st_re_ref[...] = xr
    st_im_ref[...] = xi

    y_ref[...] = (jnp.dot(bu_re_ref[...].astype(bf16), cre_ref[0].astype(bf16), preferred_element_type=f32)
                  - jnp.dot(bu_im_ref[...].astype(bf16), cim_ref[0].astype(bf16), preferred_element_type=f32)
                  + d_ref[0] * u)


def _block_diag(w, blocks):
    G, r, c = w.shape
    w4 = w.reshape(G // blocks, blocks, r, c)
    eye = jnp.eye(blocks, dtype=w.dtype)
    out = w4[:, :, :, None, :] * eye[None, :, None, :, None]
    return out.reshape(G // blocks, blocks * r, blocks * c)


def _s5_params(cfg, a_re, a_im, log_dt, b_re, b_im, c_re, c_im, d_skip):
    G, P, Hc = cfg.ssm_groups, cfg.ssm_state, cfg.ssm_group
    gps = min(G, 2 * LANES // Hc)
    ns = G // gps
    wu, wn = gps * Hc, gps * P
    return (a_re.reshape(ns, 1, wn), a_im.reshape(ns, 1, wn), jnp.repeat(log_dt, P).reshape(ns, 1, wn),
            _block_diag(b_re.transpose(0, 2, 1), gps), _block_diag(b_im.transpose(0, 2, 1), gps),
            _block_diag(c_re.transpose(0, 2, 1), gps), _block_diag(c_im.transpose(0, 2, 1), gps),
            d_skip.reshape(ns, 1, wu))


def _s5_scan(cfg, u2, params, state):
    B, L = cfg.batch, cfg.scan_chunk
    rows, W = u2.shape
    ns, _, wn = params[0].shape
    wu = params[-1].shape[2]
    slab = lambda s, t: (s, 0, 0)
    st = pl.BlockSpec((1, B, wn), slab)
    st_shape = jax.ShapeDtypeStruct((ns, B, wn), f32)
    y, s_re, s_im = pl.pallas_call(
        functools.partial(_s5_scan_kernel, cfg=cfg),
        grid=(ns, rows // (L * B)),
        in_specs=[pl.BlockSpec((L * B, wu), lambda s, t: (t, s)),
                  pl.BlockSpec((1, 1, wn), slab), pl.BlockSpec((1, 1, wn), slab), pl.BlockSpec((1, 1, wn), slab),
                  pl.BlockSpec((1, wu, wn), slab), pl.BlockSpec((1, wu, wn), slab),
                  pl.BlockSpec((1, wn, wu), slab), pl.BlockSpec((1, wn, wu), slab),
                  pl.BlockSpec((1, 1, wu), slab), st, st],
        out_specs=[pl.BlockSpec((L * B, wu), lambda s, t: (t, s)), st, st],
        out_shape=[jax.ShapeDtypeStruct((rows, W), f32), st_shape, st_shape],
        scratch_shapes=[pltpu.VMEM((2, B, wn), f32),
                        pltpu.VMEM((wu, wn), bf16), pltpu.VMEM((wu, wn), bf16),
                        pltpu.VMEM((L * B, wn), f32), pltpu.VMEM((L * B, wn), f32)],
        compiler_params=_cparams("arbitrary", "arbitrary"),
        name="s5_scan",
    )(u2, *params, *state)
    return y, (s_re, s_im)


def _s5_glu_kernel(y_ref, w_ref, b_ref, g_ref, o_ref):
    g = _gelu(y_ref[...])
    z = jnp.dot(g.astype(bf16), w_ref[...], preferred_element_type=f32) + b_ref[...]
    g = g * jax.nn.sigmoid(z)
    o_ref[...] = _rms(g, g_ref[...]).astype(o_ref.dtype)


def _s5_glu(cfg, y2, w, b, g):
    T, W = y2.shape
    tm = cfg.chunk
    row = lambda i: (i, 0)
    return pl.pallas_call(
        _s5_glu_kernel,
        grid=(T // tm,),
        in_specs=[pl.BlockSpec((tm, W), row), _resident(w.shape), _resident((1, W)), _resident((1, W))],
        out_specs=pl.BlockSpec((tm, W), row),
        out_shape=jax.ShapeDtypeStruct((T, W), bf16),
        compiler_params=_cparams("parallel"),
        name="s5_glu",
    )(y2, w, b, g)


def _out_proj_kernel(x_ref, att_ref, ssm_ref, w_ref, g_ref, h_ref, hn_ref, *, cfg):
    aw = cfg.attn_width
    h = (x_ref[...]
         + jnp.dot(att_ref[...], w_ref[:aw, :], preferred_element_type=f32)
         + jnp.dot(ssm_ref[...], w_ref[aw:, :], preferred_element_type=f32))
    h_ref[...] = h
    hn_ref[...] = _rms(h, g_ref[...]).astype(hn_ref.dtype)


def _out_proj(cfg, c, x2, att, ssm, w, g):
    D = x2.shape[1]
    tm = cfg.chunk
    T = cfg.batch * tm
    row = lambda i: (i, 0)
    return pl.pallas_call(
        functools.partial(_out_proj_kernel, cfg=cfg),
        grid=(T // tm,),
        in_specs=[pl.BlockSpec((tm, D), _chunk_rows(cfg, c)), pl.BlockSpec((tm, cfg.attn_width), row),
                  pl.BlockSpec((tm, cfg.ssm_width), lambda b: (0, b)), _resident(w.shape), _resident((1, D))],
        out_specs=[pl.BlockSpec((tm, D), row), pl.BlockSpec((tm, D), row)],
        out_shape=[jax.ShapeDtypeStruct((T, D), f32), jax.ShapeDtypeStruct((T, D), bf16)],
        compiler_params=_cparams("parallel"),
        name="out_proj",
    )(x2, att, ssm, w, g)


def _mem_kv_kernel(m_ref, g_ref, w_ref, o_ref):
    mn = _rms(m_ref[...], g_ref[...]).astype(bf16)
    o_ref[...] = jnp.dot(mn, w_ref[...], preferred_element_type=f32).astype(o_ref.dtype)


def _mem_kv(cfg, mem2, g, w):
    R, D = mem2.shape
    tm = cfg.n_mem
    row = lambda i: (i, 0)
    return pl.pallas_call(
        _mem_kv_kernel,
        grid=(R // tm,),
        in_specs=[pl.BlockSpec((tm, D), row), _resident((1, D)), _resident(w.shape)],
        out_specs=pl.BlockSpec((tm, w.shape[1]), row),
        out_shape=jax.ShapeDtypeStruct((R, w.shape[1]), bf16),
        compiler_params=_cparams("parallel"),
        name="mem_kv",
    )(mem2, g, w)


def _xattn_kernel(h_ref, hn_ref, kv_ref, wq_ref, wo_ref, g_ref, pw_ref, h2_ref, qp_ref, *, cfg):
    nh, hd, xw = cfg.xattn_heads, cfg.xattn_head_dim, cfg.xattn_width
    q = jnp.dot(hn_ref[...], wq_ref[...], preferred_element_type=f32).astype(bf16)
    outs = []
    for h in range(nh):
        qh = q[:, h * hd:(h + 1) * hd]
        kh = kv_ref[:, h * hd:(h + 1) * hd]
        vh = kv_ref[:, xw + h * hd:xw + (h + 1) * hd]
        s = lax.dot_general(qh, kh, (((1,), (1,)), ((), ())), preferred_element_type=f32) * (hd ** -0.5)
        p = jnp.exp(s - jnp.max(s, axis=1, keepdims=True))
        p = p / jnp.sum(p, axis=1, keepdims=True)
        outs.append(jnp.dot(p.astype(bf16), vh, preferred_element_type=f32).astype(bf16))
    o = jnp.concatenate(outs, axis=1)
    h2 = h_ref[...] + jnp.dot(o, wo_ref[...], preferred_element_type=f32)
    h2_ref[...] = h2
    hp = _rms(h2, g_ref[...]).astype(bf16)
    qp_ref[...] = jnp.dot(hp, pw_ref[...], preferred_element_type=f32).astype(qp_ref.dtype)


def _xattn(cfg, h1, hn, kv, wq, wo, g, pw):
    T, D = h1.shape
    tm, M = cfg.chunk, cfg.n_mem
    row = lambda i: (i, 0)
    return pl.pallas_call(
        functools.partial(_xattn_kernel, cfg=cfg),
        grid=(T // tm,),
        in_specs=[pl.BlockSpec((tm, D), row), pl.BlockSpec((tm, D), row),
                  pl.BlockSpec((M, kv.shape[1]), row),
                  _resident(wq.shape), _resident(wo.shape), _resident((1, D)), _resident(pw.shape)],
        out_specs=[pl.BlockSpec((tm, D), row), pl.BlockSpec((tm, pw.shape[1]), row)],
        out_shape=[jax.ShapeDtypeStruct((T, D), f32), jax.ShapeDtypeStruct((T, pw.shape[1]), bf16)],
        compiler_params=_cparams("parallel"),
        name="xattn",
    )(h1, hn, kv, wq, wo, g, pw)


def _top_rows(s, k, payload=None):
    n = s.shape[0]
    rows = lax.broadcasted_iota(i32, s.shape, 0)
    vals, picks = [], []
    for _ in range(k):
        m = jnp.max(s, axis=0, keepdims=True)
        idx = jnp.min(jnp.where(s == m, rows, n), axis=0, keepdims=True)
        sel = rows == idx
        vals.append(m)
        picks.append(idx if payload is None else jnp.max(jnp.where(sel, payload, -1), axis=0, keepdims=True))
        s = jnp.where(sel, NEG, s)
    return vals, picks


def _peer_route_kernel(qp_ref, k1_ref, k2_ref, e_ref, g_ref, *, cfg):
    K, nk, half = cfg.peer_topk, cfg.peer_keys, cfg.peer_qdim // 2
    tm = qp_ref.shape[0]
    dn = (((1,), (1,)), ((), ()))
    s1 = lax.dot_general(k1_ref[0], qp_ref[:, :half], dn, preferred_element_type=f32)
    s2 = lax.dot_general(k2_ref[0], qp_ref[:, half:], dn, preferred_element_type=f32)
    t1, i1 = _top_rows(s1, K)
    t2, i2 = _top_rows(s2, K)
    t1, i1 = jnp.concatenate(t1, axis=0), jnp.concatenate(i1, axis=0)
    t2, i2 = jnp.concatenate(t2, axis=0), jnp.concatenate(i2, axis=0)
    cand = jnp.concatenate([t1[0:1] + t2] + [t1[a:a + 1] + t2[:K // 2] for a in range(1, K)], axis=0)
    expert = jnp.concatenate([i1[0:1] * nk + i2] + [i1[a:a + 1] * nk + i2[:K // 2] for a in range(1, K)], axis=0)
    ts, es = _top_rows(cand, K, payload=expert)
    ts = jnp.concatenate(ts, axis=0)
    p = jnp.exp(ts - ts[0:1])
    g_ref[...] = p / jnp.sum(p, axis=0, keepdims=True)
    e_ref[...] = jnp.concatenate(es, axis=0)


def _peer_route(cfg, qp, k1, k2):
    T = qp.shape[0]
    tm, H, K, Q = cfg.route_tm, cfg.peer_heads, cfg.peer_topk, cfg.peer_qdim
    out = pl.BlockSpec((K, tm), lambda i, h: (h, i))
    return pl.pallas_call(
        functools.partial(_peer_route_kernel, cfg=cfg),
        grid=(T // tm, H),
        in_specs=[pl.BlockSpec((tm, Q), lambda i, h: (i, h)),
                  pl.BlockSpec((1,) + k1.shape[1:], lambda i, h: (h, 0, 0)),
                  pl.BlockSpec((1,) + k2.shape[1:], lambda i, h: (h, 0, 0))],
        out_specs=[out, out],
        out_shape=[jax.ShapeDtypeStruct((H * K, T), i32), jax.ShapeDtypeStruct((H * K, T), f32)],
        compiler_params=_cparams("parallel", "parallel"),
        name="peer_route",
    )(qp, k1, k2)


def _sc_gather(cfg, table, idx):
    n = idx.shape[0]
    d = table.shape[1]
    win, nb, nwr = cfg.gather_window, cfg.gather_bufs, cfg.gather_writes
    mesh = plsc.VectorSubcoreMesh(core_axis_name="core", subcore_axis_name="subcore")
    workers = mesh.num_cores * mesh.num_subcores
    per = n // workers
    steps = per // win
    assert per * workers == n and steps * win == per and steps >= nb > nwr

    @functools.partial(pl.kernel, out_type=jax.ShapeDtypeStruct((n, d), table.dtype), mesh=mesh,
                       scratch_types=[pltpu.VMEM((per,), i32)] + [pltpu.VMEM((win, d), table.dtype)] * nb
                       + [pltpu.SemaphoreType.DMA((nb,)), pltpu.SemaphoreType.DMA((nb,))])
    def gather_kernel(tab_hbm, idx_hbm, out_hbm, idx_vmem, *rest):
        bufs, gsem, wsem = rest[:nb], rest[nb], rest[nb + 1]
        wid = lax.axis_index("core") * mesh.num_subcores + lax.axis_index("subcore")
        base = wid * per
        pltpu.sync_copy(idx_hbm.at[pl.ds(base, per)], idx_vmem)

        def gather(k, b):
            return pltpu.make_async_copy(tab_hbm.at[idx_vmem.at[pl.ds(k * win, win)]], bufs[b], gsem.at[b])

        def write(k, b):
            return pltpu.make_async_copy(bufs[b], out_hbm.at[pl.ds(base + k * win, win)], wsem.at[b])

        for b in range(nb - nwr):
            gather(b, b).start()

        def step(k, b, when):
            pb = (b - nwr) % nb
            gather(k, b).wait()
            write(k, b).start()
            when(k >= nwr, lambda: write(k - nwr, pb).wait())
            when(k + nb - nwr < steps, lambda: gather(k + nb - nwr, pb).start())

        main = steps // nb * nb

        @pl.loop(0, main, step=nb)
        def _(k0):
            for b in range(nb):
                step(k0 + b, b, lambda cond, fn: pl.when(cond)(fn))

        for k in range(main, steps):
            step(k, k % nb, lambda cond, fn: fn() if cond else None)

        for j in range(nwr):
            k = steps - nwr + j
            write(k, k % nb).wait()

    return gather_kernel(table, idx)


def _pack_bf16_halves(a):
    h = a.shape[1] // 2
    b = lax.bitcast_convert_type(a.astype(bf16), jnp.uint16).astype(jnp.uint32)
    return b[:, :h] | (b[:, h:] << 16)


def _unpack_bf16_halves(words):
    return pltpu.bitcast(words << 16, f32), pltpu.bitcast(words & jnp.uint32(0xFFFF0000), f32)


def _split_bf16(a):
    hi = a.astype(bf16)
    lo = (a - hi.astype(f32)).astype(bf16)
    return jnp.concatenate([hi, lo], axis=0)


def _fold_rows(a):
    n = a.shape[0] // 2
    return a[:n] + a[n:]


def _apply_group(h2, gates, words, sel, selt, gn, gf):
    R, D = h2.shape
    nw = D // 2
    nz = 2 * words.shape[0]
    lane = lax.broadcasted_iota(i32, (2 * R, nz), 1)
    rowi = lax.broadcasted_iota(i32, (2 * R, nz), 0)
    mine = ((lane // 2) % R == rowi % R) & (lane % 2 == rowi // R)
    zu = pltpu.bitcast(words[:, :nw], bf16)
    zv = pltpu.bitcast(words[:, nw:], bf16)
    x = _rms(h2, gn)
    xs = _split_bf16(jnp.concatenate([x[:, :nw], x[:, nw:]], axis=0))
    a = _fold_rows(lax.dot_general(xs, zu, (((1,), (1,)), ((), ())), preferred_element_type=f32))
    a = jnp.where(mine, a, 0.0)
    a = a[:R] + a[R:]
    act = _fold_rows(jnp.dot(_split_bf16(a), sel, preferred_element_type=f32))
    w = gates * _gelu(act)
    w_rows = _fold_rows(jnp.dot(_split_bf16(w), selt, preferred_element_type=f32))
    ws = _split_bf16(jnp.where(mine, jnp.concatenate([w_rows, w_rows], axis=0), 0.0))
    y2 = _fold_rows(jnp.dot(ws, zv, preferred_element_type=f32))
    y = jnp.concatenate([y2[:R], y2[R:]], axis=1)
    return _rms(h2 + y, gf)


def _peer_apply_kernel(h_ref, gt_ref, rows_ref, sel_ref, selt_ref, gn_ref, gf_ref, o_ref, *, cfg):
    E, R = cfg.slots, SUBLANES
    for gi in range(cfg.apply_groups):
        tok = slice(gi * R, (gi + 1) * R)
        o_ref[tok, :] = _apply_group(h_ref[tok, :], gt_ref[tok, :], rows_ref[gi * E * R:(gi + 1) * E * R, :],
                                     sel_ref[...], selt_ref[...], gn_ref[...], gf_ref[...])


def _peer_direct_kernel(idx_ref, idxn_ref, h_ref, gt_ref, tab_ref, sel_ref, selt_ref, gn_ref, gf_ref, o_ref,
                        buf_a, buf_b, sem_ref, *, cfg):
    E, R = cfg.slots, SUBLANES
    n_rows = E * R
    i = pl.program_id(0)

    def start_row(src_idx_ref, base, r, buf, s, priority=0):
        pltpu.async_copy(tab_ref.at[pl.ds(src_idx_ref[base + r], 1)], buf.at[pl.ds(r, 1)], sem_ref.at[s],
                         priority=priority)

    def wait_rows(buf, s):
        pltpu.make_async_copy(tab_ref.at[pl.ds(0, n_rows)], buf, sem_ref.at[s]).wait()

    def group(gi, buf):
        tok = slice(gi * R, (gi + 1) * R)
        o_ref[tok, :] = _apply_group(h_ref[tok, :], gt_ref[tok, :], buf[...], sel_ref[...], selt_ref[...],
                                     gn_ref[...], gf_ref[...])

    @pl.when(i == 0)
    def _():
        def body(r, carry):
            start_row(idx_ref, 0, r, buf_a, 0)
            return carry
        lax.fori_loop(0, n_rows, body, 0, unroll=8)

    wait_rows(buf_a, 0)
    for r in range(n_rows):
        start_row(idx_ref, n_rows, r, buf_b, 1, priority=r % 2)
    group(0, buf_a)
    wait_rows(buf_b, 1)
    for r in range(n_rows):
        start_row(idxn_ref, 0, r, buf_a, 0, priority=r % 2)
    group(1, buf_b)

    @pl.when(i == pl.num_programs(0) - 1)
    def _():
        wait_rows(buf_a, 0)


def _peer_direct(cfg, h2, gates, idx, table, gn, gf):
    T, D = h2.shape
    E, R = cfg.slots, SUBLANES
    n_rows = E * R
    nz = 2 * n_rows
    n = T // (2 * R)
    sel = (jnp.arange(nz)[:, None] // (2 * R) == jnp.arange(E)[None, :]).astype(bf16)
    row = lambda i: (i, 0)
    smem = functools.partial(pl.BlockSpec, (2 * n_rows,), memory_space=pltpu.SMEM)
    return pl.pallas_call(
        functools.partial(_peer_direct_kernel, cfg=cfg),
        grid=(n,),
        in_specs=[smem(lambda i: (i,)), smem(lambda i: (jnp.minimum(i + 1, n - 1),)),
                  pl.BlockSpec((2 * R, D), row), pl.BlockSpec((2 * R, E), row), pl.BlockSpec(memory_space=pl.ANY),
                  _resident((nz, E)), _resident((E, nz)), _resident((1, D)), _resident((1, D))],
        out_specs=pl.BlockSpec((2 * R, D), row),
        out_shape=jax.ShapeDtypeStruct((T, D), f32),
        scratch_shapes=[pltpu.VMEM((n_rows, D), table.dtype), pltpu.VMEM((n_rows, D), table.dtype),
                        pltpu.SemaphoreType.DMA((2,))],
        compiler_params=pltpu.CompilerParams(dimension_semantics=("arbitrary",), vmem_limit_bytes=VMEM_LIMIT,
                                             disable_bounds_checks=True),
        name="peer_direct",
    )(idx, idx, h2, gates, table, sel, sel.T, gn, gf)


def _peer_apply(cfg, h2, gates, rows, gn, gf):
    T, D = h2.shape
    E, R, G = cfg.slots, SUBLANES, cfg.apply_groups
    assert E == LANES and rows.shape == (T * E, D)
    nz = 2 * R * E
    sel = (jnp.arange(nz)[:, None] // (2 * R) == jnp.arange(E)[None, :]).astype(bf16)
    row = lambda i: (i, 0)
    return pl.pallas_call(
        functools.partial(_peer_apply_kernel, cfg=cfg),
        grid=(T // (R * G),),
        in_specs=[pl.BlockSpec((R * G, D), row), pl.BlockSpec((R * G, E), row),
                  pl.BlockSpec((G * E * R, D), row), _resident((nz, E)), _resident((E, nz)),
                  _resident((1, D)), _resident((1, D))],
        out_specs=pl.BlockSpec((R * G, D), row),
        out_shape=jax.ShapeDtypeStruct((T, D), f32),
        compiler_params=_cparams("parallel"),
        name="peer_apply",
    )(h2, gates, rows, sel, sel.T, gn, gf)


def _block(cfg, gather_fn, x, mem, positions, mix_norm_g, w_in, lam_q1, lam_k1, lam_q2, lam_k2, attn_head_g,
           ssm_a_re, ssm_a_im, ssm_log_dt, ssm_b_re, ssm_b_im, ssm_c_re, ssm_c_im, ssm_d, glu_w, glu_b,
           ssm_out_g, w_out, xattn_norm_g, mem_norm_g, xattn_wq, xattn_wkv, xattn_wo, ffn_norm_g,
           peer_wq, peer_k1, peer_k2, peer_u, peer_v, final_norm_g):
    B, S, D = x.shape
    T = B * S
    l = 0
    row = lambda a: a.reshape(1, -1)
    x2 = x.reshape(T, D)
    pos = positions.reshape(T, 1).astype(f32)
    freqs = cfg.rope_theta ** (-jnp.arange(0, cfg.rot_dim, 2, dtype=f32) / cfg.rot_dim)
    lane = jnp.arange(LANES) % cfg.diff_qkdim
    freq_row = jnp.where(lane < cfg.rot_dim, freqs[lane % (cfg.rot_dim // 2)], 0.0).reshape(1, LANES)

    lam_p = jnp.stack([lam_q1[l], lam_k1[l], lam_q2[l], lam_k2[l]])
    s5_params = _s5_params(cfg, ssm_a_re[l], ssm_a_im[l], ssm_log_dt[l], ssm_b_re[l], ssm_b_im[l],
                           ssm_c_re[l], ssm_c_im[l], ssm_d[l])
    ns, _, wn = s5_params[0].shape
    zero_state = jnp.zeros((ns, B, wn), f32)
    state = (zero_state, zero_state)
    w_in_b, glu_w_b, w_out_b = w_in[l].astype(bf16), glu_w[l].astype(bf16), w_out[l].astype(bf16)
    wq_b, wo_b, pw_b = xattn_wq[l].astype(bf16), xattn_wo[l].astype(bf16), peer_wq[l].astype(bf16)
    k1_b, k2_b = peer_k1[l].astype(bf16), peer_k2[l].astype(bf16)
    kv = _mem_kv(cfg, mem.reshape(B * cfg.n_mem, D), row(mem_norm_g[l]), xattn_wkv[l].astype(bf16))
    table = jnp.concatenate([_pack_bf16_halves(peer_u[l]), _pack_bf16_halves(peer_v[l])], axis=1)
    E, R = cfg.slots, SUBLANES
    Lc = cfg.chunk
    Tc = B * Lc
    td = Tc * cfg.direct_eighths // 8

    def after(a, tokens):
        return lax.optimization_barrier((a,) + tuple(tokens))[0] if tokens else a

    def dense(c, qks, vs, state, tokens):
        qk, v, u = _in_proj(cfg, c, x2, pos, after(row(mix_norm_g[l]), tokens), freq_row, w_in_b)
        qks, vs = qks + [qk], vs + [v]
        att = _diff_attn(cfg, lam_p, qks, vs, row(attn_head_g[l]))
        y, state = _s5_scan(cfg, u.reshape(Tc, cfg.ssm_width), s5_params, state)
        ssm = _s5_glu(cfg, y, glu_w_b, row(glu_b[l]), row(ssm_out_g[l]))
        h1, hn = _out_proj(cfg, c, x2, att, ssm.reshape(Lc, B * cfg.ssm_width), w_out_b, row(xattn_norm_g[l]))
        h2, qp = _xattn(cfg, h1, hn, kv, wq_b, wo_b, row(ffn_norm_g[l]), pw_b)
        experts_t, gates_t = _peer_route(cfg, qp, k1_b, k2_b)
        idx = experts_t.reshape(E, Tc // R, R).transpose(1, 0, 2).reshape(-1)
        gates = gates_t.T
        out_d = (_peer_direct(cfg, h2[:td], gates[:td], idx[:td * E], table, row(ffn_norm_g[l]),
                              row(final_norm_g)) if td else None)
        rows = gather_fn(table, idx[td * E:]) if td < Tc else None
        return qks, vs, state, experts_t, (h2[td:], gates[td:], rows, out_d)

    def apply(item, tokens):
        h2, gates, rows, out_d = item
        outs_c = [] if out_d is None else [out_d]
        if rows is not None:
            outs_c.append(_peer_apply(cfg, h2, gates, rows, after(row(ffn_norm_g[l]), tokens),
                                      row(final_norm_g)))
        return jnp.concatenate(outs_c, axis=0).reshape(B, Lc, D)

    lead = 2
    nc = cfg.n_chunks
    qks, vs, experts, items, outs = [], [], [], [], []
    for c in range(nc):
        tokens = experts[c - 1:c] + (outs[c - lead - 1:c - lead] if c > lead else [])
        qks, vs, state, e, item = dense(c, qks, vs, state, tokens)
        experts.append(e)
        items.append(item)
        if c >= lead:
            outs.append(apply(items[c - lead], [e]))
    for c in range(max(nc - lead, 0), nc):
        outs.append(apply(items[c], []))
    return jnp.concatenate(outs, axis=1)


def kernel(x, mem, positions, mix_norm_g, w_in, lam_q1, lam_k1, lam_q2, lam_k2, attn_head_g, ssm_a_re, ssm_a_im, ssm_log_dt, ssm_b_re, ssm_b_im, ssm_c_re, ssm_c_im, ssm_d, glu_w, glu_b, ssm_out_g, w_out, xattn_norm_g, mem_norm_g, xattn_wq, xattn_wkv, xattn_wo, ffn_norm_g, peer_wq, peer_k1, peer_k2, peer_u, peer_v, final_norm_g):
    cfg = Cfg()
    return _block(cfg, functools.partial(_sc_gather, cfg), x, mem, positions, mix_norm_g, w_in, lam_q1, lam_k1,
                  lam_q2, lam_k2, attn_head_g, ssm_a_re, ssm_a_im, ssm_log_dt, ssm_b_re, ssm_b_im, ssm_c_re,
                  ssm_c_im, ssm_d, glu_w, glu_b, ssm_out_g, w_out, xattn_norm_g, mem_norm_g, xattn_wq,
                  xattn_wkv, xattn_wo, ffn_norm_g, peer_wq, peer_k1, peer_k2, peer_u, peer_v, final_norm_g)
```

```python
import dataclasses
import functools
import math

import jax
import jax.numpy as jnp
from jax import lax
from jax.experimental import pallas as pl
from jax.experimental.pallas import tpu as pltpu
from jax.experimental.pallas import tpu_sc as plsc

f32 = jnp.float32
bf16 = jnp.bfloat16
i32 = jnp.int32

LANES = 128
SUBLANES = 8
VMEM_LIMIT = 56 * 1024 * 1024
NEG = -1e30
EPS = 1e-6


@dataclasses.dataclass(frozen=True)
class Cfg:
    d_model: int = 2048
    batch: int = 8
    seq: int = 2048
    n_mem: int = 256
    diff_heads: int = 8
    ssm_group: int = 16
    ssm_state: int = 64
    xattn_heads: int = 4
    xattn_head_dim: int = 128
    peer_heads: int = 8
    peer_keys: int = 128
    peer_qdim: int = 256
    peer_topk: int = 16
    rope_theta: float = 500000.0
    lam_init: float = 0.8 - 0.6 * math.exp(-0.3 * 0)
    chunk: int = 256
    scan_chunk: int = 128
    route_tm: int = 256
    direct_eighths: int = 3

    @property
    def n_chunks(self):
        return self.seq // self.chunk

    @property
    def attn_width(self):
        return self.d_model // 2

    @property
    def ssm_width(self):
        return self.d_model - self.attn_width

    @property
    def diff_vdim(self):
        return self.attn_width // self.diff_heads

    @property
    def diff_qkdim(self):
        return self.diff_vdim // 2

    @property
    def rot_dim(self):
        return self.diff_qkdim // 4

    @property
    def ssm_groups(self):
        return self.ssm_width // self.ssm_group

    @property
    def xattn_width(self):
        return self.xattn_heads * self.xattn_head_dim

    @property
    def tokens(self):
        return self.batch * self.seq

    @property
    def slots(self):
        return self.peer_heads * self.peer_topk


def _cparams(*sem):
    return pltpu.CompilerParams(dimension_semantics=sem, vmem_limit_bytes=VMEM_LIMIT)


def _resident(shape):
    nd = len(shape)
    return pl.BlockSpec(shape, lambda *_: (0,) * nd, pipeline_mode=pl.Buffered(1))


def _rms(x, g):
    return x * lax.rsqrt(jnp.mean(x * x, axis=-1, keepdims=True) + EPS) * g


def _gelu(x):
    return 0.5 * x * (1.0 + lax.erf(x * (2.0 ** -0.5)))


def _in_proj_kernel(x_ref, pos_ref, g_ref, freq_ref, w_ref, qk_ref, v_ref, u_ref, *, cfg):
    n_qk, n_v = 2 * cfg.attn_width, cfg.attn_width
    half = cfg.rot_dim // 2
    xn = _rms(x_ref[...], g_ref[...]).astype(bf16)
    ang = pos_ref[...] * freq_ref[...]
    cos, sin = jnp.cos(ang), jnp.sin(ang)
    lane = lax.broadcasted_iota(i32, (1, LANES), 1) % cfg.diff_qkdim
    sin_lo = jnp.where(lane < half, -sin, 0.0)
    sin_hi = jnp.where((lane >= half) & (lane < 2 * half), sin, 0.0)
    cw = 2 * LANES
    for c in range((n_qk + n_v + cfg.ssm_width) // cw):
        col = c * cw
        z = jnp.dot(xn, w_ref[:, col:col + cw], preferred_element_type=f32)
        if col < n_qk:
            for k in range(cw // LANES):
                zk = z[:, k * LANES:(k + 1) * LANES]
                zk = zk * cos + pltpu.roll(zk, LANES - half, 1) * sin_lo + pltpu.roll(zk, half, 1) * sin_hi
                qk_ref[:, col + k * LANES:col + (k + 1) * LANES] = zk.astype(bf16)
        elif col < n_qk + n_v:
            v_ref[:, col - n_qk:col - n_qk + cw] = z.astype(bf16)
        else:
            u_ref[:, col - n_qk - n_v:col - n_qk - n_v + cw] = z


def _chunk_rows(cfg, c):
    return lambda b: (b * cfg.n_chunks + c, 0)


def _in_proj(cfg, c, x2, pos, g, freq, w):
    D = x2.shape[1]
    B, tm = cfg.batch, cfg.chunk
    n_qk, n_v, n_u = 2 * cfg.attn_width, cfg.attn_width, cfg.ssm_width
    row = lambda b: (b, 0)
    return pl.pallas_call(
        functools.partial(_in_proj_kernel, cfg=cfg),
        grid=(B,),
        in_specs=[pl.BlockSpec((tm, D), _chunk_rows(cfg, c)), pl.BlockSpec((tm, 1), _chunk_rows(cfg, c)),
                  _resident((1, D)), _resident((1, LANES)), _resident(w.shape)],
        out_specs=[pl.BlockSpec((tm, n_qk), row), pl.BlockSpec((tm, n_v), row),
                   pl.BlockSpec((tm, n_u), lambda b: (0, b))],
        out_shape=[jax.ShapeDtypeStruct((B * tm, n_qk), bf16), jax.ShapeDtypeStruct((B * tm, n_v), bf16),
                   jax.ShapeDtypeStruct((tm, B * n_u), f32)],
        compiler_params=_cparams("parallel"),
        name="in_proj",
    )(x2, pos, g, freq, w)


def _diff_attn_kernel(lam_ref, q_ref, *rest, cfg, n_kv):
    k_refs, v_refs = rest[:n_kv], rest[n_kv:2 * n_kv]
    g_ref, o_ref = rest[2 * n_kv:]
    tq = cfg.chunk
    d = cfg.diff_qkdim
    lane = lax.broadcasted_iota(i32, (1, LANES), 1)
    q = q_ref[...].astype(f32) * (d ** -0.5)
    qs = (jnp.where(lane < d, q, 0.0).astype(bf16), jnp.where(lane >= d, q, 0.0).astype(bf16))
    causal = (lax.broadcasted_iota(i32, (tq, tq), 1) <= lax.broadcasted_iota(i32, (tq, tq), 0))
    tiles = lambda a: [a[:, t * LANES:(t + 1) * LANES] for t in range(tq // LANES)]

    def scores(c, j):
        s = lax.dot_general(qs[c], k_refs[j][...], (((1,), (1,)), ((), ())), preferred_element_type=f32)
        return jnp.where(causal, s, NEG) if j == n_kv - 1 else s

    outs = []
    for c in range(2):
        mm = jnp.full((tq, LANES), NEG, f32)
        for j in range(n_kv):
            for st in tiles(scores(c, j)):
                mm = jnp.maximum(mm, st)
        mb = jnp.broadcast_to(jnp.max(mm, axis=1, keepdims=True), (tq, LANES))
        ls = jnp.zeros((tq, LANES), f32)
        acc = jnp.zeros((tq, LANES), f32)
        for j in range(n_kv):
            ps = [jnp.exp(st - mb) for st in tiles(scores(c, j))]
            for p in ps:
                ls = ls + p
            acc = acc + jnp.dot(jnp.concatenate(ps, axis=1).astype(bf16), v_refs[j][...],
                                preferred_element_type=f32)
        outs.append(acc / jnp.sum(ls, axis=1, keepdims=True))

    lp = lam_ref[...]
    lam = (jnp.exp(jnp.sum(lp[0:1] * lp[1:2], axis=1, keepdims=True))
           - jnp.exp(jnp.sum(lp[2:3] * lp[3:4], axis=1, keepdims=True)) + cfg.lam_init)
    o = outs[0] - lam * outs[1]
    o_ref[...] = (_rms(o, g_ref[...]) * (1.0 - cfg.lam_init)).astype(o_ref.dtype)


def _diff_attn(cfg, lam_p, qks, vs, g):
    n_kv = len(qks)
    B, H, tq = cfg.batch, cfg.diff_heads, cfg.chunk
    head = lambda b, h: (b, h)
    key = lambda b, h: (b, H + h)
    return pl.pallas_call(
        functools.partial(_diff_attn_kernel, cfg=cfg, n_kv=n_kv),
        grid=(B, H),
        in_specs=([_resident(lam_p.shape), pl.BlockSpec((tq, LANES), head)]
                  + [pl.BlockSpec((tq, LANES), key)] * n_kv + [pl.BlockSpec((tq, LANES), head)] * n_kv
                  + [_resident((1, LANES))]),
        out_specs=pl.BlockSpec((tq, LANES), head),
        out_shape=jax.ShapeDtypeStruct((B * tq, cfg.attn_width), bf16),
        compiler_params=_cparams("parallel", "parallel"),
        name="diff_attn",
    )(lam_p, qks[-1], *qks, *vs, g)


def _s5_scan_kernel(u_ref, are_ref, aim_ref, ldt_ref, bre_ref, bim_ref, cre_ref, cim_ref, d_ref,
                    sin_re_ref, sin_im_ref, y_ref, sout_re_ref, sout_im_ref,
                    ab_ref, bbar_re_ref, bbar_im_ref, bu_re_ref, bu_im_ref, *, cfg):
    B, L = cfg.batch, cfg.scan_chunk
    t = pl.program_id(1)
    st_re_ref, st_im_ref = sout_re_ref.at[0], sout_im_ref.at[0]

    @pl.when(t == 0)
    def _():
        dt = jnp.exp(ldt_ref[0])
        lre, lim = are_ref[0], aim_ref[0]
        mag = jnp.exp(lre * dt)
        ang = lim * dt
        ab_re, ab_im = mag * jnp.cos(ang), mag * jnp.sin(ang)
        den = lre * lre + lim * lim
        f_re = ((ab_re - 1.0) * lre + ab_im * lim) / den
        f_im = (ab_im * lre - (ab_re - 1.0) * lim) / den
        ab_ref[0] = jnp.broadcast_to(ab_re, ab_ref.shape[1:])
        ab_ref[1] = jnp.broadcast_to(ab_im, ab_ref.shape[1:])
        br, bi = bre_ref[0], bim_ref[0]
        bbar_re_ref[...] = (f_re * br - f_im * bi).astype(bf16)
        bbar_im_ref[...] = (f_re * bi + f_im * br).astype(bf16)
        st_re_ref[...] = sin_re_ref[0]
        st_im_ref[...] = sin_im_ref[0]

    u = u_ref[...]
    ub = u.astype(bf16)
    bu_re_ref[...] = jnp.dot(ub, bbar_re_ref[...], preferred_element_type=f32)
    bu_im_ref[...] = jnp.dot(ub, bbar_im_ref[...], preferred_element_type=f32)
    a_re, a_im = ab_ref[0], ab_ref[1]

    def body(s, carry):
        xr, xi = carry
        rows = pl.ds(pl.multiple_of(s * B, B), B)
        nr = a_re * xr - a_im * xi + bu_re_ref[rows, :]
        ni = a_re * xi + a_im * xr + bu_im_ref[rows, :]
        bu_re_ref[rows, :] = nr
        bu_im_ref[rows, :] = ni
        return nr, ni

    xr, xi = lax.fori_loop(0, L, body, (st_re_ref[...], st_im_ref[...]), unroll=8)
    st_re_ref[...] = xr
    st_im_ref[...] = xi

    y_ref[...] = (jnp.dot(bu_re_ref[...].astype(bf16), cre_ref[0].astype(bf16), preferred_element_type=f32)
                  - jnp.dot(bu_im_ref[...].astype(bf16), cim_ref[0].astype(bf16), preferred_element_type=f32)
                  + d_ref[0] * u)


def _block_diag(w, blocks):
    G, r, c = w.shape
    w4 = w.reshape(G // blocks, blocks, r, c)
    eye = jnp.eye(blocks, dtype=w.dtype)
    out = w4[:, :, :, None, :] * eye[None, :, None, :, None]
    return out.reshape(G // blocks, blocks * r, blocks * c)


def _s5_params(cfg, a_re, a_im, log_dt, b_re, b_im, c_re, c_im, d_skip):
    G, P, Hc = cfg.ssm_groups, cfg.ssm_state, cfg.ssm_group
    gps = min(G, 2 * LANES // Hc)
    ns = G // gps
    wu, wn = gps * Hc, gps * P
    return (a_re.reshape(ns, 1, wn), a_im.reshape(ns, 1, wn), jnp.repeat(log_dt, P).reshape(ns, 1, wn),
            _block_diag(b_re.transpose(0, 2, 1), gps), _block_diag(b_im.transpose(0, 2, 1), gps),
            _block_diag(c_re.transpose(0, 2, 1), gps), _block_diag(c_im.transpose(0, 2, 1), gps),
            d_skip.reshape(ns, 1, wu))


def _s5_scan(cfg, u2, params, state):
    B, L = cfg.batch, cfg.scan_chunk
    rows, W = u2.shape
    ns, _, wn = params[0].shape
    wu = params[-1].shape[2]
    slab = lambda s, t: (s, 0, 0)
    st = pl.BlockSpec((1, B, wn), slab)
    st_shape = jax.ShapeDtypeStruct((ns, B, wn), f32)
    y, s_re, s_im = pl.pallas_call(
        functools.partial(_s5_scan_kernel, cfg=cfg),
        grid=(ns, rows // (L * B)),
        in_specs=[pl.BlockSpec((L * B, wu), lambda s, t: (t, s)),
                  pl.BlockSpec((1, 1, wn), slab), pl.BlockSpec((1, 1, wn), slab), pl.BlockSpec((1, 1, wn), slab),
                  pl.BlockSpec((1, wu, wn), slab), pl.BlockSpec((1, wu, wn), slab),
                  pl.BlockSpec((1, wn, wu), slab), pl.BlockSpec((1, wn, wu), slab),
                  pl.BlockSpec((1, 1, wu), slab), st, st],
        out_specs=[pl.BlockSpec((L * B, wu), lambda s, t: (t, s)), st, st],
        out_shape=[jax.ShapeDtypeStruct((rows, W), f32), st_shape, st_shape],
        scratch_shapes=[pltpu.VMEM((2, B, wn), f32),
                        pltpu.VMEM((wu, wn), bf16), pltpu.VMEM((wu, wn), bf16),
                        pltpu.VMEM((L * B, wn), f32), pltpu.VMEM((L * B, wn), f32)],
        compiler_params=_cparams("arbitrary", "arbitrary"),
        name="s5_scan",
    )(u2, *params, *state)
    return y, (s_re, s_im)


def _s5_glu_kernel(y_ref, w_ref, b_ref, g_ref, o_ref):
    g = _gelu(y_ref[...])
    z = jnp.dot(g.astype(bf16), w_ref[...], preferred_element_type=f32) + b_ref[...]
    g = g * jax.nn.sigmoid(z)
    o_ref[...] = _rms(g, g_ref[...]).astype(o_ref.dtype)


def _s5_glu(cfg, y2, w, b, g):
    T, W = y2.shape
    tm = cfg.chunk
    row = lambda i: (i, 0)
    return pl.pallas_call(
        _s5_glu_kernel,
        grid=(T // tm,),
        in_specs=[pl.BlockSpec((tm, W), row), _resident(w.shape), _resident((1, W)), _resident((1, W))],
        out_specs=pl.BlockSpec((tm, W), row),
        out_shape=jax.ShapeDtypeStruct((T, W), bf16),
        compiler_params=_cparams("parallel"),
        name="s5_glu",
    )(y2, w, b, g)


def _out_proj_kernel(x_ref, att_ref, ssm_ref, w_ref, g_ref, h_ref, hn_ref, *, cfg):
    aw = cfg.attn_width
    h = (x_ref[...]
         + jnp.dot(att_ref[...], w_ref[:aw, :], preferred_element_type=f32)
         + jnp.dot(ssm_ref[...], w_ref[aw:, :], preferred_element_type=f32))
    h_ref[...] = h
    hn_ref[...] = _rms(h, g_ref[...]).astype(hn_ref.dtype)


def _out_proj(cfg, c, x2, att, ssm, w, g):
    D = x2.shape[1]
    tm = cfg.chunk
    T = cfg.batch * tm
    row = lambda i: (i, 0)
    return pl.pallas_call(
        functools.partial(_out_proj_kernel, cfg=cfg),
        grid=(T // tm,),
        in_specs=[pl.BlockSpec((tm, D), _chunk_rows(cfg, c)), pl.BlockSpec((tm, cfg.attn_width), row),
                  pl.BlockSpec((tm, cfg.ssm_width), lambda b: (0, b)), _resident(w.shape), _resident((1, D))],
        out_specs=[pl.BlockSpec((tm, D), row), pl.BlockSpec((tm, D), row)],
        out_shape=[jax.ShapeDtypeStruct((T, D), f32), jax.ShapeDtypeStruct((T, D), bf16)],
        compiler_params=_cparams("parallel"),
        name="out_proj",
    )(x2, att, ssm, w, g)


def _mem_kv_kernel(m_ref, g_ref, w_ref, o_ref):
    mn = _rms(m_ref[...], g_ref[...]).astype(bf16)
    o_ref[...] = jnp.dot(mn, w_ref[...], preferred_element_type=f32).astype(o_ref.dtype)


def _mem_kv(cfg, mem2, g, w):
    R, D = mem2.shape
    tm = cfg.n_mem
    row = lambda i: (i, 0)
    return pl.pallas_call(
        _mem_kv_kernel,
        grid=(R // tm,),
        in_specs=[pl.BlockSpec((tm, D), row), _resident((1, D)), _resident(w.shape)],
        out_specs=pl.BlockSpec((tm, w.shape[1]), row),
        out_shape=jax.ShapeDtypeStruct((R, w.shape[1]), bf16),
        compiler_params=_cparams("parallel"),
        name="mem_kv",
    )(mem2, g, w)


def _xattn_kernel(h_ref, hn_ref, kv_ref, wq_ref, wo_ref, g_ref, pw_ref, h2_ref, qp_ref, *, cfg):
    nh, hd, xw = cfg.xattn_heads, cfg.xattn_head_dim, cfg.xattn_width
    q = jnp.dot(hn_ref[...], wq_ref[...], preferred_element_type=f32).astype(bf16)
    outs = []
    for h in range(nh):
        qh = q[:, h * hd:(h + 1) * hd]
        kh = kv_ref[:, h * hd:(h + 1) * hd]
        vh = kv_ref[:, xw + h * hd:xw + (h + 1) * hd]
        s = lax.dot_general(qh, kh, (((1,), (1,)), ((), ())), preferred_element_type=f32) * (hd ** -0.5)
        p = jnp.exp(s - jnp.max(s, axis=1, keepdims=True))
        p = p / jnp.sum(p, axis=1, keepdims=True)
        outs.append(jnp.dot(p.astype(bf16), vh, preferred_element_type=f32).astype(bf16))
    o = jnp.concatenate(outs, axis=1)
    h2 = h_ref[...] + jnp.dot(o, wo_ref[...], preferred_element_type=f32)
    h2_ref[...] = h2
    hp = _rms(h2, g_ref[...]).astype(bf16)
    qp_ref[...] = jnp.dot(hp, pw_ref[...], preferred_element_type=f32).astype(qp_ref.dtype)


def _xattn(cfg, h1, hn, kv, wq, wo, g, pw):
    T, D = h1.shape
    tm, M = cfg.chunk, cfg.n_mem
    row = lambda i: (i, 0)
    return pl.pallas_call(
        functools.partial(_xattn_kernel, cfg=cfg),
        grid=(T // tm,),
        in_specs=[pl.BlockSpec((tm, D), row), pl.BlockSpec((tm, D), row),
                  pl.BlockSpec((M, kv.shape[1]), row),
                  _resident(wq.shape), _resident(wo.shape), _resident((1, D)), _resident(pw.shape)],
        out_specs=[pl.BlockSpec((tm, D), row), pl.BlockSpec((tm, pw.shape[1]), row)],
        out_shape=[jax.ShapeDtypeStruct((T, D), f32), jax.ShapeDtypeStruct((T, pw.shape[1]), bf16)],
        compiler_params=_cparams("parallel"),
        name="xattn",
    )(h1, hn, kv, wq, wo, g, pw)


def _top_rows(s, k, payload=None):
    n = s.shape[0]
    rows = lax.broadcasted_iota(i32, s.shape, 0)
    vals, picks = [], []
    for _ in range(k):
        m = jnp.max(s, axis=0, keepdims=True)
        idx = jnp.min(jnp.where(s == m, rows, n), axis=0, keepdims=True)
        sel = rows == idx
        vals.append(m)
        picks.append(idx if payload is None else jnp.max(jnp.where(sel, payload, -1), axis=0, keepdims=True))
        s = jnp.where(sel, NEG, s)
    return vals, picks


def _peer_route_kernel(qp_ref, k1_ref, k2_ref, e_ref, g_ref, *, cfg):
    K, nk, half = cfg.peer_topk, cfg.peer_keys, cfg.peer_qdim // 2
    tm = qp_ref.shape[0]
    dn = (((1,), (1,)), ((), ()))
    s1 = lax.dot_general(k1_ref[0], qp_ref[:, :half], dn, preferred_element_type=f32)
    s2 = lax.dot_general(k2_ref[0], qp_ref[:, half:], dn, preferred_element_type=f32)
    t1, i1 = _top_rows(s1, K)
    t2, i2 = _top_rows(s2, K)
    t1, i1 = jnp.concatenate(t1, axis=0), jnp.concatenate(i1, axis=0)
    t2, i2 = jnp.concatenate(t2, axis=0), jnp.concatenate(i2, axis=0)
    cand = jnp.concatenate([t1[0:1] + t2] + [t1[a:a + 1] + t2[:K // 2] for a in range(1, K)], axis=0)
    expert = jnp.concatenate([i1[0:1] * nk + i2] + [i1[a:a + 1] * nk + i2[:K // 2] for a in range(1, K)], axis=0)
    ts, es = _top_rows(cand, K, payload=expert)
    ts = jnp.concatenate(ts, axis=0)
    p = jnp.exp(ts - ts[0:1])
    g_ref[...] = p / jnp.sum(p, axis=0, keepdims=True)
    e_ref[...] = jnp.concatenate(es, axis=0)


def _peer_route(cfg, qp, k1, k2):
    T = qp.shape[0]
    tm, H, K, Q = cfg.route_tm, cfg.peer_heads, cfg.peer_topk, cfg.peer_qdim
    out = pl.BlockSpec((K, tm), lambda i, h: (h, i))
    return pl.pallas_call(
        functools.partial(_peer_route_kernel, cfg=cfg),
        grid=(T // tm, H),
        in_specs=[pl.BlockSpec((tm, Q), lambda i, h: (i, h)),
                  pl.BlockSpec((1,) + k1.shape[1:], lambda i, h: (h, 0, 0)),
                  pl.BlockSpec((1,) + k2.shape[1:], lambda i, h: (h, 0, 0))],
        out_specs=[out, out],
        out_shape=[jax.ShapeDtypeStruct((H * K, T), i32), jax.ShapeDtypeStruct((H * K, T), f32)],
        compiler_params=_cparams("parallel", "parallel"),
        name="peer_route",
    )(qp, k1, k2)


SC_LANES = 16
SC_ROWS = 32


def _sc_mesh():
    mesh = plsc.VectorSubcoreMesh(core_axis_name="core", subcore_axis_name="subcore")
    return mesh, mesh.num_cores * mesh.num_subcores


def _sc_worker(mesh):
    return lax.axis_index("core") * mesh.num_subcores + lax.axis_index("subcore")


def _sc_halves(words):
    return plsc.bitcast(words << 16, f32), plsc.bitcast(words & jnp.uint32(0xFFFF0000), f32)


def _sc_batches(tab_hbm, idx_v, bufs, sems, n_batches, compute):
    def gather(b, s):
        return pltpu.make_async_copy(tab_hbm.at[idx_v.at[pl.ds(b * SC_ROWS, SC_ROWS)]], bufs[s], sems.at[s])

    gather(0, 0).start()

    @pl.loop(0, n_batches, step=2)
    def _(b0):
        for s in range(2):
            b = b0 + s
            gather(b, s).wait()

            @pl.when(b + 1 < n_batches)
            def _():
                gather(b + 1, 1 - s).start()

            compute(b, bufs[s])


def _sc_peer_act(cfg, table_u, idx, hp):
    T, E = idx.shape
    W = table_u.shape[1]
    mesh, workers = _sc_mesh()
    tpw = T // workers
    bpt = E // SC_ROWS
    assert tpw * workers == T and bpt * SC_ROWS == E and (tpw * bpt) % 2 == 0
    RG = 4

    @functools.partial(
        pl.kernel, out_type=jax.ShapeDtypeStruct((T * E,), f32), mesh=mesh,
        scratch_types=[pltpu.VMEM((tpw * E,), i32), pltpu.VMEM((2 * W,), f32),
                       pltpu.VMEM((SC_ROWS, W), jnp.uint32), pltpu.VMEM((SC_ROWS, W), jnp.uint32),
                       pltpu.VMEM((E,), f32), pltpu.SemaphoreType.DMA((2,))],
        compiler_params=pltpu.CompilerParams(needs_layout_passes=False))
    def act_kernel(tab_hbm, idx_hbm, hp_hbm, act_hbm, idx_v, x_v, buf0, buf1, act_v, sems):
        t0 = _sc_worker(mesh) * tpw
        pltpu.sync_copy(idx_hbm.at[pl.ds(t0 * E, tpw * E)], idx_v)
        lane = lax.iota(i32, SC_LANES)

        def compute(b, buf):
            tl, bt = b // bpt, b % bpt

            @pl.when(bt == 0)
            def _():
                pltpu.sync_copy(hp_hbm.at[t0 + tl], x_v)

            for g in range(SC_ROWS // SC_LANES):
                def rows_body(i, vec):
                    r = g * SC_LANES + i * RG

                    def j_body(j, accs):
                        c = pl.multiple_of(j * SC_LANES, SC_LANES)
                        x_lo, x_hi = x_v[pl.ds(c, SC_LANES)], x_v[pl.ds(W + c, SC_LANES)]
                        out = []
                        for k in range(RG):
                            lo, hi = _sc_halves(buf[r + k, pl.ds(c, SC_LANES)])
                            out.append(accs[k] + lo * x_lo + hi * x_hi)
                        return tuple(out)

                    accs = lax.fori_loop(0, W // SC_LANES, j_body,
                                         tuple(jnp.zeros((SC_LANES,), f32) for _ in range(RG)))
                    for k in range(RG):
                        vec = jnp.where(lane == i * RG + k, jnp.sum(accs[k]), vec)
                    return vec

                vec = lax.fori_loop(0, SC_LANES // RG, rows_body, jnp.zeros((SC_LANES,), f32))
                act_v[pl.ds(pl.multiple_of(bt * SC_ROWS + g * SC_LANES, SC_LANES), SC_LANES)] = vec

            @pl.when(bt == bpt - 1)
            def _():
                pltpu.sync_copy(act_v, act_hbm.at[pl.ds((t0 + tl) * E, E)])

        _sc_batches(tab_hbm, idx_v, (buf0, buf1), sems, tpw * bpt, compute)

    return act_kernel(table_u, idx.reshape(-1), hp).reshape(T, E)


def _sc_peer_sum(cfg, table_v, idx, w):
    T, E = idx.shape
    W = table_v.shape[1]
    mesh, workers = _sc_mesh()
    tpw = T // workers
    bpt = E // SC_ROWS
    assert tpw * workers == T and bpt * SC_ROWS == E and (tpw * bpt) % 2 == 0
    NQ = 4
    qw = W // NQ

    @functools.partial(
        pl.kernel, out_type=jax.ShapeDtypeStruct((T, 2 * W), f32), mesh=mesh,
        scratch_types=[pltpu.VMEM((tpw * E,), i32), pltpu.VMEM((E,), f32), pltpu.VMEM((2 * W,), f32),
                       pltpu.VMEM((SC_ROWS, W), jnp.uint32), pltpu.VMEM((SC_ROWS, W), jnp.uint32),
                       pltpu.SemaphoreType.DMA((2,))],
        compiler_params=pltpu.CompilerParams(needs_layout_passes=False))
    def sum_kernel(tab_hbm, idx_hbm, w_hbm, y_hbm, idx_v, w_v, y_v, buf0, buf1, sems):
        t0 = _sc_worker(mesh) * tpw
        pltpu.sync_copy(idx_hbm.at[pl.ds(t0 * E, tpw * E)], idx_v)
        nq = qw // SC_LANES

        def compute(b, buf):
            tl, bt = b // bpt, b % bpt

            @pl.when(bt == 0)
            def _():
                pltpu.sync_copy(w_hbm.at[pl.ds((t0 + tl) * E, E)], w_v)

            for q in range(NQ):
                cols = [q * qw + n * SC_LANES for n in range(nq)]
                zero = jnp.zeros((SC_LANES,), f32)
                init = tuple(jnp.where(bt == 0, zero, y_v[pl.ds(c, SC_LANES)]) for c in cols) + \
                    tuple(jnp.where(bt == 0, zero, y_v[pl.ds(W + c, SC_LANES)]) for c in cols)

                def r_body(r, accs):
                    wr = plsc.load_gather(w_v, [jnp.full((SC_LANES,), bt * SC_ROWS + r, i32)])
                    lo_acc, hi_acc = list(accs[:nq]), list(accs[nq:])
                    for n, c in enumerate(cols):
                        lo, hi = _sc_halves(buf[r, pl.ds(c, SC_LANES)])
                        lo_acc[n] = lo_acc[n] + lo * wr
                        hi_acc[n] = hi_acc[n] + hi * wr
                    return tuple(lo_acc) + tuple(hi_acc)

                accs = lax.fori_loop(0, SC_ROWS, r_body, init)
                for n, c in enumerate(cols):
                    y_v[pl.ds(c, SC_LANES)] = accs[n]
                    y_v[pl.ds(W + c, SC_LANES)] = accs[nq + n]

            @pl.when(bt == bpt - 1)
            def _():
                pltpu.sync_copy(y_v, y_hbm.at[t0 + tl])

        _sc_batches(tab_hbm, idx_v, (buf0, buf1), sems, tpw * bpt, compute)

    return sum_kernel(table_v, idx.reshape(-1), w.reshape(-1))


def _pack_bf16_halves(a):
    h = a.shape[1] // 2
    b = lax.bitcast_convert_type(a.astype(bf16), jnp.uint16).astype(jnp.uint32)
    return b[:, :h] | (b[:, h:] << 16)


def _split_bf16(a):
    hi = a.astype(bf16)
    lo = (a - hi.astype(f32)).astype(bf16)
    return jnp.concatenate([hi, lo], axis=0)


def _fold_rows(a):
    n = a.shape[0] // 2
    return a[:n] + a[n:]


def _apply_group(h2, gates, words, sel, selt, gn, gf):
    R, D = h2.shape
    nw = D // 2
    nz = 2 * words.shape[0]
    lane = lax.broadcasted_iota(i32, (2 * R, nz), 1)
    rowi = lax.broadcasted_iota(i32, (2 * R, nz), 0)
    mine = ((lane // 2) % R == rowi % R) & (lane % 2 == rowi // R)
    zu = pltpu.bitcast(words[:, :nw], bf16)
    zv = pltpu.bitcast(words[:, nw:], bf16)
    x = _rms(h2, gn)
    xs = _split_bf16(jnp.concatenate([x[:, :nw], x[:, nw:]], axis=0))
    a = _fold_rows(lax.dot_general(xs, zu, (((1,), (1,)), ((), ())), preferred_element_type=f32))
    a = jnp.where(mine, a, 0.0)
    a = a[:R] + a[R:]
    act = _fold_rows(jnp.dot(_split_bf16(a), sel, preferred_element_type=f32))
    w = gates * _gelu(act)
    w_rows = _fold_rows(jnp.dot(_split_bf16(w), selt, preferred_element_type=f32))
    ws = _split_bf16(jnp.where(mine, jnp.concatenate([w_rows, w_rows], axis=0), 0.0))
    y2 = _fold_rows(jnp.dot(ws, zv, preferred_element_type=f32))
    y = jnp.concatenate([y2[:R], y2[R:]], axis=1)
    return _rms(h2 + y, gf)


def _peer_direct_kernel(idx_ref, idxn_ref, h_ref, gt_ref, tab_ref, sel_ref, selt_ref, gn_ref, gf_ref, o_ref,
                        buf_a, buf_b, sem_ref, *, cfg):
    E, R = cfg.slots, SUBLANES
    n_rows = E * R
    i = pl.program_id(0)

    def start_row(src_idx_ref, base, r, buf, s, priority=0):
        pltpu.async_copy(tab_ref.at[pl.ds(src_idx_ref[base + r], 1)], buf.at[pl.ds(r, 1)], sem_ref.at[s],
                         priority=priority)

    def wait_rows(buf, s):
        pltpu.make_async_copy(tab_ref.at[pl.ds(0, n_rows)], buf, sem_ref.at[s]).wait()

    def group(gi, buf):
        tok = slice(gi * R, (gi + 1) * R)
        o_ref[tok, :] = _apply_group(h_ref[tok, :], gt_ref[tok, :], buf[...], sel_ref[...], selt_ref[...],
                                     gn_ref[...], gf_ref[...])

    @pl.when(i == 0)
    def _():
        def body(r, carry):
            start_row(idx_ref, 0, r, buf_a, 0)
            return carry
        lax.fori_loop(0, n_rows, body, 0, unroll=8)

    wait_rows(buf_a, 0)
    for r in range(n_rows):
        start_row(idx_ref, n_rows, r, buf_b, 1, priority=r % 2)
    group(0, buf_a)
    wait_rows(buf_b, 1)
    for r in range(n_rows):
        start_row(idxn_ref, 0, r, buf_a, 0, priority=r % 2)
    group(1, buf_b)

    @pl.when(i == pl.num_programs(0) - 1)
    def _():
        wait_rows(buf_a, 0)


def _peer_direct(cfg, h2, gates, idx, table, gn, gf):
    T, D = h2.shape
    E, R = cfg.slots, SUBLANES
    n_rows = E * R
    nz = 2 * n_rows
    n = T // (2 * R)
    sel = (jnp.arange(nz)[:, None] // (2 * R) == jnp.arange(E)[None, :]).astype(bf16)
    row = lambda i: (i, 0)
    smem = functools.partial(pl.BlockSpec, (2 * n_rows,), memory_space=pltpu.SMEM)
    return pl.pallas_call(
        functools.partial(_peer_direct_kernel, cfg=cfg),
        grid=(n,),
        in_specs=[smem(lambda i: (i,)), smem(lambda i: (jnp.minimum(i + 1, n - 1),)),
                  pl.BlockSpec((2 * R, D), row), pl.BlockSpec((2 * R, E), row), pl.BlockSpec(memory_space=pl.ANY),
                  _resident((nz, E)), _resident((E, nz)), _resident((1, D)), _resident((1, D))],
        out_specs=pl.BlockSpec((2 * R, D), row),
        out_shape=jax.ShapeDtypeStruct((T, D), f32),
        scratch_shapes=[pltpu.VMEM((n_rows, D), table.dtype), pltpu.VMEM((n_rows, D), table.dtype),
                        pltpu.SemaphoreType.DMA((2,))],
        compiler_params=pltpu.CompilerParams(dimension_semantics=("arbitrary",), vmem_limit_bytes=VMEM_LIMIT,
                                             disable_bounds_checks=True),
        name="peer_direct",
    )(idx, idx, h2, gates, table, sel, sel.T, gn, gf)


def _rowwise(name, body, out_cols, *arrays, rows_per_step=256):
    T = max(a.shape[0] for a in arrays)
    tm = min(rows_per_step, T)
    assert T % tm == 0
    row = lambda i: (i, 0)

    def kern(*refs):
        refs[-1][...] = body(*(r[...] for r in refs[:-1]))

    return pl.pallas_call(
        kern, grid=(T // tm,),
        in_specs=[_resident(a.shape) if a.shape[0] == 1 else pl.BlockSpec((tm, a.shape[1]), row) for a in arrays],
        out_specs=pl.BlockSpec((tm, out_cols), row),
        out_shape=jax.ShapeDtypeStruct((T, out_cols), f32),
        compiler_params=_cparams("parallel"), name=name,
    )(*arrays)


def _peer_pre(cfg, h2, gn):
    return _rowwise("peer_pre", _rms, h2.shape[1], h2, gn)


def _peer_mid(cfg, gates, act):
    return _rowwise("peer_mid", lambda g, a: g * _gelu(a), gates.shape[1], gates, act)


def _peer_post(cfg, h2, y, gf):
    return _rowwise("peer_post", lambda h, yy, g: _rms(h + yy, g), h2.shape[1], h2, y, gf)


def _block(cfg, sc_fns, x, mem, positions, mix_norm_g, w_in, lam_q1, lam_k1, lam_q2, lam_k2, attn_head_g,
           ssm_a_re, ssm_a_im, ssm_log_dt, ssm_b_re, ssm_b_im, ssm_c_re, ssm_c_im, ssm_d, glu_w, glu_b,
           ssm_out_g, w_out, xattn_norm_g, mem_norm_g, xattn_wq, xattn_wkv, xattn_wo, ffn_norm_g,
           peer_wq, peer_k1, peer_k2, peer_u, peer_v, final_norm_g):
    B, S, D = x.shape
    T = B * S
    l = 0
    row = lambda a: a.reshape(1, -1)
    x2 = x.reshape(T, D)
    pos = positions.reshape(T, 1).astype(f32)
    freqs = cfg.rope_theta ** (-jnp.arange(0, cfg.rot_dim, 2, dtype=f32) / cfg.rot_dim)
    lane = jnp.arange(LANES) % cfg.diff_qkdim
    freq_row = jnp.where(lane < cfg.rot_dim, freqs[lane % (cfg.rot_dim // 2)], 0.0).reshape(1, LANES)

    lam_p = jnp.stack([lam_q1[l], lam_k1[l], lam_q2[l], lam_k2[l]])
    s5_params = _s5_params(cfg, ssm_a_re[l], ssm_a_im[l], ssm_log_dt[l], ssm_b_re[l], ssm_b_im[l],
                           ssm_c_re[l], ssm_c_im[l], ssm_d[l])
    ns, _, wn = s5_params[0].shape
    zero_state = jnp.zeros((ns, B, wn), f32)
    state = (zero_state, zero_state)
    w_in_b, glu_w_b, w_out_b = w_in[l].astype(bf16), glu_w[l].astype(bf16), w_out[l].astype(bf16)
    wq_b, wo_b, pw_b = xattn_wq[l].astype(bf16), xattn_wo[l].astype(bf16), peer_wq[l].astype(bf16)
    k1_b, k2_b = peer_k1[l].astype(bf16), peer_k2[l].astype(bf16)
    kv = _mem_kv(cfg, mem.reshape(B * cfg.n_mem, D), row(mem_norm_g[l]), xattn_wkv[l].astype(bf16))
    table_u, table_v = _pack_bf16_halves(peer_u[l]), _pack_bf16_halves(peer_v[l])
    table = jnp.concatenate([table_u, table_v], axis=1)
    E, R = cfg.slots, SUBLANES
    Lc = cfg.chunk
    Tc = B * Lc
    td = Tc * cfg.direct_eighths // 8
    sc_act, sc_sum = sc_fns

    def after(a, tokens):
        return lax.optimization_barrier((a,) + tuple(tokens))[0] if tokens else a

    def dense(c, qks, vs, state, tokens):
        qk, v, u = _in_proj(cfg, c, x2, pos, after(row(mix_norm_g[l]), tokens), freq_row, w_in_b)
        qks, vs = qks + [qk], vs + [v]
        att = _diff_attn(cfg, lam_p, qks, vs, row(attn_head_g[l]))
        y, state = _s5_scan(cfg, u.reshape(Tc, cfg.ssm_width), s5_params, state)
        ssm = _s5_glu(cfg, y, glu_w_b, row(glu_b[l]), row(ssm_out_g[l]))
        h1, hn = _out_proj(cfg, c, x2, att, ssm.reshape(Lc, B * cfg.ssm_width), w_out_b, row(xattn_norm_g[l]))
        h2, qp = _xattn(cfg, h1, hn, kv, wq_b, wo_b, row(ffn_norm_g[l]), pw_b)
        experts_t, gates_t = _peer_route(cfg, qp, k1_b, k2_b)
        return qks, vs, state, h2, experts_t, gates_t.T

    gn, gf = row(ffn_norm_g[l]), row(final_norm_g)
    qks, vs, outs, tokens, pending = [], [], [], [], None
    for c in range(cfg.n_chunks):
        qks, vs, state, h2, experts_t, gates = dense(c, qks, vs, state, tokens)
        parts, tokens = [], []
        if td < Tc:
            h2s = h2[td:]
            idx_s = experts_t.T[td:]
            hp = _peer_pre(cfg, h2s, gn)
        if pending is not None:
            p_h2, p_y, p_parts = pending
            p_parts.append(_peer_post(cfg, p_h2, p_y, after(gf, [hp] if td < Tc else [])))
            outs.append(jnp.concatenate(p_parts, axis=0).reshape(B, Lc, D))
            pending = None
        if td:
            idx_d = experts_t[:, :td].reshape(E, td // R, R).transpose(1, 0, 2).reshape(-1)
            parts.append(_peer_direct(cfg, h2[:td], gates[:td], idx_d, table,
                                      after(gn, [hp] if td < Tc else []), gf))
        if td < Tc:
            act = sc_act(table_u, idx_s, hp)
            w = _peer_mid(cfg, after(gates[td:], parts), act)
            tokens = [w]
            pending = (h2s, sc_sum(table_v, idx_s, w), parts)
        else:
            outs.append(jnp.concatenate(parts, axis=0).reshape(B, Lc, D))
    if pending is not None:
        p_h2, p_y, p_parts = pending
        p_parts.append(_peer_post(cfg, p_h2, p_y, gf))
        outs.append(jnp.concatenate(p_parts, axis=0).reshape(B, Lc, D))
    return jnp.concatenate(outs, axis=1)


def kernel(x, mem, positions, mix_norm_g, w_in, lam_q1, lam_k1, lam_q2, lam_k2, attn_head_g, ssm_a_re, ssm_a_im, ssm_log_dt, ssm_b_re, ssm_b_im, ssm_c_re, ssm_c_im, ssm_d, glu_w, glu_b, ssm_out_g, w_out, xattn_norm_g, mem_norm_g, xattn_wq, xattn_wkv, xattn_wo, ffn_norm_g, peer_wq, peer_k1, peer_k2, peer_u, peer_v, final_norm_g):
    cfg = Cfg()
    return _block(cfg, (functools.partial(_sc_peer_act, cfg), functools.partial(_sc_peer_sum, cfg)), x, mem, positions, mix_norm_g, w_in, lam_q1, lam_k1,
                  lam_q2, lam_k2, attn_head_g, ssm_a_re, ssm_a_im, ssm_log_dt, ssm_b_re, ssm_b_im, ssm_c_re,
                  ssm_c_im, ssm_d, glu_w, glu_b, ssm_out_g, w_out, xattn_norm_g, mem_norm_g, xattn_wq,
                  xattn_wkv, xattn_wo, ffn_norm_g, peer_wq, peer_k1, peer_k2, peer_u, peer_v, final_norm_g)
```

```python
import dataclasses
import functools
import math

import jax
import jax.numpy as jnp
from jax import lax
from jax.experimental import pallas as pl
from jax.experimental.pallas import tpu as pltpu
from jax.experimental.pallas import tpu_sc as plsc

f32 = jnp.float32
bf16 = jnp.bfloat16
i32 = jnp.int32

LANES = 128
SUBLANES = 8
VMEM_LIMIT = 56 * 1024 * 1024
NEG = -1e30
EPS = 1e-6


@dataclasses.dataclass(frozen=True)
class Cfg:
    d_model: int = 2048
    batch: int = 8
    seq: int = 2048
    n_mem: int = 256
    diff_heads: int = 8
    ssm_group: int = 16
    ssm_state: int = 64
    xattn_heads: int = 4
    xattn_head_dim: int = 128
    peer_heads: int = 8
    peer_keys: int = 128
    peer_qdim: int = 256
    peer_topk: int = 16
    rope_theta: float = 500000.0
    lam_init: float = 0.8 - 0.6 * math.exp(-0.3 * 0)
    chunk: int = 256
    scan_chunk: int = 128
    route_tm: int = 256
    direct_eighths: int = 2

    @property
    def n_chunks(self):
        return self.seq // self.chunk

    @property
    def attn_width(self):
        return self.d_model // 2

    @property
    def ssm_width(self):
        return self.d_model - self.attn_width

    @property
    def diff_vdim(self):
        return self.attn_width // self.diff_heads

    @property
    def diff_qkdim(self):
        return self.diff_vdim // 2

    @property
    def rot_dim(self):
        return self.diff_qkdim // 4

    @property
    def ssm_groups(self):
        return self.ssm_width // self.ssm_group

    @property
    def xattn_width(self):
        return self.xattn_heads * self.xattn_head_dim

    @property
    def tokens(self):
        return self.batch * self.seq

    @property
    def slots(self):
        return self.peer_heads * self.peer_topk


def _cparams(*sem):
    return pltpu.CompilerParams(dimension_semantics=sem, vmem_limit_bytes=VMEM_LIMIT)


def _resident(shape):
    nd = len(shape)
    return pl.BlockSpec(shape, lambda *_: (0,) * nd, pipeline_mode=pl.Buffered(1))


def _rms(x, g):
    return x * lax.rsqrt(jnp.mean(x * x, axis=-1, keepdims=True) + EPS) * g


def _gelu(x):
    return 0.5 * x * (1.0 + lax.erf(x * (2.0 ** -0.5)))


def _in_proj_kernel(x_ref, pos_ref, g_ref, freq_ref, w_ref, qk_ref, v_ref, u_ref, *, cfg):
    n_qk, n_v = 2 * cfg.attn_width, cfg.attn_width
    half = cfg.rot_dim // 2
    xn = _rms(x_ref[...], g_ref[...]).astype(bf16)
    ang = pos_ref[...] * freq_ref[...]
    cos, sin = jnp.cos(ang), jnp.sin(ang)
    lane = lax.broadcasted_iota(i32, (1, LANES), 1) % cfg.diff_qkdim
    sin_lo = jnp.where(lane < half, -sin, 0.0)
    sin_hi = jnp.where((lane >= half) & (lane < 2 * half), sin, 0.0)
    cw = 2 * LANES
    for c in range((n_qk + n_v + cfg.ssm_width) // cw):
        col = c * cw
        z = jnp.dot(xn, w_ref[:, col:col + cw], preferred_element_type=f32)
        if col < n_qk:
            for k in range(cw // LANES):
                zk = z[:, k * LANES:(k + 1) * LANES]
                zk = zk * cos + pltpu.roll(zk, LANES - half, 1) * sin_lo + pltpu.roll(zk, half, 1) * sin_hi
                qk_ref[:, col + k * LANES:col + (k + 1) * LANES] = zk.astype(bf16)
        elif col < n_qk + n_v:
            v_ref[:, col - n_qk:col - n_qk + cw] = z.astype(bf16)
        else:
            u_ref[:, col - n_qk - n_v:col - n_qk - n_v + cw] = z


def _chunk_rows(cfg, c):
    return lambda b: (b * cfg.n_chunks + c, 0)


def _in_proj(cfg, c, x2, pos, g, freq, w):
    D = x2.shape[1]
    B, tm = cfg.batch, cfg.chunk
    n_qk, n_v, n_u = 2 * cfg.attn_width, cfg.attn_width, cfg.ssm_width
    row = lambda b: (b, 0)
    return pl.pallas_call(
        functools.partial(_in_proj_kernel, cfg=cfg),
        grid=(B,),
        in_specs=[pl.BlockSpec((tm, D), _chunk_rows(cfg, c)), pl.BlockSpec((tm, 1), _chunk_rows(cfg, c)),
                  _resident((1, D)), _resident((1, LANES)), _resident(w.shape)],
        out_specs=[pl.BlockSpec((tm, n_qk), row), pl.BlockSpec((tm, n_v), row),
                   pl.BlockSpec((tm, n_u), lambda b: (0, b))],
        out_shape=[jax.ShapeDtypeStruct((B * tm, n_qk), bf16), jax.ShapeDtypeStruct((B * tm, n_v), bf16),
                   jax.ShapeDtypeStruct((tm, B * n_u), f32)],
        compiler_params=_cparams("parallel"),
        name="in_proj",
    )(x2, pos, g, freq, w)


def _diff_attn_kernel(lam_ref, q_ref, *rest, cfg, n_kv):
    k_refs, v_refs = rest[:n_kv], rest[n_kv:2 * n_kv]
    g_ref, o_ref = rest[2 * n_kv:]
    tq = cfg.chunk
    d = cfg.diff_qkdim
    lane = lax.broadcasted_iota(i32, (1, LANES), 1)
    q = q_ref[...].astype(f32) * (d ** -0.5)
    qs = (jnp.where(lane < d, q, 0.0).astype(bf16), jnp.where(lane >= d, q, 0.0).astype(bf16))
    causal = (lax.broadcasted_iota(i32, (tq, tq), 1) <= lax.broadcasted_iota(i32, (tq, tq), 0))
    tiles = lambda a: [a[:, t * LANES:(t + 1) * LANES] for t in range(tq // LANES)]

    def scores(c, j):
        s = lax.dot_general(qs[c], k_refs[j][...], (((1,), (1,)), ((), ())), preferred_element_type=f32)
        return jnp.where(causal, s, NEG) if j == n_kv - 1 else s

    outs = []
    for c in range(2):
        mm = jnp.full((tq, LANES), NEG, f32)
        for j in range(n_kv):
            for st in tiles(scores(c, j)):
                mm = jnp.maximum(mm, st)
        mb = jnp.broadcast_to(jnp.max(mm, axis=1, keepdims=True), (tq, LANES))
        ls = jnp.zeros((tq, LANES), f32)
        acc = jnp.zeros((tq, LANES), f32)
        for j in range(n_kv):
            ps = [jnp.exp(st - mb) for st in tiles(scores(c, j))]
            for p in ps:
                ls = ls + p
            acc = acc + jnp.dot(jnp.concatenate(ps, axis=1).astype(bf16), v_refs[j][...],
                                preferred_element_type=f32)
        outs.append(acc / jnp.sum(ls, axis=1, keepdims=True))

    lp = lam_ref[...]
    lam = (jnp.exp(jnp.sum(lp[0:1] * lp[1:2], axis=1, keepdims=True))
           - jnp.exp(jnp.sum(lp[2:3] * lp[3:4], axis=1, keepdims=True)) + cfg.lam_init)
    o = outs[0] - lam * outs[1]
    o_ref[...] = (_rms(o, g_ref[...]) * (1.0 - cfg.lam_init)).astype(o_ref.dtype)


def _diff_attn(cfg, lam_p, qks, vs, g):
    n_kv = len(qks)
    B, H, tq = cfg.batch, cfg.diff_heads, cfg.chunk
    head = lambda b, h: (b, h)
    key = lambda b, h: (b, H + h)
    return pl.pallas_call(
        functools.partial(_diff_attn_kernel, cfg=cfg, n_kv=n_kv),
        grid=(B, H),
        in_specs=([_resident(lam_p.shape), pl.BlockSpec((tq, LANES), head)]
                  + [pl.BlockSpec((tq, LANES), key)] * n_kv + [pl.BlockSpec((tq, LANES), head)] * n_kv
                  + [_resident((1, LANES))]),
        out_specs=pl.BlockSpec((tq, LANES), head),
        out_shape=jax.ShapeDtypeStruct((B * tq, cfg.attn_width), bf16),
        compiler_params=_cparams("parallel", "parallel"),
        name="diff_attn",
    )(lam_p, qks[-1], *qks, *vs, g)


def _s5_scan_kernel(u_ref, are_ref, aim_ref, ldt_ref, bre_ref, bim_ref, cre_ref, cim_ref, d_ref,
                    sin_re_ref, sin_im_ref, y_ref, sout_re_ref, sout_im_ref,
                    ab_ref, bbar_re_ref, bbar_im_ref, bu_re_ref, bu_im_ref, *, cfg):
    B, L = cfg.batch, cfg.scan_chunk
    t = pl.program_id(1)
    st_re_ref, st_im_ref = sout_re_ref.at[0], sout_im_ref.at[0]

    @pl.when(t == 0)
    def _():
        dt = jnp.exp(ldt_ref[0])
        lre, lim = are_ref[0], aim_ref[0]
        mag = jnp.exp(lre * dt)
        ang = lim * dt
        ab_re, ab_im = mag * jnp.cos(ang), mag * jnp.sin(ang)
        den = lre * lre + lim * lim
        f_re = ((ab_re - 1.0) * lre + ab_im * lim) / den
        f_im = (ab_im * lre - (ab_re - 1.0) * lim) / den
        ab_ref[0] = jnp.broadcast_to(ab_re, ab_ref.shape[1:])
        ab_ref[1] = jnp.broadcast_to(ab_im, ab_ref.shape[1:])
        br, bi = bre_ref[0], bim_ref[0]
        bbar_re_ref[...] = (f_re * br - f_im * bi).astype(bf16)
        bbar_im_ref[...] = (f_re * bi + f_im * br).astype(bf16)
        st_re_ref[...] = sin_re_ref[0]
        st_im_ref[...] = sin_im_ref[0]

    u = u_ref[...]
    ub = u.astype(bf16)
    bu_re_ref[...] = jnp.dot(ub, bbar_re_ref[...], preferred_element_type=f32)
    bu_im_ref[...] = jnp.dot(ub, bbar_im_ref[...], preferred_element_type=f32)
    a_re, a_im = ab_ref[0], ab_ref[1]

    def body(s, carry):
        xr, xi = carry
        rows = pl.ds(pl.multiple_of(s * B, B), B)
        nr = a_re * xr - a_im * xi + bu_re_ref[rows, :]
        ni = a_re * xi + a_im * xr + bu_im_ref[rows, :]
        bu_re_ref[rows, :] = nr
        bu_im_ref[rows, :] = ni
        return nr, ni

    xr, xi = lax.fori_loop(0, L, body, (st_re_ref[...], st_im_ref[...]), unroll=8)
    st_re_ref[...] = xr
    st_im_ref[...] = xi

    y_ref[...] = (jnp.dot(bu_re_ref[...].astype(bf16), cre_ref[0].astype(bf16), preferred_element_type=f32)
                  - jnp.dot(bu_im_ref[...].astype(bf16), cim_ref[0].astype(bf16), preferred_element_type=f32)
                  + d_ref[0] * u)


def _block_diag(w, blocks):
    G, r, c = w.shape
    w4 = w.reshape(G // blocks, blocks, r, c)
    eye = jnp.eye(blocks, dtype=w.dtype)
    out = w4[:, :, :, None, :] * eye[None, :, None, :, None]
    return out.reshape(G // blocks, blocks * r, blocks * c)


def _s5_params(cfg, a_re, a_im, log_dt, b_re, b_im, c_re, c_im, d_skip):
    G, P, Hc = cfg.ssm_groups, cfg.ssm_state, cfg.ssm_group
    gps = min(G, 2 * LANES // Hc)
    ns = G // gps
    wu, wn = gps * Hc, gps * P
    return (a_re.reshape(ns, 1, wn), a_im.reshape(ns, 1, wn), jnp.repeat(log_dt, P).reshape(ns, 1, wn),
            _block_diag(b_re.transpose(0, 2, 1), gps), _block_diag(b_im.transpose(0, 2, 1), gps),
            _block_diag(c_re.transpose(0, 2, 1), gps), _block_diag(c_im.transpose(0, 2, 1), gps),
            d_skip.reshape(ns, 1, wu))


def _s5_scan(cfg, u2, params, state):
    B, L = cfg.batch, cfg.scan_chunk
    rows, W = u2.shape
    ns, _, wn = params[0].shape
    wu = params[-1].shape[2]
    slab = lambda s, t: (s, 0, 0)
    st = pl.BlockSpec((1, B, wn), slab)
    st_shape = jax.ShapeDtypeStruct((ns, B, wn), f32)
    y, s_re, s_im = pl.pallas_call(
        functools.partial(_s5_scan_kernel, cfg=cfg),
        grid=(ns, rows // (L * B)),
        in_specs=[pl.BlockSpec((L * B, wu), lambda s, t: (t, s)),
                  pl.BlockSpec((1, 1, wn), slab), pl.BlockSpec((1, 1, wn), slab), pl.BlockSpec((1, 1, wn), slab),
                  pl.BlockSpec((1, wu, wn), slab), pl.BlockSpec((1, wu, wn), slab),
                  pl.BlockSpec((1, wn, wu), slab), pl.BlockSpec((1, wn, wu), slab),
                  pl.BlockSpec((1, 1, wu), slab), st, st],
        out_specs=[pl.BlockSpec((L * B, wu), lambda s, t: (t, s)), st, st],
        out_shape=[jax.ShapeDtypeStruct((rows, W), f32), st_shape, st_shape],
        scratch_shapes=[pltpu.VMEM((2, B, wn), f32),
                        pltpu.VMEM((wu, wn), bf16), pltpu.VMEM((wu, wn), bf16),
                        pltpu.VMEM((L * B, wn), f32), pltpu.VMEM((L * B, wn), f32)],
        compiler_params=_cparams("arbitrary", "arbitrary"),
        name="s5_scan",
    )(u2, *params, *state)
    return y, (s_re, s_im)


def _s5_glu_kernel(y_ref, w_ref, b_ref, g_ref, o_ref):
    g = _gelu(y_ref[...])
    z = jnp.dot(g.astype(bf16), w_ref[...], preferred_element_type=f32) + b_ref[...]
    g = g * jax.nn.sigmoid(z)
    o_ref[...] = _rms(g, g_ref[...]).astype(o_ref.dtype)


def _s5_glu(cfg, y2, w, b, g):
    T, W = y2.shape
    tm = cfg.chunk
    row = lambda i: (i, 0)
    return pl.pallas_call(
        _s5_glu_kernel,
        grid=(T // tm,),
        in_specs=[pl.BlockSpec((tm, W), row), _resident(w.shape), _resident((1, W)), _resident((1, W))],
        out_specs=pl.BlockSpec((tm, W), row),
        out_shape=jax.ShapeDtypeStruct((T, W), bf16),
        compiler_params=_cparams("parallel"),
        name="s5_glu",
    )(y2, w, b, g)


def _out_proj_kernel(x_ref, att_ref, ssm_ref, w_ref, g_ref, h_ref, hn_ref, *, cfg):
    aw = cfg.attn_width
    h = (x_ref[...]
         + jnp.dot(att_ref[...], w_ref[:aw, :], preferred_element_type=f32)
         + jnp.dot(ssm_ref[...], w_ref[aw:, :], preferred_element_type=f32))
    h_ref[...] = h
    hn_ref[...] = _rms(h, g_ref[...]).astype(hn_ref.dtype)


def _out_proj(cfg, c, x2, att, ssm, w, g):
    D = x2.shape[1]
    tm = cfg.chunk
    T = cfg.batch * tm
    row = lambda i: (i, 0)
    return pl.pallas_call(
        functools.partial(_out_proj_kernel, cfg=cfg),
        grid=(T // tm,),
        in_specs=[pl.BlockSpec((tm, D), _chunk_rows(cfg, c)), pl.BlockSpec((tm, cfg.attn_width), row),
                  pl.BlockSpec((tm, cfg.ssm_width), lambda b: (0, b)), _resident(w.shape), _resident((1, D))],
        out_specs=[pl.BlockSpec((tm, D), row), pl.BlockSpec((tm, D), row)],
        out_shape=[jax.ShapeDtypeStruct((T, D), f32), jax.ShapeDtypeStruct((T, D), bf16)],
        compiler_params=_cparams("parallel"),
        name="out_proj",
    )(x2, att, ssm, w, g)


def _mem_kv_kernel(m_ref, g_ref, w_ref, o_ref):
    mn = _rms(m_ref[...], g_ref[...]).astype(bf16)
    o_ref[...] = jnp.dot(mn, w_ref[...], preferred_element_type=f32).astype(o_ref.dtype)


def _mem_kv(cfg, mem2, g, w):
    R, D = mem2.shape
    tm = cfg.n_mem
    row = lambda i: (i, 0)
    return pl.pallas_call(
        _mem_kv_kernel,
        grid=(R // tm,),
        in_specs=[pl.BlockSpec((tm, D), row), _resident((1, D)), _resident(w.shape)],
        out_specs=pl.BlockSpec((tm, w.shape[1]), row),
        out_shape=jax.ShapeDtypeStruct((R, w.shape[1]), bf16),
        compiler_params=_cparams("parallel"),
        name="mem_kv",
    )(mem2, g, w)


def _xattn_kernel(h_ref, hn_ref, kv_ref, wq_ref, wo_ref, g_ref, pw_ref, h2_ref, qp_ref, *, cfg):
    nh, hd, xw = cfg.xattn_heads, cfg.xattn_head_dim, cfg.xattn_width
    q = jnp.dot(hn_ref[...], wq_ref[...], preferred_element_type=f32).astype(bf16)
    outs = []
    for h in range(nh):
        qh = q[:, h * hd:(h + 1) * hd]
        kh = kv_ref[:, h * hd:(h + 1) * hd]
        vh = kv_ref[:, xw + h * hd:xw + (h + 1) * hd]
        s = lax.dot_general(qh, kh, (((1,), (1,)), ((), ())), preferred_element_type=f32) * (hd ** -0.5)
        p = jnp.exp(s - jnp.max(s, axis=1, keepdims=True))
        p = p / jnp.sum(p, axis=1, keepdims=True)
        outs.append(jnp.dot(p.astype(bf16), vh, preferred_element_type=f32).astype(bf16))
    o = jnp.concatenate(outs, axis=1)
    h2 = h_ref[...] + jnp.dot(o, wo_ref[...], preferred_element_type=f32)
    h2_ref[...] = h2
    hp = _rms(h2, g_ref[...]).astype(bf16)
    qp_ref[...] = jnp.dot(hp, pw_ref[...], preferred_element_type=f32).astype(qp_ref.dtype)


def _xattn(cfg, h1, hn, kv, wq, wo, g, pw):
    T, D = h1.shape
    tm, M = cfg.chunk, cfg.n_mem
    row = lambda i: (i, 0)
    return pl.pallas_call(
        functools.partial(_xattn_kernel, cfg=cfg),
        grid=(T // tm,),
        in_specs=[pl.BlockSpec((tm, D), row), pl.BlockSpec((tm, D), row),
                  pl.BlockSpec((M, kv.shape[1]), row),
                  _resident(wq.shape), _resident(wo.shape), _resident((1, D)), _resident(pw.shape)],
        out_specs=[pl.BlockSpec((tm, D), row), pl.BlockSpec((tm, pw.shape[1]), row)],
        out_shape=[jax.ShapeDtypeStruct((T, D), f32), jax.ShapeDtypeStruct((T, pw.shape[1]), bf16)],
        compiler_params=_cparams("parallel"),
        name="xattn",
    )(h1, hn, kv, wq, wo, g, pw)


def _top_rows(s, k, payload=None):
    n = s.shape[0]
    rows = lax.broadcasted_iota(i32, s.shape, 0)
    vals, picks = [], []
    for _ in range(k):
        m = jnp.max(s, axis=0, keepdims=True)
        idx = jnp.min(jnp.where(s == m, rows, n), axis=0, keepdims=True)
        sel = rows == idx
        vals.append(m)
        picks.append(idx if payload is None else jnp.max(jnp.where(sel, payload, -1), axis=0, keepdims=True))
        s = jnp.where(sel, NEG, s)
    return vals, picks


def _peer_route_kernel(qp_ref, k1_ref, k2_ref, e_ref, g_ref, *, cfg):
    K, nk, half = cfg.peer_topk, cfg.peer_keys, cfg.peer_qdim // 2
    tm = qp_ref.shape[0]
    dn = (((1,), (1,)), ((), ()))
    s1 = lax.dot_general(k1_ref[0], qp_ref[:, :half], dn, preferred_element_type=f32)
    s2 = lax.dot_general(k2_ref[0], qp_ref[:, half:], dn, preferred_element_type=f32)
    t1, i1 = _top_rows(s1, K)
    t2, i2 = _top_rows(s2, K)
    t1, i1 = jnp.concatenate(t1, axis=0), jnp.concatenate(i1, axis=0)
    t2, i2 = jnp.concatenate(t2, axis=0), jnp.concatenate(i2, axis=0)
    cand = jnp.concatenate([t1[0:1] + t2] + [t1[a:a + 1] + t2[:K // 2] for a in range(1, K)], axis=0)
    expert = jnp.concatenate([i1[0:1] * nk + i2] + [i1[a:a + 1] * nk + i2[:K // 2] for a in range(1, K)], axis=0)
    ts, es = _top_rows(cand, K, payload=expert)
    ts = jnp.concatenate(ts, axis=0)
    p = jnp.exp(ts - ts[0:1])
    g_ref[...] = p / jnp.sum(p, axis=0, keepdims=True)
    e_ref[...] = jnp.concatenate(es, axis=0)


def _peer_route(cfg, qp, k1, k2):
    T = qp.shape[0]
    tm, H, K, Q = cfg.route_tm, cfg.peer_heads, cfg.peer_topk, cfg.peer_qdim
    out = pl.BlockSpec((K, tm), lambda i, h: (h, i))
    return pl.pallas_call(
        functools.partial(_peer_route_kernel, cfg=cfg),
        grid=(T // tm, H),
        in_specs=[pl.BlockSpec((tm, Q), lambda i, h: (i, h)),
                  pl.BlockSpec((1,) + k1.shape[1:], lambda i, h: (h, 0, 0)),
                  pl.BlockSpec((1,) + k2.shape[1:], lambda i, h: (h, 0, 0))],
        out_specs=[out, out],
        out_shape=[jax.ShapeDtypeStruct((H * K, T), i32), jax.ShapeDtypeStruct((H * K, T), f32)],
        compiler_params=_cparams("parallel", "parallel"),
        name="peer_route",
    )(qp, k1, k2)


SC_LANES = 16
SC_ROWS = 32


def _sc_mesh():
    mesh = plsc.VectorSubcoreMesh(core_axis_name="core", subcore_axis_name="subcore")
    return mesh, mesh.num_cores * mesh.num_subcores


def _sc_worker(mesh):
    return lax.axis_index("core") * mesh.num_subcores + lax.axis_index("subcore")


def _sc_halves(words):
    return plsc.bitcast(words << 16, f32), plsc.bitcast(words & jnp.uint32(0xFFFF0000), f32)


def _sc_batches(tab_hbm, idx_v, bufs, sems, n_batches, compute):
    def gather(b, s):
        return pltpu.make_async_copy(tab_hbm.at[idx_v.at[pl.ds(b * SC_ROWS, SC_ROWS)]], bufs[s], sems.at[s])

    gather(0, 0).start()

    @pl.loop(0, n_batches, step=2)
    def _(b0):
        for s in range(2):
            b = b0 + s
            gather(b, s).wait()

            @pl.when(b + 1 < n_batches)
            def _():
                gather(b + 1, 1 - s).start()

            compute(b, bufs[s])


def _sc_peer_act(cfg, table_u, idx, hp):
    T, E = idx.shape
    W = table_u.shape[1]
    mesh, workers = _sc_mesh()
    tpw = T // workers
    bpt = E // SC_ROWS
    assert tpw * workers == T and bpt * SC_ROWS == E and (tpw * bpt) % 2 == 0
    RG = 4

    @functools.partial(
        pl.kernel, out_type=jax.ShapeDtypeStruct((T * E,), f32), mesh=mesh,
        scratch_types=[pltpu.VMEM((tpw * E,), i32), pltpu.VMEM((2 * W,), f32),
                       pltpu.VMEM((SC_ROWS, W), jnp.uint32), pltpu.VMEM((SC_ROWS, W), jnp.uint32),
                       pltpu.VMEM((E,), f32), pltpu.SemaphoreType.DMA((2,))],
        compiler_params=pltpu.CompilerParams(needs_layout_passes=False))
    def act_kernel(tab_hbm, idx_hbm, hp_hbm, act_hbm, idx_v, x_v, buf0, buf1, act_v, sems):
        t0 = _sc_worker(mesh) * tpw
        pltpu.sync_copy(idx_hbm.at[pl.ds(t0 * E, tpw * E)], idx_v)
        lane = lax.iota(i32, SC_LANES)

        def compute(b, buf):
            tl, bt = b // bpt, b % bpt

            @pl.when(bt == 0)
            def _():
                pltpu.sync_copy(hp_hbm.at[t0 + tl], x_v)

            for g in range(SC_ROWS // SC_LANES):
                def rows_body(i, vec):
                    r = g * SC_LANES + i * RG

                    def j_body(j, accs):
                        c = pl.multiple_of(j * SC_LANES, SC_LANES)
                        x_lo, x_hi = x_v[pl.ds(c, SC_LANES)], x_v[pl.ds(W + c, SC_LANES)]
                        out = []
                        for k in range(RG):
                            lo, hi = _sc_halves(buf[r + k, pl.ds(c, SC_LANES)])
                            out.append(accs[k] + lo * x_lo + hi * x_hi)
                        return tuple(out)

                    accs = lax.fori_loop(0, W // SC_LANES, j_body,
                                         tuple(jnp.zeros((SC_LANES,), f32) for _ in range(RG)))
                    for k in range(RG):
                        vec = jnp.where(lane == i * RG + k, jnp.sum(accs[k]), vec)
                    return vec

                vec = lax.fori_loop(0, SC_LANES // RG, rows_body, jnp.zeros((SC_LANES,), f32))
                act_v[pl.ds(pl.multiple_of(bt * SC_ROWS + g * SC_LANES, SC_LANES), SC_LANES)] = vec

            @pl.when(bt == bpt - 1)
            def _():
                pltpu.sync_copy(act_v, act_hbm.at[pl.ds((t0 + tl) * E, E)])

        _sc_batches(tab_hbm, idx_v, (buf0, buf1), sems, tpw * bpt, compute)

    return act_kernel(table_u, idx.reshape(-1), hp).reshape(T, E)


def _sc_peer_sum(cfg, table_v, idx, w):
    T, E = idx.shape
    W = table_v.shape[1]
    mesh, workers = _sc_mesh()
    tpw = T // workers
    bpt = E // SC_ROWS
    assert tpw * workers == T and bpt * SC_ROWS == E and (tpw * bpt) % 2 == 0
    NQ = 4
    qw = W // NQ

    @functools.partial(
        pl.kernel, out_type=jax.ShapeDtypeStruct((T, 2 * W), f32), mesh=mesh,
        scratch_types=[pltpu.VMEM((tpw * E,), i32), pltpu.VMEM((E,), f32), pltpu.VMEM((2 * W,), f32),
                       pltpu.VMEM((SC_ROWS, W), jnp.uint32), pltpu.VMEM((SC_ROWS, W), jnp.uint32),
                       pltpu.SemaphoreType.DMA((2,))],
        compiler_params=pltpu.CompilerParams(needs_layout_passes=False))
    def sum_kernel(tab_hbm, idx_hbm, w_hbm, y_hbm, idx_v, w_v, y_v, buf0, buf1, sems):
        t0 = _sc_worker(mesh) * tpw
        pltpu.sync_copy(idx_hbm.at[pl.ds(t0 * E, tpw * E)], idx_v)
        nq = qw // SC_LANES

        def compute(b, buf):
            tl, bt = b // bpt, b % bpt

            @pl.when(bt == 0)
            def _():
                pltpu.sync_copy(w_hbm.at[pl.ds((t0 + tl) * E, E)], w_v)

            for q in range(NQ):
                cols = [q * qw + n * SC_LANES for n in range(nq)]
                zero = jnp.zeros((SC_LANES,), f32)
                init = tuple(jnp.where(bt == 0, zero, y_v[pl.ds(c, SC_LANES)]) for c in cols) + \
                    tuple(jnp.where(bt == 0, zero, y_v[pl.ds(W + c, SC_LANES)]) for c in cols)

                def r_body(r, accs):
                    wr = plsc.load_gather(w_v, [jnp.full((SC_LANES,), bt * SC_ROWS + r, i32)])
                    lo_acc, hi_acc = list(accs[:nq]), list(accs[nq:])
                    for n, c in enumerate(cols):
                        lo, hi = _sc_halves(buf[r, pl.ds(c, SC_LANES)])
                        lo_acc[n] = lo_acc[n] + lo * wr
                        hi_acc[n] = hi_acc[n] + hi * wr
                    return tuple(lo_acc) + tuple(hi_acc)

                accs = lax.fori_loop(0, SC_ROWS, r_body, init)
                for n, c in enumerate(cols):
                    y_v[pl.ds(c, SC_LANES)] = accs[n]
                    y_v[pl.ds(W + c, SC_LANES)] = accs[nq + n]

            @pl.when(bt == bpt - 1)
            def _():
                pltpu.sync_copy(y_v, y_hbm.at[t0 + tl])

        _sc_batches(tab_hbm, idx_v, (buf0, buf1), sems, tpw * bpt, compute)

    return sum_kernel(table_v, idx.reshape(-1), w.reshape(-1))


def _pack_bf16_halves(a):
    h = a.shape[1] // 2
    b = lax.bitcast_convert_type(a.astype(bf16), jnp.uint16).astype(jnp.uint32)
    return b[:, :h] | (b[:, h:] << 16)


def _split_bf16(a):
    hi = a.astype(bf16)
    lo = (a - hi.astype(f32)).astype(bf16)
    return jnp.concatenate([hi, lo], axis=0)


def _fold_rows(a):
    n = a.shape[0] // 2
    return a[:n] + a[n:]


def _apply_group(h2, gates, words, sel, selt, gn, gf):
    R, D = h2.shape
    nw = D // 2
    nz = 2 * words.shape[0]
    lane = lax.broadcasted_iota(i32, (2 * R, nz), 1)
    rowi = lax.broadcasted_iota(i32, (2 * R, nz), 0)
    mine = ((lane // 2) % R == rowi % R) & (lane % 2 == rowi // R)
    zu = pltpu.bitcast(words[:, :nw], bf16)
    zv = pltpu.bitcast(words[:, nw:], bf16)
    x = _rms(h2, gn)
    xs = _split_bf16(jnp.concatenate([x[:, :nw], x[:, nw:]], axis=0))
    a = _fold_rows(lax.dot_general(xs, zu, (((1,), (1,)), ((), ())), preferred_element_type=f32))
    a = jnp.where(mine, a, 0.0)
    a = a[:R] + a[R:]
    act = _fold_rows(jnp.dot(_split_bf16(a), sel, preferred_element_type=f32))
    w = gates * _gelu(act)
    w_rows = _fold_rows(jnp.dot(_split_bf16(w), selt, preferred_element_type=f32))
    ws = _split_bf16(jnp.where(mine, jnp.concatenate([w_rows, w_rows], axis=0), 0.0))
    y2 = _fold_rows(jnp.dot(ws, zv, preferred_element_type=f32))
    y = jnp.concatenate([y2[:R], y2[R:]], axis=1)
    return _rms(h2 + y, gf)


def _peer_direct_kernel(idx_ref, idxn_ref, h_ref, gt_ref, tab_ref, sel_ref, selt_ref, gn_ref, gf_ref, o_ref,
                        buf_a, buf_b, sem_ref, *, cfg):
    E, R = cfg.slots, SUBLANES
    n_rows = E * R
    i = pl.program_id(0)

    def start_row(src_idx_ref, base, r, buf, s, priority=0):
        pltpu.async_copy(tab_ref.at[pl.ds(src_idx_ref[base + r], 1)], buf.at[pl.ds(r, 1)], sem_ref.at[s],
                         priority=priority)

    def wait_rows(buf, s):
        pltpu.make_async_copy(tab_ref.at[pl.ds(0, n_rows)], buf, sem_ref.at[s]).wait()

    def group(gi, buf):
        tok = slice(gi * R, (gi + 1) * R)
        o_ref[tok, :] = _apply_group(h_ref[tok, :], gt_ref[tok, :], buf[...], sel_ref[...], selt_ref[...],
                                     gn_ref[...], gf_ref[...])

    @pl.when(i == 0)
    def _():
        def body(r, carry):
            start_row(idx_ref, 0, r, buf_a, 0)
            return carry
        lax.fori_loop(0, n_rows, body, 0, unroll=8)

    wait_rows(buf_a, 0)
    for r in range(n_rows):
        start_row(idx_ref, n_rows, r, buf_b, 1, priority=r % 2)
    group(0, buf_a)
    wait_rows(buf_b, 1)
    for r in range(n_rows):
        start_row(idxn_ref, 0, r, buf_a, 0, priority=r % 2)
    group(1, buf_b)

    @pl.when(i == pl.num_programs(0) - 1)
    def _():
        wait_rows(buf_a, 0)


def _peer_direct(cfg, h2, gates, idx, table, gn, gf):
    T, D = h2.shape
    E, R = cfg.slots, SUBLANES
    n_rows = E * R
    nz = 2 * n_rows
    n = T // (2 * R)
    sel = (jnp.arange(nz)[:, None] // (2 * R) == jnp.arange(E)[None, :]).astype(bf16)
    row = lambda i: (i, 0)
    smem = functools.partial(pl.BlockSpec, (2 * n_rows,), memory_space=pltpu.SMEM)
    return pl.pallas_call(
        functools.partial(_peer_direct_kernel, cfg=cfg),
        grid=(n,),
        in_specs=[smem(lambda i: (i,)), smem(lambda i: (jnp.minimum(i + 1, n - 1),)),
                  pl.BlockSpec((2 * R, D), row), pl.BlockSpec((2 * R, E), row), pl.BlockSpec(memory_space=pl.ANY),
                  _resident((nz, E)), _resident((E, nz)), _resident((1, D)), _resident((1, D))],
        out_specs=pl.BlockSpec((2 * R, D), row),
        out_shape=jax.ShapeDtypeStruct((T, D), f32),
        scratch_shapes=[pltpu.VMEM((n_rows, D), table.dtype), pltpu.VMEM((n_rows, D), table.dtype),
                        pltpu.SemaphoreType.DMA((2,))],
        compiler_params=pltpu.CompilerParams(dimension_semantics=("arbitrary",), vmem_limit_bytes=VMEM_LIMIT,
                                             disable_bounds_checks=True),
        name="peer_direct",
    )(idx, idx, h2, gates, table, sel, sel.T, gn, gf)


def _rowwise(name, body, out_cols, *arrays, rows_per_step=256):
    T = max(a.shape[0] for a in arrays)
    tm = min(rows_per_step, T)
    assert T % tm == 0
    row = lambda i: (i, 0)

    def kern(*refs):
        refs[-1][...] = body(*(r[...] for r in refs[:-1]))

    return pl.pallas_call(
        kern, grid=(T // tm,),
        in_specs=[_resident(a.shape) if a.shape[0] == 1 else pl.BlockSpec((tm, a.shape[1]), row) for a in arrays],
        out_specs=pl.BlockSpec((tm, out_cols), row),
        out_shape=jax.ShapeDtypeStruct((T, out_cols), f32),
        compiler_params=_cparams("parallel"), name=name,
    )(*arrays)


def _peer_pre(cfg, h2, gn):
    return _rowwise("peer_pre", _rms, h2.shape[1], h2, gn)


def _peer_mid(cfg, gates, act):
    return _rowwise("peer_mid", lambda g, a: g * _gelu(a), gates.shape[1], gates, act)


def _peer_post(cfg, h2, y, gf):
    return _rowwise("peer_post", lambda h, yy, g: _rms(h + yy, g), h2.shape[1], h2, y, gf)


def _block(cfg, sc_fns, x, mem, positions, mix_norm_g, w_in, lam_q1, lam_k1, lam_q2, lam_k2, attn_head_g,
           ssm_a_re, ssm_a_im, ssm_log_dt, ssm_b_re, ssm_b_im, ssm_c_re, ssm_c_im, ssm_d, glu_w, glu_b,
           ssm_out_g, w_out, xattn_norm_g, mem_norm_g, xattn_wq, xattn_wkv, xattn_wo, ffn_norm_g,
           peer_wq, peer_k1, peer_k2, peer_u, peer_v, final_norm_g):
    B, S, D = x.shape
    T = B * S
    l = 0
    row = lambda a: a.reshape(1, -1)
    x2 = x.reshape(T, D)
    pos = positions.reshape(T, 1).astype(f32)
    freqs = cfg.rope_theta ** (-jnp.arange(0, cfg.rot_dim, 2, dtype=f32) / cfg.rot_dim)
    lane = jnp.arange(LANES) % cfg.diff_qkdim
    freq_row = jnp.where(lane < cfg.rot_dim, freqs[lane % (cfg.rot_dim // 2)], 0.0).reshape(1, LANES)

    lam_p = jnp.stack([lam_q1[l], lam_k1[l], lam_q2[l], lam_k2[l]])
    s5_params = _s5_params(cfg, ssm_a_re[l], ssm_a_im[l], ssm_log_dt[l], ssm_b_re[l], ssm_b_im[l],
                           ssm_c_re[l], ssm_c_im[l], ssm_d[l])
    ns, _, wn = s5_params[0].shape
    zero_state = jnp.zeros((ns, B, wn), f32)
    state = (zero_state, zero_state)
    w_in_b, glu_w_b, w_out_b = w_in[l].astype(bf16), glu_w[l].astype(bf16), w_out[l].astype(bf16)
    wq_b, wo_b, pw_b = xattn_wq[l].astype(bf16), xattn_wo[l].astype(bf16), peer_wq[l].astype(bf16)
    k1_b, k2_b = peer_k1[l].astype(bf16), peer_k2[l].astype(bf16)
    kv = _mem_kv(cfg, mem.reshape(B * cfg.n_mem, D), row(mem_norm_g[l]), xattn_wkv[l].astype(bf16))
    table_u, table_v = _pack_bf16_halves(peer_u[l]), _pack_bf16_halves(peer_v[l])
    table = jnp.concatenate([table_u, table_v], axis=1)
    E, R = cfg.slots, SUBLANES
    Lc = cfg.chunk
    Tc = B * Lc
    td = Tc * cfg.direct_eighths // 8
    sc_act, sc_sum = sc_fns

    def after(a, tokens):
        return lax.optimization_barrier((a,) + tuple(tokens))[0] if tokens else a

    def dense(c, qks, vs, state, tokens):
        qk, v, u = _in_proj(cfg, c, x2, pos, after(row(mix_norm_g[l]), tokens), freq_row, w_in_b)
        qks, vs = qks + [qk], vs + [v]
        att = _diff_attn(cfg, lam_p, qks, vs, row(attn_head_g[l]))
        y, state = _s5_scan(cfg, u.reshape(Tc, cfg.ssm_width), s5_params, state)
        ssm = _s5_glu(cfg, y, glu_w_b, row(glu_b[l]), row(ssm_out_g[l]))
        h1, hn = _out_proj(cfg, c, x2, att, ssm.reshape(Lc, B * cfg.ssm_width), w_out_b, row(xattn_norm_g[l]))
        h2, qp = _xattn(cfg, h1, hn, kv, wq_b, wo_b, row(ffn_norm_g[l]), pw_b)
        experts_t, gates_t = _peer_route(cfg, qp, k1_b, k2_b)
        return qks, vs, state, h2, experts_t, gates_t.T

    gn, gf = row(ffn_norm_g[l]), row(final_norm_g)
    qks, vs, outs, tokens, pending = [], [], [], [], None
    for c in range(cfg.n_chunks):
        qks, vs, state, h2, experts_t, gates = dense(c, qks, vs, state, tokens)
        parts, tokens, before_direct = [], [], []
        if td < Tc:
            h2s = h2[td:]
            idx_s = experts_t.T[td:]
            hp = _peer_pre(cfg, h2s, gn)
            before_direct.append(hp)
        if pending is not None:
            p_h2, p_y, p_parts = pending
            p_parts.append(_peer_post(cfg, p_h2, p_y, after(gf, before_direct)))
            before_direct.append(p_parts[-1])
            outs.append(jnp.concatenate(p_parts, axis=0).reshape(B, Lc, D))
            pending = None
        if td:
            idx_d = experts_t[:, :td].reshape(E, td // R, R).transpose(1, 0, 2).reshape(-1)
            parts.append(_peer_direct(cfg, h2[:td], gates[:td], idx_d, table, after(gn, before_direct), gf))
        if td < Tc:
            act = sc_act(table_u, idx_s, hp)
            w = _peer_mid(cfg, after(gates[td:], parts), act)
            tokens = [w]
            pending = (h2s, sc_sum(table_v, idx_s, w), parts)
        else:
            outs.append(jnp.concatenate(parts, axis=0).reshape(B, Lc, D))
    if pending is not None:
        p_h2, p_y, p_parts = pending
        p_parts.append(_peer_post(cfg, p_h2, p_y, gf))
        outs.append(jnp.concatenate(p_parts, axis=0).reshape(B, Lc, D))
    return jnp.concatenate(outs, axis=1)


def kernel(x, mem, positions, mix_norm_g, w_in, lam_q1, lam_k1, lam_q2, lam_k2, attn_head_g, ssm_a_re, ssm_a_im, ssm_log_dt, ssm_b_re, ssm_b_im, ssm_c_re, ssm_c_im, ssm_d, glu_w, glu_b, ssm_out_g, w_out, xattn_norm_g, mem_norm_g, xattn_wq, xattn_wkv, xattn_wo, ffn_norm_g, peer_wq, peer_k1, peer_k2, peer_u, peer_v, final_norm_g):
    cfg = Cfg()
    return _block(cfg, (functools.partial(_sc_peer_act, cfg), functools.partial(_sc_peer_sum, cfg)), x, mem, positions, mix_norm_g, w_in, lam_q1, lam_k1,
                  lam_q2, lam_k2, attn_head_g, ssm_a_re, ssm_a_im, ssm_log_dt, ssm_b_re, ssm_b_im, ssm_c_re,
                  ssm_c_im, ssm_d, glu_w, glu_b, ssm_out_g, w_out, xattn_norm_g, mem_norm_g, xattn_wq,
                  xattn_wkv, xattn_wo, ffn_norm_g, peer_wq, peer_k1, peer_k2, peer_u, peer_v, final_norm_g)
```

```python
import dataclasses
import functools
import math

import jax
import jax.numpy as jnp
from jax import lax
from jax.experimental import pallas as pl
from jax.experimental.pallas import tpu as pltpu
from jax.experimental.pallas import tpu_sc as plsc

f32 = jnp.float32
bf16 = jnp.bfloat16
i32 = jnp.int32

LANES = 128
SUBLANES = 8
VMEM_LIMIT = 56 * 1024 * 1024
NEG = -1e30
EPS = 1e-6


@dataclasses.dataclass(frozen=True)
class Cfg:
    d_model: int = 2048
    batch: int = 8
    seq: int = 2048
    n_mem: int = 256
    diff_heads: int = 8
    ssm_group: int = 16
    ssm_state: int = 64
    xattn_heads: int = 4
    xattn_head_dim: int = 128
    peer_heads: int = 8
    peer_keys: int = 128
    peer_qdim: int = 256
    peer_topk: int = 16
    rope_theta: float = 500000.0
    lam_init: float = 0.8 - 0.6 * math.exp(-0.3 * 0)
    chunk: int = 256
    scan_chunk: int = 128
    route_tm: int = 256
    direct_sixteenths: int = 5

    @property
    def n_chunks(self):
        return self.seq // self.chunk

    @property
    def attn_width(self):
        return self.d_model // 2

    @property
    def ssm_width(self):
        return self.d_model - self.attn_width

    @property
    def diff_vdim(self):
        return self.attn_width // self.diff_heads

    @property
    def diff_qkdim(self):
        return self.diff_vdim // 2

    @property
    def rot_dim(self):
        return self.diff_qkdim // 4

    @property
    def ssm_groups(self):
        return self.ssm_width // self.ssm_group

    @property
    def xattn_width(self):
        return self.xattn_heads * self.xattn_head_dim

    @property
    def tokens(self):
        return self.batch * self.seq

    @property
    def slots(self):
        return self.peer_heads * self.peer_topk


def _cparams(*sem):
    return pltpu.CompilerParams(dimension_semantics=sem, vmem_limit_bytes=VMEM_LIMIT)


def _resident(shape):
    nd = len(shape)
    return pl.BlockSpec(shape, lambda *_: (0,) * nd, pipeline_mode=pl.Buffered(1))


def _rms(x, g):
    return x * lax.rsqrt(jnp.mean(x * x, axis=-1, keepdims=True) + EPS) * g


def _gelu(x):
    return 0.5 * x * (1.0 + lax.erf(x * (2.0 ** -0.5)))


def _in_proj_kernel(x_ref, pos_ref, g_ref, freq_ref, w_ref, qk_ref, v_ref, u_ref, *, cfg):
    n_qk, n_v = 2 * cfg.attn_width, cfg.attn_width
    half = cfg.rot_dim // 2
    xn = _rms(x_ref[...], g_ref[...]).astype(bf16)
    ang = pos_ref[...] * freq_ref[...]
    cos, sin = jnp.cos(ang), jnp.sin(ang)
    lane = lax.broadcasted_iota(i32, (1, LANES), 1) % cfg.diff_qkdim
    sin_lo = jnp.where(lane < half, -sin, 0.0)
    sin_hi = jnp.where((lane >= half) & (lane < 2 * half), sin, 0.0)
    cw = 2 * LANES
    for c in range((n_qk + n_v + cfg.ssm_width) // cw):
        col = c * cw
        z = jnp.dot(xn, w_ref[:, col:col + cw], preferred_element_type=f32)
        if col < n_qk:
            for k in range(cw // LANES):
                zk = z[:, k * LANES:(k + 1) * LANES]
                zk = zk * cos + pltpu.roll(zk, LANES - half, 1) * sin_lo + pltpu.roll(zk, half, 1) * sin_hi
                qk_ref[:, col + k * LANES:col + (k + 1) * LANES] = zk.astype(bf16)
        elif col < n_qk + n_v:
            v_ref[:, col - n_qk:col - n_qk + cw] = z.astype(bf16)
        else:
            u_ref[:, col - n_qk - n_v:col - n_qk - n_v + cw] = z


def _chunk_rows(cfg, c):
    return lambda b: (b * cfg.n_chunks + c, 0)


def _in_proj(cfg, c, x2, pos, g, freq, w):
    D = x2.shape[1]
    B, tm = cfg.batch, cfg.chunk
    n_qk, n_v, n_u = 2 * cfg.attn_width, cfg.attn_width, cfg.ssm_width
    row = lambda b: (b, 0)
    return pl.pallas_call(
        functools.partial(_in_proj_kernel, cfg=cfg),
        grid=(B,),
        in_specs=[pl.BlockSpec((tm, D), _chunk_rows(cfg, c)), pl.BlockSpec((tm, 1), _chunk_rows(cfg, c)),
                  _resident((1, D)), _resident((1, LANES)), _resident(w.shape)],
        out_specs=[pl.BlockSpec((tm, n_qk), row), pl.BlockSpec((tm, n_v), row),
                   pl.BlockSpec((tm, n_u), lambda b: (0, b))],
        out_shape=[jax.ShapeDtypeStruct((B * tm, n_qk), bf16), jax.ShapeDtypeStruct((B * tm, n_v), bf16),
                   jax.ShapeDtypeStruct((tm, B * n_u), f32)],
        compiler_params=_cparams("parallel"),
        name="in_proj",
    )(x2, pos, g, freq, w)


def _diff_attn_kernel(lam_ref, q_ref, *rest, cfg, n_kv):
    k_refs, v_refs = rest[:n_kv], rest[n_kv:2 * n_kv]
    g_ref, o_ref = rest[2 * n_kv:]
    tq = cfg.chunk
    d = cfg.diff_qkdim
    lane = lax.broadcasted_iota(i32, (1, LANES), 1)
    q = q_ref[...].astype(f32) * (d ** -0.5)
    qs = (jnp.where(lane < d, q, 0.0).astype(bf16), jnp.where(lane >= d, q, 0.0).astype(bf16))
    causal = (lax.broadcasted_iota(i32, (tq, tq), 1) <= lax.broadcasted_iota(i32, (tq, tq), 0))
    tiles = lambda a: [a[:, t * LANES:(t + 1) * LANES] for t in range(tq // LANES)]

    def scores(c, j):
        s = lax.dot_general(qs[c], k_refs[j][...], (((1,), (1,)), ((), ())), preferred_element_type=f32)
        return jnp.where(causal, s, NEG) if j == n_kv - 1 else s

    outs = []
    for c in range(2):
        mm = jnp.full((tq, LANES), NEG, f32)
        for j in range(n_kv):
            for st in tiles(scores(c, j)):
                mm = jnp.maximum(mm, st)
        mb = jnp.broadcast_to(jnp.max(mm, axis=1, keepdims=True), (tq, LANES))
        ls = jnp.zeros((tq, LANES), f32)
        acc = jnp.zeros((tq, LANES), f32)
        for j in range(n_kv):
            ps = [jnp.exp(st - mb) for st in tiles(scores(c, j))]
            for p in ps:
                ls = ls + p
            acc = acc + jnp.dot(jnp.concatenate(ps, axis=1).astype(bf16), v_refs[j][...],
                                preferred_element_type=f32)
        outs.append(acc / jnp.sum(ls, axis=1, keepdims=True))

    lp = lam_ref[...]
    lam = (jnp.exp(jnp.sum(lp[0:1] * lp[1:2], axis=1, keepdims=True))
           - jnp.exp(jnp.sum(lp[2:3] * lp[3:4], axis=1, keepdims=True)) + cfg.lam_init)
    o = outs[0] - lam * outs[1]
    o_ref[...] = (_rms(o, g_ref[...]) * (1.0 - cfg.lam_init)).astype(o_ref.dtype)


def _diff_attn(cfg, lam_p, qks, vs, g):
    n_kv = len(qks)
    B, H, tq = cfg.batch, cfg.diff_heads, cfg.chunk
    head = lambda b, h: (b, h)
    key = lambda b, h: (b, H + h)
    return pl.pallas_call(
        functools.partial(_diff_attn_kernel, cfg=cfg, n_kv=n_kv),
        grid=(B, H),
        in_specs=([_resident(lam_p.shape), pl.BlockSpec((tq, LANES), head)]
                  + [pl.BlockSpec((tq, LANES), key)] * n_kv + [pl.BlockSpec((tq, LANES), head)] * n_kv
                  + [_resident((1, LANES))]),
        out_specs=pl.BlockSpec((tq, LANES), head),
        out_shape=jax.ShapeDtypeStruct((B * tq, cfg.attn_width), bf16),
        compiler_params=_cparams("parallel", "parallel"),
        name="diff_attn",
    )(lam_p, qks[-1], *qks, *vs, g)


def _s5_scan_kernel(u_ref, are_ref, aim_ref, ldt_ref, bre_ref, bim_ref, cre_ref, cim_ref, d_ref,
                    sin_re_ref, sin_im_ref, y_ref, sout_re_ref, sout_im_ref,
                    ab_ref, bbar_re_ref, bbar_im_ref, bu_re_ref, bu_im_ref, *, cfg):
    B, L = cfg.batch, cfg.scan_chunk
    t = pl.program_id(1)
    st_re_ref, st_im_ref = sout_re_ref.at[0], sout_im_ref.at[0]

    @pl.when(t == 0)
    def _():
        dt = jnp.exp(ldt_ref[0])
        lre, lim = are_ref[0], aim_ref[0]
        mag = jnp.exp(lre * dt)
        ang = lim * dt
        ab_re, ab_im = mag * jnp.cos(ang), mag * jnp.sin(ang)
        den = lre * lre + lim * lim
        f_re = ((ab_re - 1.0) * lre + ab_im * lim) / den
        f_im = (ab_im * lre - (ab_re - 1.0) * lim) / den
        ab_ref[0] = jnp.broadcast_to(ab_re, ab_ref.shape[1:])
        ab_ref[1] = jnp.broadcast_to(ab_im, ab_ref.shape[1:])
        br, bi = bre_ref[0], bim_ref[0]
        bbar_re_ref[...] = (f_re * br - f_im * bi).astype(bf16)
        bbar_im_ref[...] = (f_re * bi + f_im * br).astype(bf16)
        st_re_ref[...] = sin_re_ref[0]
        st_im_ref[...] = sin_im_ref[0]

    u = u_ref[...]
    ub = u.astype(bf16)
    bu_re_ref[...] = jnp.dot(ub, bbar_re_ref[...], preferred_element_type=f32)
    bu_im_ref[...] = jnp.dot(ub, bbar_im_ref[...], preferred_element_type=f32)
    a_re, a_im = ab_ref[0], ab_ref[1]

    def body(s, carry):
        xr, xi = carry
        rows = pl.ds(pl.multiple_of(s * B, B), B)
        nr = a_re * xr - a_im * xi + bu_re_ref[rows, :]
        ni = a_re * xi + a_im * xr + bu_im_ref[rows, :]
        bu_re_ref[rows, :] = nr
        bu_im_ref[rows, :] = ni
        return nr, ni

    xr, xi = lax.fori_loop(0, L, body, (st_re_ref[...], st_im_ref[...]), unroll=8)
    st_re_ref[...] = xr
    st_im_ref[...] = xi

    y_ref[...] = (jnp.dot(bu_re_ref[...].astype(bf16), cre_ref[0].astype(bf16), preferred_element_type=f32)
                  - jnp.dot(bu_im_ref[...].astype(bf16), cim_ref[0].astype(bf16), preferred_element_type=f32)
                  + d_ref[0] * u)


def _block_diag(w, blocks):
    G, r, c = w.shape
    w4 = w.reshape(G // blocks, blocks, r, c)
    eye = jnp.eye(blocks, dtype=w.dtype)
    out = w4[:, :, :, None, :] * eye[None, :, None, :, None]
    return out.reshape(G // blocks, blocks * r, blocks * c)


def _s5_params(cfg, a_re, a_im, log_dt, b_re, b_im, c_re, c_im, d_skip):
    G, P, Hc = cfg.ssm_groups, cfg.ssm_state, cfg.ssm_group
    gps = min(G, 2 * LANES // Hc)
    ns = G // gps
    wu, wn = gps * Hc, gps * P
    return (a_re.reshape(ns, 1, wn), a_im.reshape(ns, 1, wn), jnp.repeat(log_dt, P).reshape(ns, 1, wn),
            _block_diag(b_re.transpose(0, 2, 1), gps), _block_diag(b_im.transpose(0, 2, 1), gps),
            _block_diag(c_re.transpose(0, 2, 1), gps), _block_diag(c_im.transpose(0, 2, 1), gps),
            d_skip.reshape(ns, 1, wu))


def _s5_scan(cfg, u2, params, state):
    B, L = cfg.batch, cfg.scan_chunk
    rows, W = u2.shape
    ns, _, wn = params[0].shape
    wu = params[-1].shape[2]
    slab = lambda s, t: (s, 0, 0)
    st = pl.BlockSpec((1, B, wn), slab)
    st_shape = jax.ShapeDtypeStruct((ns, B, wn), f32)
    y, s_re, s_im = pl.pallas_call(
        functools.partial(_s5_scan_kernel, cfg=cfg),
        grid=(ns, rows // (L * B)),
        in_specs=[pl.BlockSpec((L * B, wu), lambda s, t: (t, s)),
                  pl.BlockSpec((1, 1, wn), slab), pl.BlockSpec((1, 1, wn), slab), pl.BlockSpec((1, 1, wn), slab),
                  pl.BlockSpec((1, wu, wn), slab), pl.BlockSpec((1, wu, wn), slab),
                  pl.BlockSpec((1, wn, wu), slab), pl.BlockSpec((1, wn, wu), slab),
                  pl.BlockSpec((1, 1, wu), slab), st, st],
        out_specs=[pl.BlockSpec((L * B, wu), lambda s, t: (t, s)), st, st],
        out_shape=[jax.ShapeDtypeStruct((rows, W), f32), st_shape, st_shape],
        scratch_shapes=[pltpu.VMEM((2, B, wn), f32),
                        pltpu.VMEM((wu, wn), bf16), pltpu.VMEM((wu, wn), bf16),
                        pltpu.VMEM((L * B, wn), f32), pltpu.VMEM((L * B, wn), f32)],
        compiler_params=_cparams("arbitrary", "arbitrary"),
        name="s5_scan",
    )(u2, *params, *state)
    return y, (s_re, s_im)


def _s5_glu_kernel(y_ref, w_ref, b_ref, g_ref, o_ref):
    g = _gelu(y_ref[...])
    z = jnp.dot(g.astype(bf16), w_ref[...], preferred_element_type=f32) + b_ref[...]
    g = g * jax.nn.sigmoid(z)
    o_ref[...] = _rms(g, g_ref[...]).astype(o_ref.dtype)


def _s5_glu(cfg, y2, w, b, g):
    T, W = y2.shape
    tm = cfg.chunk
    row = lambda i: (i, 0)
    return pl.pallas_call(
        _s5_glu_kernel,
        grid=(T // tm,),
        in_specs=[pl.BlockSpec((tm, W), row), _resident(w.shape), _resident((1, W)), _resident((1, W))],
        out_specs=pl.BlockSpec((tm, W), row),
        out_shape=jax.ShapeDtypeStruct((T, W), bf16),
        compiler_params=_cparams("parallel"),
        name="s5_glu",
    )(y2, w, b, g)


def _out_proj_kernel(x_ref, att_ref, ssm_ref, w_ref, g_ref, h_ref, hn_ref, *, cfg):
    aw = cfg.attn_width
    h = (x_ref[...]
         + jnp.dot(att_ref[...], w_ref[:aw, :], preferred_element_type=f32)
         + jnp.dot(ssm_ref[...], w_ref[aw:, :], preferred_element_type=f32))
    h_ref[...] = h
    hn_ref[...] = _rms(h, g_ref[...]).astype(hn_ref.dtype)


def _out_proj(cfg, c, x2, att, ssm, w, g):
    D = x2.shape[1]
    tm = cfg.chunk
    T = cfg.batch * tm
    row = lambda i: (i, 0)
    return pl.pallas_call(
        functools.partial(_out_proj_kernel, cfg=cfg),
        grid=(T // tm,),
        in_specs=[pl.BlockSpec((tm, D), _chunk_rows(cfg, c)), pl.BlockSpec((tm, cfg.attn_width), row),
                  pl.BlockSpec((tm, cfg.ssm_width), lambda b: (0, b)), _resident(w.shape), _resident((1, D))],
        out_specs=[pl.BlockSpec((tm, D), row), pl.BlockSpec((tm, D), row)],
        out_shape=[jax.ShapeDtypeStruct((T, D), f32), jax.ShapeDtypeStruct((T, D), bf16)],
        compiler_params=_cparams("parallel"),
        name="out_proj",
    )(x2, att, ssm, w, g)


def _mem_kv_kernel(m_ref, g_ref, w_ref, o_ref):
    mn = _rms(m_ref[...], g_ref[...]).astype(bf16)
    o_ref[...] = jnp.dot(mn, w_ref[...], preferred_element_type=f32).astype(o_ref.dtype)


def _mem_kv(cfg, mem2, g, w):
    R, D = mem2.shape
    tm = cfg.n_mem
    row = lambda i: (i, 0)
    return pl.pallas_call(
        _mem_kv_kernel,
        grid=(R // tm,),
        in_specs=[pl.BlockSpec((tm, D), row), _resident((1, D)), _resident(w.shape)],
        out_specs=pl.BlockSpec((tm, w.shape[1]), row),
        out_shape=jax.ShapeDtypeStruct((R, w.shape[1]), bf16),
        compiler_params=_cparams("parallel"),
        name="mem_kv",
    )(mem2, g, w)


def _xattn_kernel(h_ref, hn_ref, kv_ref, wq_ref, wo_ref, g_ref, pw_ref, h2_ref, qp_ref, *, cfg):
    nh, hd, xw = cfg.xattn_heads, cfg.xattn_head_dim, cfg.xattn_width
    q = jnp.dot(hn_ref[...], wq_ref[...], preferred_element_type=f32).astype(bf16)
    outs = []
    for h in range(nh):
        qh = q[:, h * hd:(h + 1) * hd]
        kh = kv_ref[:, h * hd:(h + 1) * hd]
        vh = kv_ref[:, xw + h * hd:xw + (h + 1) * hd]
        s = lax.dot_general(qh, kh, (((1,), (1,)), ((), ())), preferred_element_type=f32) * (hd ** -0.5)
        p = jnp.exp(s - jnp.max(s, axis=1, keepdims=True))
        p = p / jnp.sum(p, axis=1, keepdims=True)
        outs.append(jnp.dot(p.astype(bf16), vh, preferred_element_type=f32).astype(bf16))
    o = jnp.concatenate(outs, axis=1)
    h2 = h_ref[...] + jnp.dot(o, wo_ref[...], preferred_element_type=f32)
    h2_ref[...] = h2
    hp = _rms(h2, g_ref[...]).astype(bf16)
    qp_ref[...] = jnp.dot(hp, pw_ref[...], preferred_element_type=f32).astype(qp_ref.dtype)


def _xattn(cfg, h1, hn, kv, wq, wo, g, pw):
    T, D = h1.shape
    tm, M = cfg.chunk, cfg.n_mem
    row = lambda i: (i, 0)
    return pl.pallas_call(
        functools.partial(_xattn_kernel, cfg=cfg),
        grid=(T // tm,),
        in_specs=[pl.BlockSpec((tm, D), row), pl.BlockSpec((tm, D), row),
                  pl.BlockSpec((M, kv.shape[1]), row),
                  _resident(wq.shape), _resident(wo.shape), _resident((1, D)), _resident(pw.shape)],
        out_specs=[pl.BlockSpec((tm, D), row), pl.BlockSpec((tm, pw.shape[1]), row)],
        out_shape=[jax.ShapeDtypeStruct((T, D), f32), jax.ShapeDtypeStruct((T, pw.shape[1]), bf16)],
        compiler_params=_cparams("parallel"),
        name="xattn",
    )(h1, hn, kv, wq, wo, g, pw)


def _top_rows(s, k, payload=None):
    n = s.shape[0]
    rows = lax.broadcasted_iota(i32, s.shape, 0)
    vals, picks = [], []
    for _ in range(k):
        m = jnp.max(s, axis=0, keepdims=True)
        idx = jnp.min(jnp.where(s == m, rows, n), axis=0, keepdims=True)
        sel = rows == idx
        vals.append(m)
        picks.append(idx if payload is None else jnp.max(jnp.where(sel, payload, -1), axis=0, keepdims=True))
        s = jnp.where(sel, NEG, s)
    return vals, picks


def _peer_route_kernel(qp_ref, k1_ref, k2_ref, e_ref, g_ref, *, cfg):
    K, nk, half = cfg.peer_topk, cfg.peer_keys, cfg.peer_qdim // 2
    tm = qp_ref.shape[0]
    dn = (((1,), (1,)), ((), ()))
    s1 = lax.dot_general(k1_ref[0], qp_ref[:, :half], dn, preferred_element_type=f32)
    s2 = lax.dot_general(k2_ref[0], qp_ref[:, half:], dn, preferred_element_type=f32)
    t1, i1 = _top_rows(s1, K)
    t2, i2 = _top_rows(s2, K)
    t1, i1 = jnp.concatenate(t1, axis=0), jnp.concatenate(i1, axis=0)
    t2, i2 = jnp.concatenate(t2, axis=0), jnp.concatenate(i2, axis=0)
    cand = jnp.concatenate([t1[0:1] + t2] + [t1[a:a + 1] + t2[:K // 2] for a in range(1, K)], axis=0)
    expert = jnp.concatenate([i1[0:1] * nk + i2] + [i1[a:a + 1] * nk + i2[:K // 2] for a in range(1, K)], axis=0)
    ts, es = _top_rows(cand, K, payload=expert)
    ts = jnp.concatenate(ts, axis=0)
    p = jnp.exp(ts - ts[0:1])
    g_ref[...] = p / jnp.sum(p, axis=0, keepdims=True)
    e_ref[...] = jnp.concatenate(es, axis=0)


def _peer_route(cfg, qp, k1, k2):
    T = qp.shape[0]
    tm, H, K, Q = cfg.route_tm, cfg.peer_heads, cfg.peer_topk, cfg.peer_qdim
    out = pl.BlockSpec((K, tm), lambda i, h: (h, i))
    return pl.pallas_call(
        functools.partial(_peer_route_kernel, cfg=cfg),
        grid=(T // tm, H),
        in_specs=[pl.BlockSpec((tm, Q), lambda i, h: (i, h)),
                  pl.BlockSpec((1,) + k1.shape[1:], lambda i, h: (h, 0, 0)),
                  pl.BlockSpec((1,) + k2.shape[1:], lambda i, h: (h, 0, 0))],
        out_specs=[out, out],
        out_shape=[jax.ShapeDtypeStruct((H * K, T), i32), jax.ShapeDtypeStruct((H * K, T), f32)],
        compiler_params=_cparams("parallel", "parallel"),
        name="peer_route",
    )(qp, k1, k2)


SC_LANES = 16
SC_ROWS = 32


def _sc_mesh():
    mesh = plsc.VectorSubcoreMesh(core_axis_name="core", subcore_axis_name="subcore")
    return mesh, mesh.num_cores * mesh.num_subcores


def _sc_worker(mesh):
    return lax.axis_index("core") * mesh.num_subcores + lax.axis_index("subcore")


def _sc_halves(words):
    return plsc.bitcast(words << 16, f32), plsc.bitcast(words & jnp.uint32(0xFFFF0000), f32)


def _sc_batches(tab_hbm, idx_v, bufs, sems, n_batches, compute):
    def gather(b, s):
        return pltpu.make_async_copy(tab_hbm.at[idx_v.at[pl.ds(b * SC_ROWS, SC_ROWS)]], bufs[s], sems.at[s])

    gather(0, 0).start()

    @pl.loop(0, n_batches, step=2)
    def _(b0):
        for s in range(2):
            b = b0 + s
            gather(b, s).wait()

            @pl.when(b + 1 < n_batches)
            def _():
                gather(b + 1, 1 - s).start()

            compute(b, bufs[s])


def _sc_peer_act(cfg, table_u, idx, hp):
    T, E = idx.shape
    W = table_u.shape[1]
    mesh, workers = _sc_mesh()
    tpw = T // workers
    bpt = E // SC_ROWS
    assert tpw * workers == T and bpt * SC_ROWS == E and (tpw * bpt) % 2 == 0
    RG = 4

    @functools.partial(
        pl.kernel, out_type=jax.ShapeDtypeStruct((T * E,), f32), mesh=mesh,
        scratch_types=[pltpu.VMEM((tpw * E,), i32), pltpu.VMEM((2 * W,), f32),
                       pltpu.VMEM((SC_ROWS, W), jnp.uint32), pltpu.VMEM((SC_ROWS, W), jnp.uint32),
                       pltpu.VMEM((E,), f32), pltpu.SemaphoreType.DMA((2,))],
        compiler_params=pltpu.CompilerParams(needs_layout_passes=False))
    def act_kernel(tab_hbm, idx_hbm, hp_hbm, act_hbm, idx_v, x_v, buf0, buf1, act_v, sems):
        t0 = _sc_worker(mesh) * tpw
        pltpu.sync_copy(idx_hbm.at[pl.ds(t0 * E, tpw * E)], idx_v)
        lane = lax.iota(i32, SC_LANES)

        def compute(b, buf):
            tl, bt = b // bpt, b % bpt

            @pl.when(bt == 0)
            def _():
                pltpu.sync_copy(hp_hbm.at[t0 + tl], x_v)

            for g in range(SC_ROWS // SC_LANES):
                def rows_body(i, vec):
                    r = g * SC_LANES + i * RG

                    def j_body(j, accs):
                        c = pl.multiple_of(j * SC_LANES, SC_LANES)
                        x_lo, x_hi = x_v[pl.ds(c, SC_LANES)], x_v[pl.ds(W + c, SC_LANES)]
                        out = []
                        for k in range(RG):
                            lo, hi = _sc_halves(buf[r + k, pl.ds(c, SC_LANES)])
                            out.append(accs[k] + lo * x_lo + hi * x_hi)
                        return tuple(out)

                    accs = lax.fori_loop(0, W // SC_LANES, j_body,
                                         tuple(jnp.zeros((SC_LANES,), f32) for _ in range(RG)))
                    for k in range(RG):
                        vec = jnp.where(lane == i * RG + k, jnp.sum(accs[k]), vec)
                    return vec

                vec = lax.fori_loop(0, SC_LANES // RG, rows_body, jnp.zeros((SC_LANES,), f32))
                act_v[pl.ds(pl.multiple_of(bt * SC_ROWS + g * SC_LANES, SC_LANES), SC_LANES)] = vec

            @pl.when(bt == bpt - 1)
            def _():
                pltpu.sync_copy(act_v, act_hbm.at[pl.ds((t0 + tl) * E, E)])

        _sc_batches(tab_hbm, idx_v, (buf0, buf1), sems, tpw * bpt, compute)

    return act_kernel(table_u, idx.reshape(-1), hp).reshape(T, E)


def _sc_peer_sum(cfg, table_v, idx, w):
    T, E = idx.shape
    W = table_v.shape[1]
    mesh, workers = _sc_mesh()
    tpw = T // workers
    bpt = E // SC_ROWS
    assert tpw * workers == T and bpt * SC_ROWS == E and (tpw * bpt) % 2 == 0
    NQ = 4
    qw = W // NQ

    @functools.partial(
        pl.kernel, out_type=jax.ShapeDtypeStruct((T, 2 * W), f32), mesh=mesh,
        scratch_types=[pltpu.VMEM((tpw * E,), i32), pltpu.VMEM((E,), f32), pltpu.VMEM((2 * W,), f32),
                       pltpu.VMEM((SC_ROWS, W), jnp.uint32), pltpu.VMEM((SC_ROWS, W), jnp.uint32),
                       pltpu.SemaphoreType.DMA((2,))],
        compiler_params=pltpu.CompilerParams(needs_layout_passes=False))
    def sum_kernel(tab_hbm, idx_hbm, w_hbm, y_hbm, idx_v, w_v, y_v, buf0, buf1, sems):
        t0 = _sc_worker(mesh) * tpw
        pltpu.sync_copy(idx_hbm.at[pl.ds(t0 * E, tpw * E)], idx_v)
        nq = qw // SC_LANES

        def compute(b, buf):
            tl, bt = b // bpt, b % bpt

            @pl.when(bt == 0)
            def _():
                pltpu.sync_copy(w_hbm.at[pl.ds((t0 + tl) * E, E)], w_v)

            for q in range(NQ):
                cols = [q * qw + n * SC_LANES for n in range(nq)]
                zero = jnp.zeros((SC_LANES,), f32)
                init = tuple(jnp.where(bt == 0, zero, y_v[pl.ds(c, SC_LANES)]) for c in cols) + \
                    tuple(jnp.where(bt == 0, zero, y_v[pl.ds(W + c, SC_LANES)]) for c in cols)

                def r_body(r, accs):
                    wr = plsc.load_gather(w_v, [jnp.full((SC_LANES,), bt * SC_ROWS + r, i32)])
                    lo_acc, hi_acc = list(accs[:nq]), list(accs[nq:])
                    for n, c in enumerate(cols):
                        lo, hi = _sc_halves(buf[r, pl.ds(c, SC_LANES)])
                        lo_acc[n] = lo_acc[n] + lo * wr
                        hi_acc[n] = hi_acc[n] + hi * wr
                    return tuple(lo_acc) + tuple(hi_acc)

                accs = lax.fori_loop(0, SC_ROWS, r_body, init)
                for n, c in enumerate(cols):
                    y_v[pl.ds(c, SC_LANES)] = accs[n]
                    y_v[pl.ds(W + c, SC_LANES)] = accs[nq + n]

            @pl.when(bt == bpt - 1)
            def _():
                pltpu.sync_copy(y_v, y_hbm.at[t0 + tl])

        _sc_batches(tab_hbm, idx_v, (buf0, buf1), sems, tpw * bpt, compute)

    return sum_kernel(table_v, idx.reshape(-1), w.reshape(-1))


def _pack_bf16_halves(a):
    h = a.shape[1] // 2
    bits = lambda t: lax.bitcast_convert_type(t.astype(bf16), jnp.uint16).astype(jnp.uint32)
    return bits(a[:, :h]) | (bits(a[:, h:]) << 16)


def _split_bf16(a):
    hi = a.astype(bf16)
    lo = (a - hi.astype(f32)).astype(bf16)
    return jnp.concatenate([hi, lo], axis=0)


def _fold_rows(a):
    n = a.shape[0] // 2
    return a[:n] + a[n:]


def _apply_group(h2, gates, words, sel, selt, gn, gf):
    R, D = h2.shape
    nw = D // 2
    nz = 2 * words.shape[0]
    lane = lax.broadcasted_iota(i32, (2 * R, nz), 1)
    rowi = lax.broadcasted_iota(i32, (2 * R, nz), 0)
    mine = ((lane // 2) % R == rowi % R) & (lane % 2 == rowi // R)
    zu = pltpu.bitcast(words[:, :nw], bf16)
    zv = pltpu.bitcast(words[:, nw:], bf16)
    x = _rms(h2, gn)
    xs = _split_bf16(jnp.concatenate([x[:, :nw], x[:, nw:]], axis=0))
    a = _fold_rows(lax.dot_general(xs, zu, (((1,), (1,)), ((), ())), preferred_element_type=f32))
    a = jnp.where(mine, a, 0.0)
    a = a[:R] + a[R:]
    act = _fold_rows(jnp.dot(_split_bf16(a), sel, preferred_element_type=f32))
    w = gates * _gelu(act)
    w_rows = _fold_rows(jnp.dot(_split_bf16(w), selt, preferred_element_type=f32))
    ws = _split_bf16(jnp.where(mine, jnp.concatenate([w_rows, w_rows], axis=0), 0.0))
    y2 = _fold_rows(jnp.dot(ws, zv, preferred_element_type=f32))
    y = jnp.concatenate([y2[:R], y2[R:]], axis=1)
    return _rms(h2 + y, gf)


def _peer_direct_kernel(idx_ref, idxn_ref, h_ref, gt_ref, tab_ref, sel_ref, selt_ref, gn_ref, gf_ref, o_ref,
                        buf_a, buf_b, sem_ref, *, cfg):
    E, R = cfg.slots, SUBLANES
    n_rows = E * R
    i = pl.program_id(0)

    def start_row(src_idx_ref, base, r, buf, s, priority=0):
        pltpu.async_copy(tab_ref.at[pl.ds(src_idx_ref[base + r], 1)], buf.at[pl.ds(r, 1)], sem_ref.at[s],
                         priority=priority)

    def wait_rows(buf, s):
        pltpu.make_async_copy(tab_ref.at[pl.ds(0, n_rows)], buf, sem_ref.at[s]).wait()

    def group(gi, buf):
        tok = slice(gi * R, (gi + 1) * R)
        o_ref[tok, :] = _apply_group(h_ref[tok, :], gt_ref[tok, :], buf[...], sel_ref[...], selt_ref[...],
                                     gn_ref[...], gf_ref[...])

    @pl.when(i == 0)
    def _():
        def body(r, carry):
            start_row(idx_ref, 0, r, buf_a, 0)
            return carry
        lax.fori_loop(0, n_rows, body, 0, unroll=8)

    wait_rows(buf_a, 0)
    for r in range(n_rows):
        start_row(idx_ref, n_rows, r, buf_b, 1, priority=r % 2)
    group(0, buf_a)
    wait_rows(buf_b, 1)
    for r in range(n_rows):
        start_row(idxn_ref, 0, r, buf_a, 0, priority=r % 2)
    group(1, buf_b)

    @pl.when(i == pl.num_programs(0) - 1)
    def _():
        wait_rows(buf_a, 0)


def _peer_direct(cfg, h2, gates, idx, table, gn, gf):
    D = h2.shape[1]
    E, R = cfg.slots, SUBLANES
    T = idx.shape[0] // E
    n_rows = E * R
    nz = 2 * n_rows
    n = T // (2 * R)
    sel = (jnp.arange(nz)[:, None] // (2 * R) == jnp.arange(E)[None, :]).astype(bf16)
    row = lambda i: (i, 0)
    smem = functools.partial(pl.BlockSpec, (2 * n_rows,), memory_space=pltpu.SMEM)
    return pl.pallas_call(
        functools.partial(_peer_direct_kernel, cfg=cfg),
        grid=(n,),
        in_specs=[smem(lambda i: (i,)), smem(lambda i: (jnp.minimum(i + 1, n - 1),)),
                  pl.BlockSpec((2 * R, D), row), pl.BlockSpec((2 * R, E), row), pl.BlockSpec(memory_space=pl.ANY),
                  _resident((nz, E)), _resident((E, nz)), _resident((1, D)), _resident((1, D))],
        out_specs=pl.BlockSpec((2 * R, D), row),
        out_shape=jax.ShapeDtypeStruct((T, D), f32),
        scratch_shapes=[pltpu.VMEM((n_rows, D), table.dtype), pltpu.VMEM((n_rows, D), table.dtype),
                        pltpu.SemaphoreType.DMA((2,))],
        compiler_params=pltpu.CompilerParams(dimension_semantics=("arbitrary",), vmem_limit_bytes=VMEM_LIMIT,
                                             disable_bounds_checks=True),
        name="peer_direct",
    )(idx, idx, h2, gates, table, sel, sel.T, gn, gf)


def _rowwise(name, body, out_cols, T, row0, *arrays, rows_per_step=128):
    tm = min(rows_per_step, T)
    assert T % tm == 0 and row0 % tm == 0

    def spec(a):
        if a.shape[0] == 1:
            return _resident(a.shape)
        off = row0 // tm if a.shape[0] > T else 0
        return pl.BlockSpec((tm, a.shape[1]), lambda i: (i + off, 0))

    def kern(*refs):
        refs[-1][...] = body(*(r[...] for r in refs[:-1]))

    return pl.pallas_call(
        kern, grid=(T // tm,),
        in_specs=[spec(a) for a in arrays],
        out_specs=pl.BlockSpec((tm, out_cols), lambda i: (i, 0)),
        out_shape=jax.ShapeDtypeStruct((T, out_cols), f32),
        compiler_params=_cparams("parallel"), name=name,
    )(*arrays)


def _peer_pre(cfg, h2, row0, gn):
    return _rowwise("peer_pre", _rms, h2.shape[1], h2.shape[0] - row0, row0, h2, gn)


def _peer_mid(cfg, gates, row0, act):
    return _rowwise("peer_mid", lambda g, a: g * _gelu(a), gates.shape[1], act.shape[0], row0, gates, act)


def _peer_post(cfg, h2, row0, y, gf):
    return _rowwise("peer_post", lambda h, yy, g: _rms(h + yy, g), h2.shape[1], y.shape[0], row0, h2, y, gf)


def _block(cfg, sc_fns, x, mem, positions, mix_norm_g, w_in, lam_q1, lam_k1, lam_q2, lam_k2, attn_head_g,
           ssm_a_re, ssm_a_im, ssm_log_dt, ssm_b_re, ssm_b_im, ssm_c_re, ssm_c_im, ssm_d, glu_w, glu_b,
           ssm_out_g, w_out, xattn_norm_g, mem_norm_g, xattn_wq, xattn_wkv, xattn_wo, ffn_norm_g,
           peer_wq, peer_k1, peer_k2, peer_u, peer_v, final_norm_g):
    B, S, D = x.shape
    T = B * S
    l = 0
    row = lambda a: a.reshape(1, -1)
    x2 = x.reshape(T, D)
    pos = positions.reshape(T, 1).astype(f32)
    freqs = cfg.rope_theta ** (-jnp.arange(0, cfg.rot_dim, 2, dtype=f32) / cfg.rot_dim)
    lane = jnp.arange(LANES) % cfg.diff_qkdim
    freq_row = jnp.where(lane < cfg.rot_dim, freqs[lane % (cfg.rot_dim // 2)], 0.0).reshape(1, LANES)

    lam_p = jnp.stack([lam_q1[l], lam_k1[l], lam_q2[l], lam_k2[l]])
    s5_params = _s5_params(cfg, ssm_a_re[l], ssm_a_im[l], ssm_log_dt[l], ssm_b_re[l], ssm_b_im[l],
                           ssm_c_re[l], ssm_c_im[l], ssm_d[l])
    ns, _, wn = s5_params[0].shape
    zero_state = jnp.zeros((ns, B, wn), f32)
    state = (zero_state, zero_state)
    w_in_b, glu_w_b, w_out_b = w_in[l].astype(bf16), glu_w[l].astype(bf16), w_out[l].astype(bf16)
    wq_b, wo_b, pw_b = xattn_wq[l].astype(bf16), xattn_wo[l].astype(bf16), peer_wq[l].astype(bf16)
    k1_b, k2_b = peer_k1[l].astype(bf16), peer_k2[l].astype(bf16)
    kv = _mem_kv(cfg, mem.reshape(B * cfg.n_mem, D), row(mem_norm_g[l]), xattn_wkv[l].astype(bf16))
    table_u, table_v = _pack_bf16_halves(peer_u[l]), _pack_bf16_halves(peer_v[l])
    table = jnp.concatenate([table_u, table_v], axis=1)
    E, R = cfg.slots, SUBLANES
    Lc = cfg.chunk
    Tc = B * Lc
    td = Tc * cfg.direct_sixteenths // 16
    sc_act, sc_sum = sc_fns

    def after(a, tokens):
        return lax.optimization_barrier((a,) + tuple(tokens))[0] if tokens else a

    def dense(c, qks, vs, state, tokens):
        qk, v, u = _in_proj(cfg, c, x2, pos, after(row(mix_norm_g[l]), tokens), freq_row, w_in_b)
        qks, vs = qks + [qk], vs + [v]
        att = _diff_attn(cfg, lam_p, qks, vs, row(attn_head_g[l]))
        y, state = _s5_scan(cfg, u.reshape(Tc, cfg.ssm_width), s5_params, state)
        ssm = _s5_glu(cfg, y, glu_w_b, row(glu_b[l]), row(ssm_out_g[l]))
        h1, hn = _out_proj(cfg, c, x2, att, ssm.reshape(Lc, B * cfg.ssm_width), w_out_b, row(xattn_norm_g[l]))
        h2, qp = _xattn(cfg, h1, hn, kv, wq_b, wo_b, row(ffn_norm_g[l]), pw_b)
        experts_t, gates_t = _peer_route(cfg, qp, k1_b, k2_b)
        return qks, vs, state, h2, experts_t, gates_t.T

    gn, gf = row(ffn_norm_g[l]), row(final_norm_g)
    qks, vs, outs, tokens, pending = [], [], [], [], None
    for c in range(cfg.n_chunks):
        qks, vs, state, h2, experts_t, gates = dense(c, qks, vs, state, tokens)
        parts, tokens, before_direct = [], [], []
        if td < Tc:
            idx_s = experts_t.T[td:]
            hp = _peer_pre(cfg, h2, td, gn)
            before_direct.append(hp)
        if pending is not None:
            p_h2, p_y, p_parts = pending
            p_parts.append(_peer_post(cfg, p_h2, td, p_y, after(gf, before_direct)))
            before_direct.append(p_parts[-1])
            outs.append(jnp.concatenate(p_parts, axis=0).reshape(B, Lc, D))
            pending = None
        if td:
            idx_d = experts_t[:, :td].reshape(E, td // R, R).transpose(1, 0, 2).reshape(-1)
            parts.append(_peer_direct(cfg, h2, gates, idx_d, table, after(gn, before_direct), gf))
        if td < Tc:
            act = sc_act(table_u, idx_s, hp)
            w = _peer_mid(cfg, after(gates, parts), td, act)
            tokens = [w]
            pending = (h2, sc_sum(table_v, idx_s, w), parts)
        else:
            outs.append(jnp.concatenate(parts, axis=0).reshape(B, Lc, D))
    if pending is not None:
        p_h2, p_y, p_parts = pending
        p_parts.append(_peer_post(cfg, p_h2, td, p_y, gf))
        outs.append(jnp.concatenate(p_parts, axis=0).reshape(B, Lc, D))
    return jnp.concatenate(outs, axis=1)


def kernel(x, mem, positions, mix_norm_g, w_in, lam_q1, lam_k1, lam_q2, lam_k2, attn_head_g, ssm_a_re, ssm_a_im, ssm_log_dt, ssm_b_re, ssm_b_im, ssm_c_re, ssm_c_im, ssm_d, glu_w, glu_b, ssm_out_g, w_out, xattn_norm_g, mem_norm_g, xattn_wq, xattn_wkv, xattn_wo, ffn_norm_g, peer_wq, peer_k1, peer_k2, peer_u, peer_v, final_norm_g):
    cfg = Cfg()
    return _block(cfg, (functools.partial(_sc_peer_act, cfg), functools.partial(_sc_peer_sum, cfg)), x, mem, positions, mix_norm_g, w_in, lam_q1, lam_k1,
                  lam_q2, lam_k2, attn_head_g, ssm_a_re, ssm_a_im, ssm_log_dt, ssm_b_re, ssm_b_im, ssm_c_re,
                  ssm_c_im, ssm_d, glu_w, glu_b, ssm_out_g, w_out, xattn_norm_g, mem_norm_g, xattn_wq,
                  xattn_wkv, xattn_wo, ffn_norm_g, peer_wq, peer_k1, peer_k2, peer_u, peer_v, final_norm_g)
```

```python
import dataclasses
import functools
import math

import jax
import jax.numpy as jnp
from jax import lax
from jax.experimental import pallas as pl
from jax.experimental.pallas import tpu as pltpu
from jax.experimental.pallas import tpu_sc as plsc

f32 = jnp.float32
bf16 = jnp.bfloat16
i32 = jnp.int32

LANES = 128
SUBLANES = 8
VMEM_LIMIT = 56 * 1024 * 1024
NEG = -1e30
EPS = 1e-6


@dataclasses.dataclass(frozen=True)
class Cfg:
    d_model: int = 2048
    batch: int = 8
    seq: int = 2048
    n_mem: int = 256
    diff_heads: int = 8
    ssm_group: int = 16
    ssm_state: int = 64
    xattn_heads: int = 4
    xattn_head_dim: int = 128
    peer_heads: int = 8
    peer_keys: int = 128
    peer_qdim: int = 256
    peer_topk: int = 16
    rope_theta: float = 500000.0
    lam_init: float = 0.8 - 0.6 * math.exp(-0.3 * 0)
    chunk: int = 256
    scan_chunk: int = 128
    route_tm: int = 256
    direct_sixteenths: int = 4

    @property
    def n_chunks(self):
        return self.seq // self.chunk

    @property
    def attn_width(self):
        return self.d_model // 2

    @property
    def ssm_width(self):
        return self.d_model - self.attn_width

    @property
    def diff_vdim(self):
        return self.attn_width // self.diff_heads

    @property
    def diff_qkdim(self):
        return self.diff_vdim // 2

    @property
    def rot_dim(self):
        return self.diff_qkdim // 4

    @property
    def ssm_groups(self):
        return self.ssm_width // self.ssm_group

    @property
    def xattn_width(self):
        return self.xattn_heads * self.xattn_head_dim

    @property
    def tokens(self):
        return self.batch * self.seq

    @property
    def slots(self):
        return self.peer_heads * self.peer_topk


def _cparams(*sem):
    return pltpu.CompilerParams(dimension_semantics=sem, vmem_limit_bytes=VMEM_LIMIT)


def _resident(shape):
    nd = len(shape)
    return pl.BlockSpec(shape, lambda *_: (0,) * nd, pipeline_mode=pl.Buffered(1))


def _rms(x, g):
    return x * lax.rsqrt(jnp.mean(x * x, axis=-1, keepdims=True) + EPS) * g


def _gelu(x):
    return 0.5 * x * (1.0 + lax.erf(x * (2.0 ** -0.5)))


def _in_proj_kernel(x_ref, pos_ref, g_ref, freq_ref, w_ref, qk_ref, v_ref, u_ref, *, cfg):
    n_qk, n_v = 2 * cfg.attn_width, cfg.attn_width
    half = cfg.rot_dim // 2
    xn = _rms(x_ref[...], g_ref[...]).astype(bf16)
    ang = pos_ref[...] * freq_ref[...]
    cos, sin = jnp.cos(ang), jnp.sin(ang)
    lane = lax.broadcasted_iota(i32, (1, LANES), 1) % cfg.diff_qkdim
    sin_lo = jnp.where(lane < half, -sin, 0.0)
    sin_hi = jnp.where((lane >= half) & (lane < 2 * half), sin, 0.0)
    cw = 2 * LANES
    for c in range((n_qk + n_v + cfg.ssm_width) // cw):
        col = c * cw
        z = jnp.dot(xn, w_ref[:, col:col + cw], preferred_element_type=f32)
        if col < n_qk:
            for k in range(cw // LANES):
                zk = z[:, k * LANES:(k + 1) * LANES]
                zk = zk * cos + pltpu.roll(zk, LANES - half, 1) * sin_lo + pltpu.roll(zk, half, 1) * sin_hi
                qk_ref[:, col + k * LANES:col + (k + 1) * LANES] = zk.astype(bf16)
        elif col < n_qk + n_v:
            v_ref[:, col - n_qk:col - n_qk + cw] = z.astype(bf16)
        else:
            u_ref[:, col - n_qk - n_v:col - n_qk - n_v + cw] = z


def _chunk_rows(cfg, c):
    return lambda b: (b * cfg.n_chunks + c, 0)


def _in_proj(cfg, c, x2, pos, g, freq, w):
    D = x2.shape[1]
    B, tm = cfg.batch, cfg.chunk
    n_qk, n_v, n_u = 2 * cfg.attn_width, cfg.attn_width, cfg.ssm_width
    row = lambda b: (b, 0)
    return pl.pallas_call(
        functools.partial(_in_proj_kernel, cfg=cfg),
        grid=(B,),
        in_specs=[pl.BlockSpec((tm, D), _chunk_rows(cfg, c)), pl.BlockSpec((tm, 1), _chunk_rows(cfg, c)),
                  _resident((1, D)), _resident((1, LANES)), _resident(w.shape)],
        out_specs=[pl.BlockSpec((tm, n_qk), row), pl.BlockSpec((tm, n_v), row),
                   pl.BlockSpec((tm, n_u), lambda b: (0, b))],
        out_shape=[jax.ShapeDtypeStruct((B * tm, n_qk), bf16), jax.ShapeDtypeStruct((B * tm, n_v), bf16),
                   jax.ShapeDtypeStruct((tm, B * n_u), f32)],
        compiler_params=_cparams("parallel"),
        name="in_proj",
    )(x2, pos, g, freq, w)


def _diff_attn_kernel(lam_ref, q_ref, *rest, cfg, n_kv):
    k_refs, v_refs = rest[:n_kv], rest[n_kv:2 * n_kv]
    g_ref, o_ref = rest[2 * n_kv:]
    tq = cfg.chunk
    d = cfg.diff_qkdim
    lane = lax.broadcasted_iota(i32, (1, LANES), 1)
    q = q_ref[...].astype(f32) * (d ** -0.5)
    qs = (jnp.where(lane < d, q, 0.0).astype(bf16), jnp.where(lane >= d, q, 0.0).astype(bf16))
    causal = (lax.broadcasted_iota(i32, (tq, tq), 1) <= lax.broadcasted_iota(i32, (tq, tq), 0))
    tiles = lambda a: [a[:, t * LANES:(t + 1) * LANES] for t in range(tq // LANES)]

    def scores(c, j):
        s = lax.dot_general(qs[c], k_refs[j][...], (((1,), (1,)), ((), ())), preferred_element_type=f32)
        return jnp.where(causal, s, NEG) if j == n_kv - 1 else s

    outs = []
    for c in range(2):
        mm = jnp.full((tq, LANES), NEG, f32)
        for j in range(n_kv):
            for st in tiles(scores(c, j)):
                mm = jnp.maximum(mm, st)
        mb = jnp.broadcast_to(jnp.max(mm, axis=1, keepdims=True), (tq, LANES))
        ls = jnp.zeros((tq, LANES), f32)
        acc = jnp.zeros((tq, LANES), f32)
        for j in range(n_kv):
            ps = [jnp.exp(st - mb) for st in tiles(scores(c, j))]
            for p in ps:
                ls = ls + p
            acc = acc + jnp.dot(jnp.concatenate(ps, axis=1).astype(bf16), v_refs[j][...],
                                preferred_element_type=f32)
        outs.append(acc / jnp.sum(ls, axis=1, keepdims=True))

    lp = lam_ref[...]
    lam = (jnp.exp(jnp.sum(lp[0:1] * lp[1:2], axis=1, keepdims=True))
           - jnp.exp(jnp.sum(lp[2:3] * lp[3:4], axis=1, keepdims=True)) + cfg.lam_init)
    o = outs[0] - lam * outs[1]
    o_ref[...] = (_rms(o, g_ref[...]) * (1.0 - cfg.lam_init)).astype(o_ref.dtype)


def _diff_attn(cfg, lam_p, qks, vs, g):
    n_kv = len(qks)
    B, H, tq = cfg.batch, cfg.diff_heads, cfg.chunk
    head = lambda b, h: (b, h)
    key = lambda b, h: (b, H + h)
    return pl.pallas_call(
        functools.partial(_diff_attn_kernel, cfg=cfg, n_kv=n_kv),
        grid=(B, H),
        in_specs=([_resident(lam_p.shape), pl.BlockSpec((tq, LANES), head)]
                  + [pl.BlockSpec((tq, LANES), key)] * n_kv + [pl.BlockSpec((tq, LANES), head)] * n_kv
                  + [_resident((1, LANES))]),
        out_specs=pl.BlockSpec((tq, LANES), head),
        out_shape=jax.ShapeDtypeStruct((B * tq, cfg.attn_width), bf16),
        compiler_params=_cparams("parallel", "parallel"),
        name="diff_attn",
    )(lam_p, qks[-1], *qks, *vs, g)


def _s5_scan_kernel(u_ref, are_ref, aim_ref, ldt_ref, bre_ref, bim_ref, cre_ref, cim_ref, d_ref,
                    sin_re_ref, sin_im_ref, y_ref, sout_re_ref, sout_im_ref,
                    ab_ref, bbar_re_ref, bbar_im_ref, bu_re_ref, bu_im_ref, *, cfg):
    B, L = cfg.batch, cfg.scan_chunk
    t = pl.program_id(1)
    st_re_ref, st_im_ref = sout_re_ref.at[0], sout_im_ref.at[0]

    @pl.when(t == 0)
    def _():
        dt = jnp.exp(ldt_ref[0])
        lre, lim = are_ref[0], aim_ref[0]
        mag = jnp.exp(lre * dt)
        ang = lim * dt
        ab_re, ab_im = mag * jnp.cos(ang), mag * jnp.sin(ang)
        den = lre * lre + lim * lim
        f_re = ((ab_re - 1.0) * lre + ab_im * lim) / den
        f_im = (ab_im * lre - (ab_re - 1.0) * lim) / den
        ab_ref[0] = jnp.broadcast_to(ab_re, ab_ref.shape[1:])
        ab_ref[1] = jnp.broadcast_to(ab_im, ab_ref.shape[1:])
        br, bi = bre_ref[0], bim_ref[0]
        bbar_re_ref[...] = (f_re * br - f_im * bi).astype(bf16)
        bbar_im_ref[...] = (f_re * bi + f_im * br).astype(bf16)
        st_re_ref[...] = sin_re_ref[0]
        st_im_ref[...] = sin_im_ref[0]

    u = u_ref[...]
    ub = u.astype(bf16)
    bu_re_ref[...] = jnp.dot(ub, bbar_re_ref[...], preferred_element_type=f32)
    bu_im_ref[...] = jnp.dot(ub, bbar_im_ref[...], preferred_element_type=f32)
    a_re, a_im = ab_ref[0], ab_ref[1]

    def body(s, carry):
        xr, xi = carry
        rows = pl.ds(pl.multiple_of(s * B, B), B)
        nr = a_re * xr - a_im * xi + bu_re_ref[rows, :]
        ni = a_re * xi + a_im * xr + bu_im_ref[rows, :]
        bu_re_ref[rows, :] = nr
        bu_im_ref[rows, :] = ni
        return nr, ni

    xr, xi = lax.fori_loop(0, L, body, (st_re_ref[...], st_im_ref[...]), unroll=8)
    st_re_ref[...] = xr
    st_im_ref[...] = xi

    y_ref[...] = (jnp.dot(bu_re_ref[...].astype(bf16), cre_ref[0].astype(bf16), preferred_element_type=f32)
                  - jnp.dot(bu_im_ref[...].astype(bf16), cim_ref[0].astype(bf16), preferred_element_type=f32)
                  + d_ref[0] * u)


def _block_diag(w, blocks):
    G, r, c = w.shape
    w4 = w.reshape(G // blocks, blocks, r, c)
    eye = jnp.eye(blocks, dtype=w.dtype)
    out = w4[:, :, :, None, :] * eye[None, :, None, :, None]
    return out.reshape(G // blocks, blocks * r, blocks * c)


def _s5_params(cfg, a_re, a_im, log_dt, b_re, b_im, c_re, c_im, d_skip):
    G, P, Hc = cfg.ssm_groups, cfg.ssm_state, cfg.ssm_group
    gps = min(G, 2 * LANES // Hc)
    ns = G // gps
    wu, wn = gps * Hc, gps * P
    return (a_re.reshape(ns, 1, wn), a_im.reshape(ns, 1, wn), jnp.repeat(log_dt, P).reshape(ns, 1, wn),
            _block_diag(b_re.transpose(0, 2, 1), gps), _block_diag(b_im.transpose(0, 2, 1), gps),
            _block_diag(c_re.transpose(0, 2, 1), gps), _block_diag(c_im.transpose(0, 2, 1), gps),
            d_skip.reshape(ns, 1, wu))


def _s5_scan(cfg, u2, params, state):
    B, L = cfg.batch, cfg.scan_chunk
    rows, W = u2.shape
    ns, _, wn = params[0].shape
    wu = params[-1].shape[2]
    slab = lambda s, t: (s, 0, 0)
    st = pl.BlockSpec((1, B, wn), slab)
    st_shape = jax.ShapeDtypeStruct((ns, B, wn), f32)
    y, s_re, s_im = pl.pallas_call(
        functools.partial(_s5_scan_kernel, cfg=cfg),
        grid=(ns, rows // (L * B)),
        in_specs=[pl.BlockSpec((L * B, wu), lambda s, t: (t, s)),
                  pl.BlockSpec((1, 1, wn), slab), pl.BlockSpec((1, 1, wn), slab), pl.BlockSpec((1, 1, wn), slab),
                  pl.BlockSpec((1, wu, wn), slab), pl.BlockSpec((1, wu, wn), slab),
                  pl.BlockSpec((1, wn, wu), slab), pl.BlockSpec((1, wn, wu), slab),
                  pl.BlockSpec((1, 1, wu), slab), st, st],
        out_specs=[pl.BlockSpec((L * B, wu), lambda s, t: (t, s)), st, st],
        out_shape=[jax.ShapeDtypeStruct((rows, W), f32), st_shape, st_shape],
        scratch_shapes=[pltpu.VMEM((2, B, wn), f32),
                        pltpu.VMEM((wu, wn), bf16), pltpu.VMEM((wu, wn), bf16),
                        pltpu.VMEM((L * B, wn), f32), pltpu.VMEM((L * B, wn), f32)],
        compiler_params=_cparams("arbitrary", "arbitrary"),
        name="s5_scan",
    )(u2, *params, *state)
    return y, (s_re, s_im)


def _s5_glu_kernel(y_ref, w_ref, b_ref, g_ref, o_ref):
    g = _gelu(y_ref[...])
    z = jnp.dot(g.astype(bf16), w_ref[...], preferred_element_type=f32) + b_ref[...]
    g = g * jax.nn.sigmoid(z)
    o_ref[...] = _rms(g, g_ref[...]).astype(o_ref.dtype)


def _s5_glu(cfg, y2, w, b, g):
    T, W = y2.shape
    tm = cfg.chunk
    row = lambda i: (i, 0)
    return pl.pallas_call(
        _s5_glu_kernel,
        grid=(T // tm,),
        in_specs=[pl.BlockSpec((tm, W), row), _resident(w.shape), _resident((1, W)), _resident((1, W))],
        out_specs=pl.BlockSpec((tm, W), row),
        out_shape=jax.ShapeDtypeStruct((T, W), bf16),
        compiler_params=_cparams("parallel"),
        name="s5_glu",
    )(y2, w, b, g)


def _out_proj_kernel(x_ref, att_ref, ssm_ref, w_ref, g_ref, h_ref, hn_ref, *, cfg):
    aw = cfg.attn_width
    h = (x_ref[...]
         + jnp.dot(att_ref[...], w_ref[:aw, :], preferred_element_type=f32)
         + jnp.dot(ssm_ref[...], w_ref[aw:, :], preferred_element_type=f32))
    h_ref[...] = h
    hn_ref[...] = _rms(h, g_ref[...]).astype(hn_ref.dtype)


def _out_proj(cfg, c, x2, att, ssm, w, g):
    D = x2.shape[1]
    tm = cfg.chunk
    T = cfg.batch * tm
    row = lambda i: (i, 0)
    return pl.pallas_call(
        functools.partial(_out_proj_kernel, cfg=cfg),
        grid=(T // tm,),
        in_specs=[pl.BlockSpec((tm, D), _chunk_rows(cfg, c)), pl.BlockSpec((tm, cfg.attn_width), row),
                  pl.BlockSpec((tm, cfg.ssm_width), lambda b: (0, b)), _resident(w.shape), _resident((1, D))],
        out_specs=[pl.BlockSpec((tm, D), row), pl.BlockSpec((tm, D), row)],
        out_shape=[jax.ShapeDtypeStruct((T, D), f32), jax.ShapeDtypeStruct((T, D), bf16)],
        compiler_params=_cparams("parallel"),
        name="out_proj",
    )(x2, att, ssm, w, g)


def _mem_kv_kernel(m_ref, g_ref, w_ref, o_ref):
    mn = _rms(m_ref[...], g_ref[...]).astype(bf16)
    o_ref[...] = jnp.dot(mn, w_ref[...], preferred_element_type=f32).astype(o_ref.dtype)


def _mem_kv(cfg, mem2, g, w):
    R, D = mem2.shape
    tm = cfg.n_mem
    row = lambda i: (i, 0)
    return pl.pallas_call(
        _mem_kv_kernel,
        grid=(R // tm,),
        in_specs=[pl.BlockSpec((tm, D), row), _resident((1, D)), _resident(w.shape)],
        out_specs=pl.BlockSpec((tm, w.shape[1]), row),
        out_shape=jax.ShapeDtypeStruct((R, w.shape[1]), bf16),
        compiler_params=_cparams("parallel"),
        name="mem_kv",
    )(mem2, g, w)


def _xattn_kernel(h_ref, hn_ref, kv_ref, wq_ref, wo_ref, g_ref, pw_ref, h2_ref, qp_ref, *, cfg):
    nh, hd, xw = cfg.xattn_heads, cfg.xattn_head_dim, cfg.xattn_width
    q = jnp.dot(hn_ref[...], wq_ref[...], preferred_element_type=f32).astype(bf16)
    outs = []
    for h in range(nh):
        qh = q[:, h * hd:(h + 1) * hd]
        kh = kv_ref[:, h * hd:(h + 1) * hd]
        vh = kv_ref[:, xw + h * hd:xw + (h + 1) * hd]
        s = lax.dot_general(qh, kh, (((1,), (1,)), ((), ())), preferred_element_type=f32) * (hd ** -0.5)
        p = jnp.exp(s - jnp.max(s, axis=1, keepdims=True))
        p = p / jnp.sum(p, axis=1, keepdims=True)
        outs.append(jnp.dot(p.astype(bf16), vh, preferred_element_type=f32).astype(bf16))
    o = jnp.concatenate(outs, axis=1)
    h2 = h_ref[...] + jnp.dot(o, wo_ref[...], preferred_element_type=f32)
    h2_ref[...] = h2
    hp = _rms(h2, g_ref[...]).astype(bf16)
    qp_ref[...] = jnp.dot(hp, pw_ref[...], preferred_element_type=f32).astype(qp_ref.dtype)


def _xattn(cfg, h1, hn, kv, wq, wo, g, pw):
    T, D = h1.shape
    tm, M = cfg.chunk, cfg.n_mem
    row = lambda i: (i, 0)
    return pl.pallas_call(
        functools.partial(_xattn_kernel, cfg=cfg),
        grid=(T // tm,),
        in_specs=[pl.BlockSpec((tm, D), row), pl.BlockSpec((tm, D), row),
                  pl.BlockSpec((M, kv.shape[1]), row),
                  _resident(wq.shape), _resident(wo.shape), _resident((1, D)), _resident(pw.shape)],
        out_specs=[pl.BlockSpec((tm, D), row), pl.BlockSpec((tm, pw.shape[1]), row)],
        out_shape=[jax.ShapeDtypeStruct((T, D), f32), jax.ShapeDtypeStruct((T, pw.shape[1]), bf16)],
        compiler_params=_cparams("parallel"),
        name="xattn",
    )(h1, hn, kv, wq, wo, g, pw)


def _top_rows(s, k, payload=None):
    n = s.shape[0]
    rows = lax.broadcasted_iota(i32, s.shape, 0)
    vals, picks = [], []
    for _ in range(k):
        m = jnp.max(s, axis=0, keepdims=True)
        idx = jnp.min(jnp.where(s == m, rows, n), axis=0, keepdims=True)
        sel = rows == idx
        vals.append(m)
        picks.append(idx if payload is None else jnp.max(jnp.where(sel, payload, -1), axis=0, keepdims=True))
        s = jnp.where(sel, NEG, s)
    return vals, picks


def _peer_route_kernel(qp_ref, k1_ref, k2_ref, e_ref, g_ref, *, cfg):
    K, nk, half = cfg.peer_topk, cfg.peer_keys, cfg.peer_qdim // 2
    tm = qp_ref.shape[0]
    dn = (((1,), (1,)), ((), ()))
    s1 = lax.dot_general(k1_ref[0], qp_ref[:, :half], dn, preferred_element_type=f32)
    s2 = lax.dot_general(k2_ref[0], qp_ref[:, half:], dn, preferred_element_type=f32)
    t1, i1 = _top_rows(s1, K)
    t2, i2 = _top_rows(s2, K)
    t1, i1 = jnp.concatenate(t1, axis=0), jnp.concatenate(i1, axis=0)
    t2, i2 = jnp.concatenate(t2, axis=0), jnp.concatenate(i2, axis=0)
    cand = jnp.concatenate([t1[0:1] + t2] + [t1[a:a + 1] + t2[:K // 2] for a in range(1, K)], axis=0)
    expert = jnp.concatenate([i1[0:1] * nk + i2] + [i1[a:a + 1] * nk + i2[:K // 2] for a in range(1, K)], axis=0)
    ts, es = _top_rows(cand, K, payload=expert)
    ts = jnp.concatenate(ts, axis=0)
    p = jnp.exp(ts - ts[0:1])
    g_ref[...] = p / jnp.sum(p, axis=0, keepdims=True)
    e_ref[...] = jnp.concatenate(es, axis=0)


def _peer_route(cfg, qp, k1, k2):
    T = qp.shape[0]
    tm, H, K, Q = cfg.route_tm, cfg.peer_heads, cfg.peer_topk, cfg.peer_qdim
    out = pl.BlockSpec((K, tm), lambda i, h: (h, i))
    return pl.pallas_call(
        functools.partial(_peer_route_kernel, cfg=cfg),
        grid=(T // tm, H),
        in_specs=[pl.BlockSpec((tm, Q), lambda i, h: (i, h)),
                  pl.BlockSpec((1,) + k1.shape[1:], lambda i, h: (h, 0, 0)),
                  pl.BlockSpec((1,) + k2.shape[1:], lambda i, h: (h, 0, 0))],
        out_specs=[out, out],
        out_shape=[jax.ShapeDtypeStruct((H * K, T), i32), jax.ShapeDtypeStruct((H * K, T), f32)],
        compiler_params=_cparams("parallel", "parallel"),
        name="peer_route",
    )(qp, k1, k2)


SC_LANES = 16
SC_ROWS = 32
SC_BUFS = 3


def _sc_mesh():
    mesh = plsc.VectorSubcoreMesh(core_axis_name="core", subcore_axis_name="subcore")
    return mesh, mesh.num_cores * mesh.num_subcores


def _sc_worker(mesh):
    return lax.axis_index("core") * mesh.num_subcores + lax.axis_index("subcore")


def _sc_halves(words):
    return plsc.bitcast(words << 16, f32), plsc.bitcast(words & jnp.uint32(0xFFFF0000), f32)


def _sc_batches(tab_hbm, idx_v, bufs, sems, n_batches, compute):
    nb = len(bufs)

    def gather(b, s):
        return pltpu.make_async_copy(tab_hbm.at[idx_v.at[pl.ds(b * SC_ROWS, SC_ROWS)]], bufs[s], sems.at[s])

    def step(b, s, when):
        gather(b, s).wait()
        when(b + nb - 1 < n_batches, lambda: gather(b + nb - 1, (s - 1) % nb).start())
        compute(b, bufs[s])

    for s in range(min(nb - 1, n_batches)):
        gather(s, s).start()
    main = n_batches // nb * nb

    @pl.loop(0, main, step=nb)
    def _(b0):
        for s in range(nb):
            step(b0 + s, s, lambda cond, fn: pl.when(cond)(fn))

    for b in range(main, n_batches):
        step(b, b % nb, lambda cond, fn: fn() if cond else None)


def _sc_peer_act(cfg, table_u, idx, hp):
    T, E = idx.shape
    W = table_u.shape[1]
    mesh, workers = _sc_mesh()
    tpw = T // workers
    bpt = E // SC_ROWS
    assert tpw * workers == T and bpt * SC_ROWS == E and (tpw * bpt) % SC_BUFS == 0
    RG = 4

    @functools.partial(
        pl.kernel, out_type=jax.ShapeDtypeStruct((T * E,), f32), mesh=mesh,
        scratch_types=[pltpu.VMEM((tpw * E,), i32), pltpu.VMEM((2, 2 * W), f32),
                       pltpu.VMEM((2, E), f32), pltpu.SemaphoreType.DMA((SC_BUFS,)),
                       pltpu.SemaphoreType.DMA((2,)), pltpu.SemaphoreType.DMA((2,))]
        + [pltpu.VMEM((SC_ROWS, W), jnp.uint32)] * SC_BUFS,
        compiler_params=pltpu.CompilerParams(needs_layout_passes=False))
    def act_kernel(tab_hbm, idx_hbm, hp_hbm, act_hbm, idx_v, x_v, act_v, sems, xsem, osem, *bufs):
        t0 = _sc_worker(mesh) * tpw
        pltpu.sync_copy(idx_hbm.at[pl.ds(t0 * E, tpw * E)], idx_v)
        lane = lax.iota(i32, SC_LANES)

        def x_in(tl, s):
            return pltpu.make_async_copy(hp_hbm.at[t0 + tl], x_v.at[s], xsem.at[s])

        def act_out(tl, s):
            return pltpu.make_async_copy(act_v.at[s], act_hbm.at[pl.ds((t0 + tl) * E, E)], osem.at[s])

        x_in(0, 0).start()

        def compute(b, buf):
            tl, bt = b // bpt, b % bpt
            s = tl % 2

            @pl.when(bt == 0)
            def _():
                x_in(tl, s).wait()

                @pl.when(tl + 1 < tpw)
                def _():
                    x_in(tl + 1, 1 - s).start()

                @pl.when(tl >= 2)
                def _():
                    act_out(tl - 2, s).wait()

            for g in range(SC_ROWS // SC_LANES):
                def rows_body(i, vec):
                    r = g * SC_LANES + i * RG

                    def j_body(j, accs):
                        c = pl.multiple_of(j * SC_LANES, SC_LANES)
                        x_lo, x_hi = x_v[s, pl.ds(c, SC_LANES)], x_v[s, pl.ds(W + c, SC_LANES)]
                        out = []
                        for k in range(RG):
                            lo, hi = _sc_halves(buf[r + k, pl.ds(c, SC_LANES)])
                            out.append(accs[k] + lo * x_lo + hi * x_hi)
                        return tuple(out)

                    accs = lax.fori_loop(0, W // SC_LANES, j_body,
                                         tuple(jnp.zeros((SC_LANES,), f32) for _ in range(RG)))
                    for k in range(RG):
                        vec = jnp.where(lane == i * RG + k, jnp.sum(accs[k]), vec)
                    return vec

                vec = lax.fori_loop(0, SC_LANES // RG, rows_body, jnp.zeros((SC_LANES,), f32))
                act_v[s, pl.ds(pl.multiple_of(bt * SC_ROWS + g * SC_LANES, SC_LANES), SC_LANES)] = vec

            @pl.when(bt == bpt - 1)
            def _():
                act_out(tl, s).start()

        _sc_batches(tab_hbm, idx_v, bufs, sems, tpw * bpt, compute)
        for tl in range(max(tpw - 2, 0), tpw):
            act_out(tl, tl % 2).wait()

    return act_kernel(table_u, idx.reshape(-1), hp).reshape(T, E)


def _sc_peer_sum(cfg, table_v, idx, w):
    T, E = idx.shape
    W = table_v.shape[1]
    mesh, workers = _sc_mesh()
    tpw = T // workers
    bpt = E // SC_ROWS
    assert tpw * workers == T and bpt * SC_ROWS == E and (tpw * bpt) % SC_BUFS == 0
    NQ = 4
    qw = W // NQ

    @functools.partial(
        pl.kernel, out_type=jax.ShapeDtypeStruct((T, 2 * W), f32), mesh=mesh,
        scratch_types=[pltpu.VMEM((tpw * E,), i32), pltpu.VMEM((2, E), f32), pltpu.VMEM((2, 2 * W), f32),
                       pltpu.SemaphoreType.DMA((SC_BUFS,)), pltpu.SemaphoreType.DMA((2,)),
                       pltpu.SemaphoreType.DMA((2,))] + [pltpu.VMEM((SC_ROWS, W), jnp.uint32)] * SC_BUFS,
        compiler_params=pltpu.CompilerParams(needs_layout_passes=False))
    def sum_kernel(tab_hbm, idx_hbm, w_hbm, y_hbm, idx_v, w_v, y_v, sems, wsem, osem, *bufs):
        t0 = _sc_worker(mesh) * tpw
        pltpu.sync_copy(idx_hbm.at[pl.ds(t0 * E, tpw * E)], idx_v)
        nq = qw // SC_LANES

        def w_in(tl, s):
            return pltpu.make_async_copy(w_hbm.at[pl.ds((t0 + tl) * E, E)], w_v.at[s], wsem.at[s])

        def y_out(tl, s):
            return pltpu.make_async_copy(y_v.at[s], y_hbm.at[t0 + tl], osem.at[s])

        w_in(0, 0).start()

        def compute(b, buf):
            tl, bt = b // bpt, b % bpt
            s = tl % 2

            @pl.when(bt == 0)
            def _():
                w_in(tl, s).wait()

                @pl.when(tl + 1 < tpw)
                def _():
                    w_in(tl + 1, 1 - s).start()

                @pl.when(tl >= 2)
                def _():
                    y_out(tl - 2, s).wait()

            half = jnp.full((SC_LANES,), s, i32)
            for q in range(NQ):
                cols = [q * qw + n * SC_LANES for n in range(nq)]
                zero = jnp.zeros((SC_LANES,), f32)
                init = tuple(jnp.where(bt == 0, zero, y_v[s, pl.ds(c, SC_LANES)]) for c in cols) + \
                    tuple(jnp.where(bt == 0, zero, y_v[s, pl.ds(W + c, SC_LANES)]) for c in cols)

                def r_body(r, accs):
                    wr = plsc.load_gather(w_v, [half, jnp.full((SC_LANES,), bt * SC_ROWS + r, i32)])
                    lo_acc, hi_acc = list(accs[:nq]), list(accs[nq:])
                    for n, c in enumerate(cols):
                        lo, hi = _sc_halves(buf[r, pl.ds(c, SC_LANES)])
                        lo_acc[n] = lo_acc[n] + lo * wr
                        hi_acc[n] = hi_acc[n] + hi * wr
                    return tuple(lo_acc) + tuple(hi_acc)

                accs = lax.fori_loop(0, SC_ROWS, r_body, init)
                for n, c in enumerate(cols):
                    y_v[s, pl.ds(c, SC_LANES)] = accs[n]
                    y_v[s, pl.ds(W + c, SC_LANES)] = accs[nq + n]

            @pl.when(bt == bpt - 1)
            def _():
                y_out(tl, s).start()

        _sc_batches(tab_hbm, idx_v, bufs, sems, tpw * bpt, compute)
        for tl in range(max(tpw - 2, 0), tpw):
            y_out(tl, tl % 2).wait()

    return sum_kernel(table_v, idx.reshape(-1), w.reshape(-1))


def _pack_bf16_halves(a):
    h = a.shape[1] // 2
    bits = lambda t: lax.bitcast_convert_type(t.astype(bf16), jnp.uint16).astype(jnp.uint32)
    return bits(a[:, :h]) | (bits(a[:, h:]) << 16)


def _split_bf16(a):
    hi = a.astype(bf16)
    lo = (a - hi.astype(f32)).astype(bf16)
    return jnp.concatenate([hi, lo], axis=0)


def _fold_rows(a):
    n = a.shape[0] // 2
    return a[:n] + a[n:]


def _apply_group(h2, gates, words, sel, selt, gn, gf):
    R, D = h2.shape
    nw = D // 2
    nz = 2 * words.shape[0]
    lane = lax.broadcasted_iota(i32, (2 * R, nz), 1)
    rowi = lax.broadcasted_iota(i32, (2 * R, nz), 0)
    mine = ((lane // 2) % R == rowi % R) & (lane % 2 == rowi // R)
    zu = pltpu.bitcast(words[:, :nw], bf16)
    zv = pltpu.bitcast(words[:, nw:], bf16)
    x = _rms(h2, gn)
    xs = _split_bf16(jnp.concatenate([x[:, :nw], x[:, nw:]], axis=0))
    a = _fold_rows(lax.dot_general(xs, zu, (((1,), (1,)), ((), ())), preferred_element_type=f32))
    a = jnp.where(mine, a, 0.0)
    a = a[:R] + a[R:]
    act = _fold_rows(jnp.dot(_split_bf16(a), sel, preferred_element_type=f32))
    w = gates * _gelu(act)
    w_rows = _fold_rows(jnp.dot(_split_bf16(w), selt, preferred_element_type=f32))
    ws = _split_bf16(jnp.where(mine, jnp.concatenate([w_rows, w_rows], axis=0), 0.0))
    y2 = _fold_rows(jnp.dot(ws, zv, preferred_element_type=f32))
    y = jnp.concatenate([y2[:R], y2[R:]], axis=1)
    return _rms(h2 + y, gf)


def _peer_direct_kernel(idx_ref, idxn_ref, h_ref, gt_ref, tab_ref, sel_ref, selt_ref, gn_ref, gf_ref, o_ref,
                        buf_a, buf_b, sem_ref, *, cfg):
    E, R = cfg.slots, SUBLANES
    n_rows = E * R
    i = pl.program_id(0)

    def start_row(src_idx_ref, base, r, buf, s, priority=0):
        pltpu.async_copy(tab_ref.at[pl.ds(src_idx_ref[base + r], 1)], buf.at[pl.ds(r, 1)], sem_ref.at[s],
                         priority=priority)

    def wait_rows(buf, s):
        pltpu.make_async_copy(tab_ref.at[pl.ds(0, n_rows)], buf, sem_ref.at[s]).wait()

    def group(gi, buf):
        tok = slice(gi * R, (gi + 1) * R)
        o_ref[tok, :] = _apply_group(h_ref[tok, :], gt_ref[tok, :], buf[...], sel_ref[...], selt_ref[...],
                                     gn_ref[...], gf_ref[...])

    @pl.when(i == 0)
    def _():
        def body(r, carry):
            start_row(idx_ref, 0, r, buf_a, 0)
            return carry
        lax.fori_loop(0, n_rows, body, 0, unroll=8)

    wait_rows(buf_a, 0)
    for r in range(n_rows):
        start_row(idx_ref, n_rows, r, buf_b, 1, priority=r % 2)
    group(0, buf_a)
    wait_rows(buf_b, 1)
    for r in range(n_rows):
        start_row(idxn_ref, 0, r, buf_a, 0, priority=r % 2)
    group(1, buf_b)

    @pl.when(i == pl.num_programs(0) - 1)
    def _():
        wait_rows(buf_a, 0)


def _peer_direct(cfg, h2, gates, idx, table, gn, gf):
    D = h2.shape[1]
    E, R = cfg.slots, SUBLANES
    T = idx.shape[0] // E
    n_rows = E * R
    nz = 2 * n_rows
    n = T // (2 * R)
    sel = (jnp.arange(nz)[:, None] // (2 * R) == jnp.arange(E)[None, :]).astype(bf16)
    row = lambda i: (i, 0)
    smem = functools.partial(pl.BlockSpec, (2 * n_rows,), memory_space=pltpu.SMEM)
    return pl.pallas_call(
        functools.partial(_peer_direct_kernel, cfg=cfg),
        grid=(n,),
        in_specs=[smem(lambda i: (i,)), smem(lambda i: (jnp.minimum(i + 1, n - 1),)),
                  pl.BlockSpec((2 * R, D), row), pl.BlockSpec((2 * R, E), row), pl.BlockSpec(memory_space=pl.ANY),
                  _resident((nz, E)), _resident((E, nz)), _resident((1, D)), _resident((1, D))],
        out_specs=pl.BlockSpec((2 * R, D), row),
        out_shape=jax.ShapeDtypeStruct((T, D), f32),
        scratch_shapes=[pltpu.VMEM((n_rows, D), table.dtype), pltpu.VMEM((n_rows, D), table.dtype),
                        pltpu.SemaphoreType.DMA((2,))],
        compiler_params=pltpu.CompilerParams(dimension_semantics=("arbitrary",), vmem_limit_bytes=VMEM_LIMIT,
                                             disable_bounds_checks=True),
        name="peer_direct",
    )(idx, idx, h2, gates, table, sel, sel.T, gn, gf)


def _rowwise(name, body, out_cols, T, row0, *arrays, rows_per_step=128):
    tm = min(rows_per_step, T)
    assert T % tm == 0 and row0 % tm == 0

    def spec(a):
        if a.shape[0] == 1:
            return _resident(a.shape)
        off = row0 // tm if a.shape[0] > T else 0
        return pl.BlockSpec((tm, a.shape[1]), lambda i: (i + off, 0))

    def kern(*refs):
        refs[-1][...] = body(*(r[...] for r in refs[:-1]))

    return pl.pallas_call(
        kern, grid=(T // tm,),
        in_specs=[spec(a) for a in arrays],
        out_specs=pl.BlockSpec((tm, out_cols), lambda i: (i, 0)),
        out_shape=jax.ShapeDtypeStruct((T, out_cols), f32),
        compiler_params=_cparams("parallel"), name=name,
    )(*arrays)


def _peer_pre(cfg, h2, row0, gn):
    return _rowwise("peer_pre", _rms, h2.shape[1], h2.shape[0] - row0, row0, h2, gn)


def _peer_mid(cfg, gates, row0, act):
    return _rowwise("peer_mid", lambda g, a: g * _gelu(a), gates.shape[1], act.shape[0], row0, gates, act)


def _peer_post(cfg, h2, row0, y, gf):
    return _rowwise("peer_post", lambda h, yy, g: _rms(h + yy, g), h2.shape[1], y.shape[0], row0, h2, y, gf)


def _block(cfg, sc_fns, x, mem, positions, mix_norm_g, w_in, lam_q1, lam_k1, lam_q2, lam_k2, attn_head_g,
           ssm_a_re, ssm_a_im, ssm_log_dt, ssm_b_re, ssm_b_im, ssm_c_re, ssm_c_im, ssm_d, glu_w, glu_b,
           ssm_out_g, w_out, xattn_norm_g, mem_norm_g, xattn_wq, xattn_wkv, xattn_wo, ffn_norm_g,
           peer_wq, peer_k1, peer_k2, peer_u, peer_v, final_norm_g):
    B, S, D = x.shape
    T = B * S
    l = 0
    row = lambda a: a.reshape(1, -1)
    x2 = x.reshape(T, D)
    pos = positions.reshape(T, 1).astype(f32)
    freqs = cfg.rope_theta ** (-jnp.arange(0, cfg.rot_dim, 2, dtype=f32) / cfg.rot_dim)
    lane = jnp.arange(LANES) % cfg.diff_qkdim
    freq_row = jnp.where(lane < cfg.rot_dim, freqs[lane % (cfg.rot_dim // 2)], 0.0).reshape(1, LANES)

    lam_p = jnp.stack([lam_q1[l], lam_k1[l], lam_q2[l], lam_k2[l]])
    s5_params = _s5_params(cfg, ssm_a_re[l], ssm_a_im[l], ssm_log_dt[l], ssm_b_re[l], ssm_b_im[l],
                           ssm_c_re[l], ssm_c_im[l], ssm_d[l])
    ns, _, wn = s5_params[0].shape
    zero_state = jnp.zeros((ns, B, wn), f32)
    state = (zero_state, zero_state)
    w_in_b, glu_w_b, w_out_b = w_in[l].astype(bf16), glu_w[l].astype(bf16), w_out[l].astype(bf16)
    wq_b, wo_b, pw_b = xattn_wq[l].astype(bf16), xattn_wo[l].astype(bf16), peer_wq[l].astype(bf16)
    k1_b, k2_b = peer_k1[l].astype(bf16), peer_k2[l].astype(bf16)
    kv = _mem_kv(cfg, mem.reshape(B * cfg.n_mem, D), row(mem_norm_g[l]), xattn_wkv[l].astype(bf16))
    table_u, table_v = _pack_bf16_halves(peer_u[l]), _pack_bf16_halves(peer_v[l])
    table = jnp.concatenate([table_u, table_v], axis=1)
    E, R = cfg.slots, SUBLANES
    Lc = cfg.chunk
    Tc = B * Lc
    td = Tc * cfg.direct_sixteenths // 16
    sc_act, sc_sum = sc_fns

    def after(a, tokens):
        return lax.optimization_barrier((a,) + tuple(tokens))[0] if tokens else a

    def dense(c, qks, vs, state, tokens):
        qk, v, u = _in_proj(cfg, c, x2, pos, after(row(mix_norm_g[l]), tokens), freq_row, w_in_b)
        qks, vs = qks + [qk], vs + [v]
        att = _diff_attn(cfg, lam_p, qks, vs, row(attn_head_g[l]))
        y, state = _s5_scan(cfg, u.reshape(Tc, cfg.ssm_width), s5_params, state)
        ssm = _s5_glu(cfg, y, glu_w_b, row(glu_b[l]), row(ssm_out_g[l]))
        h1, hn = _out_proj(cfg, c, x2, att, ssm.reshape(Lc, B * cfg.ssm_width), w_out_b, row(xattn_norm_g[l]))
        h2, qp = _xattn(cfg, h1, hn, kv, wq_b, wo_b, row(ffn_norm_g[l]), pw_b)
        experts_t, gates_t = _peer_route(cfg, qp, k1_b, k2_b)
        return qks, vs, state, h2, experts_t, gates_t.T

    gn, gf = row(ffn_norm_g[l]), row(final_norm_g)
    qks, vs, outs, tokens, pending = [], [], [], [], None
    for c in range(cfg.n_chunks):
        qks, vs, state, h2, experts_t, gates = dense(c, qks, vs, state, tokens)
        parts, tokens, before_direct = [], [], []
        if td < Tc:
            idx_s = experts_t.T[td:]
            hp = _peer_pre(cfg, h2, td, gn)
            before_direct.append(hp)
        if pending is not None:
            p_h2, p_y, p_parts = pending
            p_parts.append(_peer_post(cfg, p_h2, td, p_y, after(gf, before_direct)))
            before_direct.append(p_parts[-1])
            outs.append(jnp.concatenate(p_parts, axis=0).reshape(B, Lc, D))
            pending = None
        if td:
            idx_d = experts_t[:, :td].reshape(E, td // R, R).transpose(1, 0, 2).reshape(-1)
            parts.append(_peer_direct(cfg, h2, gates, idx_d, table, after(gn, before_direct), gf))
        if td < Tc:
            act = sc_act(table_u, idx_s, hp)
            w = _peer_mid(cfg, after(gates, parts), td, act)
            tokens = [w]
            pending = (h2, sc_sum(table_v, idx_s, w), parts)
        else:
            outs.append(jnp.concatenate(parts, axis=0).reshape(B, Lc, D))
    if pending is not None:
        p_h2, p_y, p_parts = pending
        p_parts.append(_peer_post(cfg, p_h2, td, p_y, gf))
        outs.append(jnp.concatenate(p_parts, axis=0).reshape(B, Lc, D))
    return jnp.concatenate(outs, axis=1)


def kernel(x, mem, positions, mix_norm_g, w_in, lam_q1, lam_k1, lam_q2, lam_k2, attn_head_g, ssm_a_re, ssm_a_im, ssm_log_dt, ssm_b_re, ssm_b_im, ssm_c_re, ssm_c_im, ssm_d, glu_w, glu_b, ssm_out_g, w_out, xattn_norm_g, mem_norm_g, xattn_wq, xattn_wkv, xattn_wo, ffn_norm_g, peer_wq, peer_k1, peer_k2, peer_u, peer_v, final_norm_g):
    cfg = Cfg()
    return _block(cfg, (functools.partial(_sc_peer_act, cfg), functools.partial(_sc_peer_sum, cfg)), x, mem, positions, mix_norm_g, w_in, lam_q1, lam_k1,
                  lam_q2, lam_k2, attn_head_g, ssm_a_re, ssm_a_im, ssm_log_dt, ssm_b_re, ssm_b_im, ssm_c_re,
                  ssm_c_im, ssm_d, glu_w, glu_b, ssm_out_g, w_out, xattn_norm_g, mem_norm_g, xattn_wq,
                  xattn_wkv, xattn_wo, ffn_norm_g, peer_wq, peer_k1, peer_k2, peer_u, peer_v, final_norm_g)
```

```python
import dataclasses
import functools
import math

import jax
import jax.numpy as jnp
from jax import lax
from jax.experimental import pallas as pl
from jax.experimental.pallas import tpu as pltpu
from jax.experimental.pallas import tpu_sc as plsc

f32 = jnp.float32
bf16 = jnp.bfloat16
i32 = jnp.int32

LANES = 128
SUBLANES = 8
VMEM_LIMIT = 56 * 1024 * 1024
NEG = -1e30
EPS = 1e-6


@dataclasses.dataclass(frozen=True)
class Cfg:
    d_model: int = 2048
    batch: int = 8
    seq: int = 2048
    n_mem: int = 256
    diff_heads: int = 8
    ssm_group: int = 16
    ssm_state: int = 64
    xattn_heads: int = 4
    xattn_head_dim: int = 128
    peer_heads: int = 8
    peer_keys: int = 128
    peer_qdim: int = 256
    peer_topk: int = 16
    rope_theta: float = 500000.0
    lam_init: float = 0.8 - 0.6 * math.exp(-0.3 * 0)
    chunk: int = 256
    scan_chunk: int = 128
    route_tm: int = 256
    direct_sixteenths: int = 4

    @property
    def n_chunks(self):
        return self.seq // self.chunk

    @property
    def attn_width(self):
        return self.d_model // 2

    @property
    def ssm_width(self):
        return self.d_model - self.attn_width

    @property
    def diff_vdim(self):
        return self.attn_width // self.diff_heads

    @property
    def diff_qkdim(self):
        return self.diff_vdim // 2

    @property
    def rot_dim(self):
        return self.diff_qkdim // 4

    @property
    def ssm_groups(self):
        return self.ssm_width // self.ssm_group

    @property
    def xattn_width(self):
        return self.xattn_heads * self.xattn_head_dim

    @property
    def tokens(self):
        return self.batch * self.seq

    @property
    def slots(self):
        return self.peer_heads * self.peer_topk


def _cparams(*sem):
    return pltpu.CompilerParams(dimension_semantics=sem, vmem_limit_bytes=VMEM_LIMIT)


def _resident(shape):
    nd = len(shape)
    return pl.BlockSpec(shape, lambda *_: (0,) * nd, pipeline_mode=pl.Buffered(1))


def _rms(x, g):
    return x * lax.rsqrt(jnp.mean(x * x, axis=-1, keepdims=True) + EPS) * g


def _gelu(x):
    return 0.5 * x * (1.0 + lax.erf(x * (2.0 ** -0.5)))


def _in_proj_kernel(x_ref, pos_ref, g_ref, freq_ref, w_ref, qk_ref, v_ref, u_ref, *, cfg):
    n_qk, n_v = 2 * cfg.attn_width, cfg.attn_width
    half = cfg.rot_dim // 2
    xn = _rms(x_ref[...], g_ref[...]).astype(bf16)
    ang = pos_ref[...] * freq_ref[...]
    cos, sin = jnp.cos(ang), jnp.sin(ang)
    lane = lax.broadcasted_iota(i32, (1, LANES), 1) % cfg.diff_qkdim
    sin_lo = jnp.where(lane < half, -sin, 0.0)
    sin_hi = jnp.where((lane >= half) & (lane < 2 * half), sin, 0.0)
    cw = 2 * LANES
    for c in range((n_qk + n_v + cfg.ssm_width) // cw):
        col = c * cw
        z = jnp.dot(xn, w_ref[:, col:col + cw], preferred_element_type=f32)
        if col < n_qk:
            for k in range(cw // LANES):
                zk = z[:, k * LANES:(k + 1) * LANES]
                zk = zk * cos + pltpu.roll(zk, LANES - half, 1) * sin_lo + pltpu.roll(zk, half, 1) * sin_hi
                qk_ref[:, col + k * LANES:col + (k + 1) * LANES] = zk.astype(bf16)
        elif col < n_qk + n_v:
            v_ref[:, col - n_qk:col - n_qk + cw] = z.astype(bf16)
        else:
            u_ref[:, col - n_qk - n_v:col - n_qk - n_v + cw] = z


def _chunk_rows(cfg, c):
    return lambda b: (b * cfg.n_chunks + c, 0)


def _in_proj(cfg, c, x2, pos, g, freq, w):
    D = x2.shape[1]
    B, tm = cfg.batch, cfg.chunk
    n_qk, n_v, n_u = 2 * cfg.attn_width, cfg.attn_width, cfg.ssm_width
    row = lambda b: (b, 0)
    return pl.pallas_call(
        functools.partial(_in_proj_kernel, cfg=cfg),
        grid=(B,),
        in_specs=[pl.BlockSpec((tm, D), _chunk_rows(cfg, c)), pl.BlockSpec((tm, 1), _chunk_rows(cfg, c)),
                  _resident((1, D)), _resident((1, LANES)), _resident(w.shape)],
        out_specs=[pl.BlockSpec((tm, n_qk), row), pl.BlockSpec((tm, n_v), row),
                   pl.BlockSpec((tm, n_u), lambda b: (0, b))],
        out_shape=[jax.ShapeDtypeStruct((B * tm, n_qk), bf16), jax.ShapeDtypeStruct((B * tm, n_v), bf16),
                   jax.ShapeDtypeStruct((tm, B * n_u), f32)],
        compiler_params=_cparams("parallel"),
        name="in_proj",
    )(x2, pos, g, freq, w)


def _diff_attn_kernel(lam_ref, q_ref, *rest, cfg, n_kv):
    k_refs, v_refs = rest[:n_kv], rest[n_kv:2 * n_kv]
    g_ref, o_ref = rest[2 * n_kv:]
    tq = cfg.chunk
    d = cfg.diff_qkdim
    lane = lax.broadcasted_iota(i32, (1, LANES), 1)
    q = q_ref[...].astype(f32) * (d ** -0.5)
    qs = (jnp.where(lane < d, q, 0.0).astype(bf16), jnp.where(lane >= d, q, 0.0).astype(bf16))
    causal = (lax.broadcasted_iota(i32, (tq, tq), 1) <= lax.broadcasted_iota(i32, (tq, tq), 0))
    tiles = lambda a: [a[:, t * LANES:(t + 1) * LANES] for t in range(tq // LANES)]

    def scores(c, j):
        s = lax.dot_general(qs[c], k_refs[j][...], (((1,), (1,)), ((), ())), preferred_element_type=f32)
        return jnp.where(causal, s, NEG) if j == n_kv - 1 else s

    outs = []
    for c in range(2):
        mm = jnp.full((tq, LANES), NEG, f32)
        for j in range(n_kv):
            for st in tiles(scores(c, j)):
                mm = jnp.maximum(mm, st)
        mb = jnp.broadcast_to(jnp.max(mm, axis=1, keepdims=True), (tq, LANES))
        ls = jnp.zeros((tq, LANES), f32)
        acc = jnp.zeros((tq, LANES), f32)
        for j in range(n_kv):
            ps = [jnp.exp(st - mb) for st in tiles(scores(c, j))]
            for p in ps:
                ls = ls + p
            acc = acc + jnp.dot(jnp.concatenate(ps, axis=1).astype(bf16), v_refs[j][...],
                                preferred_element_type=f32)
        outs.append(acc / jnp.sum(ls, axis=1, keepdims=True))

    lp = lam_ref[...]
    lam = (jnp.exp(jnp.sum(lp[0:1] * lp[1:2], axis=1, keepdims=True))
           - jnp.exp(jnp.sum(lp[2:3] * lp[3:4], axis=1, keepdims=True)) + cfg.lam_init)
    o = outs[0] - lam * outs[1]
    o_ref[...] = (_rms(o, g_ref[...]) * (1.0 - cfg.lam_init)).astype(o_ref.dtype)


def _diff_attn(cfg, lam_p, qks, vs, g):
    n_kv = len(qks)
    B, H, tq = cfg.batch, cfg.diff_heads, cfg.chunk
    head = lambda b, h: (b, h)
    key = lambda b, h: (b, H + h)
    return pl.pallas_call(
        functools.partial(_diff_attn_kernel, cfg=cfg, n_kv=n_kv),
        grid=(B, H),
        in_specs=([_resident(lam_p.shape), pl.BlockSpec((tq, LANES), head)]
                  + [pl.BlockSpec((tq, LANES), key)] * n_kv + [pl.BlockSpec((tq, LANES), head)] * n_kv
                  + [_resident((1, LANES))]),
        out_specs=pl.BlockSpec((tq, LANES), head),
        out_shape=jax.ShapeDtypeStruct((B * tq, cfg.attn_width), bf16),
        compiler_params=_cparams("parallel", "parallel"),
        name="diff_attn",
    )(lam_p, qks[-1], *qks, *vs, g)


def _s5_scan_kernel(u_ref, are_ref, aim_ref, ldt_ref, bre_ref, bim_ref, cre_ref, cim_ref, d_ref,
                    sin_re_ref, sin_im_ref, y_ref, sout_re_ref, sout_im_ref,
                    ab_ref, bbar_re_ref, bbar_im_ref, bu_re_ref, bu_im_ref, *, cfg):
    B, L = cfg.batch, cfg.scan_chunk
    t = pl.program_id(1)
    st_re_ref, st_im_ref = sout_re_ref.at[0], sout_im_ref.at[0]

    @pl.when(t == 0)
    def _():
        dt = jnp.exp(ldt_ref[0])
        lre, lim = are_ref[0], aim_ref[0]
        mag = jnp.exp(lre * dt)
        ang = lim * dt
        ab_re, ab_im = mag * jnp.cos(ang), mag * jnp.sin(ang)
        den = lre * lre + lim * lim
        f_re = ((ab_re - 1.0) * lre + ab_im * lim) / den
        f_im = (ab_im * lre - (ab_re - 1.0) * lim) / den
        ab_ref[0] = jnp.broadcast_to(ab_re, ab_ref.shape[1:])
        ab_ref[1] = jnp.broadcast_to(ab_im, ab_ref.shape[1:])
        br, bi = bre_ref[0], bim_ref[0]
        bbar_re_ref[...] = (f_re * br - f_im * bi).astype(bf16)
        bbar_im_ref[...] = (f_re * bi + f_im * br).astype(bf16)
        st_re_ref[...] = sin_re_ref[0]
        st_im_ref[...] = sin_im_ref[0]

    u = u_ref[...]
    ub = u.astype(bf16)
    bu_re_ref[...] = jnp.dot(ub, bbar_re_ref[...], preferred_element_type=f32)
    bu_im_ref[...] = jnp.dot(ub, bbar_im_ref[...], preferred_element_type=f32)
    a_re, a_im = ab_ref[0], ab_ref[1]

    def body(s, carry):
        xr, xi = carry
        rows = pl.ds(pl.multiple_of(s * B, B), B)
        nr = a_re * xr - a_im * xi + bu_re_ref[rows, :]
        ni = a_re * xi + a_im * xr + bu_im_ref[rows, :]
        bu_re_ref[rows, :] = nr
        bu_im_ref[rows, :] = ni
        return nr, ni

    xr, xi = lax.fori_loop(0, L, body, (st_re_ref[...], st_im_ref[...]), unroll=8)
    st_re_ref[...] = xr
    st_im_ref[...] = xi

    y_ref[...] = (jnp.dot(bu_re_ref[...].astype(bf16), cre_ref[0].astype(bf16), preferred_element_type=f32)
                  - jnp.dot(bu_im_ref[...].astype(bf16), cim_ref[0].astype(bf16), preferred_element_type=f32)
                  + d_ref[0] * u)


def _block_diag(w, blocks):
    G, r, c = w.shape
    w4 = w.reshape(G // blocks, blocks, r, c)
    eye = jnp.eye(blocks, dtype=w.dtype)
    out = w4[:, :, :, None, :] * eye[None, :, None, :, None]
    return out.reshape(G // blocks, blocks * r, blocks * c)


def _s5_params(cfg, a_re, a_im, log_dt, b_re, b_im, c_re, c_im, d_skip):
    G, P, Hc = cfg.ssm_groups, cfg.ssm_state, cfg.ssm_group
    gps = min(G, 2 * LANES // Hc)
    ns = G // gps
    wu, wn = gps * Hc, gps * P
    return (a_re.reshape(ns, 1, wn), a_im.reshape(ns, 1, wn), jnp.repeat(log_dt, P).reshape(ns, 1, wn),
            _block_diag(b_re.transpose(0, 2, 1), gps), _block_diag(b_im.transpose(0, 2, 1), gps),
            _block_diag(c_re.transpose(0, 2, 1), gps), _block_diag(c_im.transpose(0, 2, 1), gps),
            d_skip.reshape(ns, 1, wu))


def _s5_scan(cfg, u2, params, state):
    B, L = cfg.batch, cfg.scan_chunk
    rows, W = u2.shape
    ns, _, wn = params[0].shape
    wu = params[-1].shape[2]
    slab = lambda s, t: (s, 0, 0)
    st = pl.BlockSpec((1, B, wn), slab)
    st_shape = jax.ShapeDtypeStruct((ns, B, wn), f32)
    y, s_re, s_im = pl.pallas_call(
        functools.partial(_s5_scan_kernel, cfg=cfg),
        grid=(ns, rows // (L * B)),
        in_specs=[pl.BlockSpec((L * B, wu), lambda s, t: (t, s)),
                  pl.BlockSpec((1, 1, wn), slab), pl.BlockSpec((1, 1, wn), slab), pl.BlockSpec((1, 1, wn), slab),
                  pl.BlockSpec((1, wu, wn), slab), pl.BlockSpec((1, wu, wn), slab),
                  pl.BlockSpec((1, wn, wu), slab), pl.BlockSpec((1, wn, wu), slab),
                  pl.BlockSpec((1, 1, wu), slab), st, st],
        out_specs=[pl.BlockSpec((L * B, wu), lambda s, t: (t, s)), st, st],
        out_shape=[jax.ShapeDtypeStruct((rows, W), f32), st_shape, st_shape],
        scratch_shapes=[pltpu.VMEM((2, B, wn), f32),
                        pltpu.VMEM((wu, wn), bf16), pltpu.VMEM((wu, wn), bf16),
                        pltpu.VMEM((L * B, wn), f32), pltpu.VMEM((L * B, wn), f32)],
        compiler_params=_cparams("arbitrary", "arbitrary"),
        name="s5_scan",
    )(u2, *params, *state)
    return y, (s_re, s_im)


def _s5_glu_kernel(y_ref, w_ref, b_ref, g_ref, o_ref):
    g = _gelu(y_ref[...])
    z = jnp.dot(g.astype(bf16), w_ref[...], preferred_element_type=f32) + b_ref[...]
    g = g * jax.nn.sigmoid(z)
    o_ref[...] = _rms(g, g_ref[...]).astype(o_ref.dtype)


def _s5_glu(cfg, y2, w, b, g):
    T, W = y2.shape
    tm = cfg.chunk
    row = lambda i: (i, 0)
    return pl.pallas_call(
        _s5_glu_kernel,
        grid=(T // tm,),
        in_specs=[pl.BlockSpec((tm, W), row), _resident(w.shape), _resident((1, W)), _resident((1, W))],
        out_specs=pl.BlockSpec((tm, W), row),
        out_shape=jax.ShapeDtypeStruct((T, W), bf16),
        compiler_params=_cparams("parallel"),
        name="s5_glu",
    )(y2, w, b, g)


def _out_proj_kernel(x_ref, att_ref, ssm_ref, w_ref, g_ref, h_ref, hn_ref, *, cfg):
    aw = cfg.attn_width
    h = (x_ref[...]
         + jnp.dot(att_ref[...], w_ref[:aw, :], preferred_element_type=f32)
         + jnp.dot(ssm_ref[...], w_ref[aw:, :], preferred_element_type=f32))
    h_ref[...] = h
    hn_ref[...] = _rms(h, g_ref[...]).astype(hn_ref.dtype)


def _out_proj(cfg, c, x2, att, ssm, w, g):
    D = x2.shape[1]
    tm = cfg.chunk
    T = cfg.batch * tm
    row = lambda i: (i, 0)
    return pl.pallas_call(
        functools.partial(_out_proj_kernel, cfg=cfg),
        grid=(T // tm,),
        in_specs=[pl.BlockSpec((tm, D), _chunk_rows(cfg, c)), pl.BlockSpec((tm, cfg.attn_width), row),
                  pl.BlockSpec((tm, cfg.ssm_width), lambda b: (0, b)), _resident(w.shape), _resident((1, D))],
        out_specs=[pl.BlockSpec((tm, D), row), pl.BlockSpec((tm, D), row)],
        out_shape=[jax.ShapeDtypeStruct((T, D), f32), jax.ShapeDtypeStruct((T, D), bf16)],
        compiler_params=_cparams("parallel"),
        name="out_proj",
    )(x2, att, ssm, w, g)


def _mem_kv_kernel(m_ref, g_ref, w_ref, o_ref):
    mn = _rms(m_ref[...], g_ref[...]).astype(bf16)
    o_ref[...] = jnp.dot(mn, w_ref[...], preferred_element_type=f32).astype(o_ref.dtype)


def _mem_kv(cfg, mem2, g, w):
    R, D = mem2.shape
    tm = cfg.n_mem
    row = lambda i: (i, 0)
    return pl.pallas_call(
        _mem_kv_kernel,
        grid=(R // tm,),
        in_specs=[pl.BlockSpec((tm, D), row), _resident((1, D)), _resident(w.shape)],
        out_specs=pl.BlockSpec((tm, w.shape[1]), row),
        out_shape=jax.ShapeDtypeStruct((R, w.shape[1]), bf16),
        compiler_params=_cparams("parallel"),
        name="mem_kv",
    )(mem2, g, w)


def _xattn_kernel(h_ref, hn_ref, kv_ref, wq_ref, wo_ref, g_ref, pw_ref, h2_ref, qp_ref, *, cfg):
    nh, hd, xw = cfg.xattn_heads, cfg.xattn_head_dim, cfg.xattn_width
    q = jnp.dot(hn_ref[...], wq_ref[...], preferred_element_type=f32).astype(bf16)
    outs = []
    for h in range(nh):
        qh = q[:, h * hd:(h + 1) * hd]
        kh = kv_ref[:, h * hd:(h + 1) * hd]
        vh = kv_ref[:, xw + h * hd:xw + (h + 1) * hd]
        s = lax.dot_general(qh, kh, (((1,), (1,)), ((), ())), preferred_element_type=f32) * (hd ** -0.5)
        p = jnp.exp(s - jnp.max(s, axis=1, keepdims=True))
        p = p / jnp.sum(p, axis=1, keepdims=True)
        outs.append(jnp.dot(p.astype(bf16), vh, preferred_element_type=f32).astype(bf16))
    o = jnp.concatenate(outs, axis=1)
    h2 = h_ref[...] + jnp.dot(o, wo_ref[...], preferred_element_type=f32)
    h2_ref[...] = h2
    hp = _rms(h2, g_ref[...]).astype(bf16)
    qp_ref[...] = jnp.dot(hp, pw_ref[...], preferred_element_type=f32).astype(qp_ref.dtype)


def _xattn(cfg, h1, hn, kv, wq, wo, g, pw):
    T, D = h1.shape
    tm, M = cfg.chunk, cfg.n_mem
    row = lambda i: (i, 0)
    return pl.pallas_call(
        functools.partial(_xattn_kernel, cfg=cfg),
        grid=(T // tm,),
        in_specs=[pl.BlockSpec((tm, D), row), pl.BlockSpec((tm, D), row),
                  pl.BlockSpec((M, kv.shape[1]), row),
                  _resident(wq.shape), _resident(wo.shape), _resident((1, D)), _resident(pw.shape)],
        out_specs=[pl.BlockSpec((tm, D), row), pl.BlockSpec((tm, pw.shape[1]), row)],
        out_shape=[jax.ShapeDtypeStruct((T, D), f32), jax.ShapeDtypeStruct((T, pw.shape[1]), bf16)],
        compiler_params=_cparams("parallel"),
        name="xattn",
    )(h1, hn, kv, wq, wo, g, pw)


def _top_rows(s, k, payload=None):
    n = s.shape[0]
    rows = lax.broadcasted_iota(i32, s.shape, 0)
    vals, picks = [], []
    for _ in range(k):
        m = jnp.max(s, axis=0, keepdims=True)
        idx = jnp.min(jnp.where(s == m, rows, n), axis=0, keepdims=True)
        sel = rows == idx
        vals.append(m)
        picks.append(idx if payload is None else jnp.max(jnp.where(sel, payload, -1), axis=0, keepdims=True))
        s = jnp.where(sel, NEG, s)
    return vals, picks


def _peer_route_kernel(qp_ref, k1_ref, k2_ref, e_ref, g_ref, *, cfg):
    K, nk, half = cfg.peer_topk, cfg.peer_keys, cfg.peer_qdim // 2
    tm = qp_ref.shape[0]
    dn = (((1,), (1,)), ((), ()))
    s1 = lax.dot_general(k1_ref[0], qp_ref[:, :half], dn, preferred_element_type=f32)
    s2 = lax.dot_general(k2_ref[0], qp_ref[:, half:], dn, preferred_element_type=f32)
    t1, i1 = _top_rows(s1, K)
    t2, i2 = _top_rows(s2, K)
    t1, i1 = jnp.concatenate(t1, axis=0), jnp.concatenate(i1, axis=0)
    t2, i2 = jnp.concatenate(t2, axis=0), jnp.concatenate(i2, axis=0)
    cand = jnp.concatenate([t1[0:1] + t2] + [t1[a:a + 1] + t2[:K // 2] for a in range(1, K)], axis=0)
    expert = jnp.concatenate([i1[0:1] * nk + i2] + [i1[a:a + 1] * nk + i2[:K // 2] for a in range(1, K)], axis=0)
    ts, es = _top_rows(cand, K, payload=expert)
    ts = jnp.concatenate(ts, axis=0)
    p = jnp.exp(ts - ts[0:1])
    g_ref[...] = p / jnp.sum(p, axis=0, keepdims=True)
    e_ref[...] = jnp.concatenate(es, axis=0)


def _peer_route(cfg, qp, k1, k2):
    T = qp.shape[0]
    tm, H, K, Q = cfg.route_tm, cfg.peer_heads, cfg.peer_topk, cfg.peer_qdim
    out = pl.BlockSpec((K, tm), lambda i, h: (h, i))
    return pl.pallas_call(
        functools.partial(_peer_route_kernel, cfg=cfg),
        grid=(T // tm, H),
        in_specs=[pl.BlockSpec((tm, Q), lambda i, h: (i, h)),
                  pl.BlockSpec((1,) + k1.shape[1:], lambda i, h: (h, 0, 0)),
                  pl.BlockSpec((1,) + k2.shape[1:], lambda i, h: (h, 0, 0))],
        out_specs=[out, out],
        out_shape=[jax.ShapeDtypeStruct((H * K, T), i32), jax.ShapeDtypeStruct((H * K, T), f32)],
        compiler_params=_cparams("parallel", "parallel"),
        name="peer_route",
    )(qp, k1, k2)


SC_LANES = 16
SC_ROWS = 32
SC_BUFS = 3


def _sc_mesh():
    mesh = plsc.VectorSubcoreMesh(core_axis_name="core", subcore_axis_name="subcore")
    return mesh, mesh.num_cores * mesh.num_subcores


def _sc_worker(mesh):
    return lax.axis_index("core") * mesh.num_subcores + lax.axis_index("subcore")


def _sc_halves(words):
    return plsc.bitcast(words << 16, f32), plsc.bitcast(words & jnp.uint32(0xFFFF0000), f32)


def _sc_batches(tab_hbm, idx_v, bufs, sems, n_batches, compute):
    nb = len(bufs)

    def gather(b, s):
        return pltpu.make_async_copy(tab_hbm.at[idx_v.at[pl.ds(b * SC_ROWS, SC_ROWS)]], bufs[s], sems.at[s])

    def step(b, s, when):
        gather(b, s).wait()
        when(b + nb - 1 < n_batches, lambda: gather(b + nb - 1, (s - 1) % nb).start())
        compute(b, bufs[s])

    for s in range(min(nb - 1, n_batches)):
        gather(s, s).start()
    main = n_batches // nb * nb

    @pl.loop(0, main, step=nb)
    def _(b0):
        for s in range(nb):
            step(b0 + s, s, lambda cond, fn: pl.when(cond)(fn))

    for b in range(main, n_batches):
        step(b, b % nb, lambda cond, fn: fn() if cond else None)


def _sc_peer_act(cfg, table_u, idx, hp):
    T, E = idx.shape
    W = table_u.shape[1]
    mesh, workers = _sc_mesh()
    tpw = T // workers
    bpt = E // SC_ROWS
    assert tpw * workers == T and bpt * SC_ROWS == E and (tpw * bpt) % SC_BUFS == 0
    RG = 4

    @functools.partial(
        pl.kernel, out_type=jax.ShapeDtypeStruct((T * E,), f32), mesh=mesh,
        scratch_types=[pltpu.VMEM((tpw * E,), i32), pltpu.VMEM((2, 2 * W), f32),
                       pltpu.VMEM((2, E), f32), pltpu.SemaphoreType.DMA((SC_BUFS,)),
                       pltpu.SemaphoreType.DMA((2,)), pltpu.SemaphoreType.DMA((2,))]
        + [pltpu.VMEM((SC_ROWS, W), jnp.uint32)] * SC_BUFS,
        compiler_params=pltpu.CompilerParams(needs_layout_passes=False))
    def act_kernel(tab_hbm, idx_hbm, hp_hbm, act_hbm, idx_v, x_v, act_v, sems, xsem, osem, *bufs):
        t0 = _sc_worker(mesh) * tpw
        pltpu.sync_copy(idx_hbm.at[pl.ds(t0 * E, tpw * E)], idx_v)
        lane = lax.iota(i32, SC_LANES)

        def x_in(tl, s):
            return pltpu.make_async_copy(hp_hbm.at[t0 + tl], x_v.at[s], xsem.at[s])

        def act_out(tl, s):
            return pltpu.make_async_copy(act_v.at[s], act_hbm.at[pl.ds((t0 + tl) * E, E)], osem.at[s])

        x_in(0, 0).start()

        def compute(b, buf):
            tl, bt = b // bpt, b % bpt
            s = tl % 2

            @pl.when(bt == 0)
            def _():
                x_in(tl, s).wait()

                @pl.when(tl + 1 < tpw)
                def _():
                    x_in(tl + 1, 1 - s).start()

                @pl.when(tl >= 2)
                def _():
                    act_out(tl - 2, s).wait()

            for g in range(SC_ROWS // SC_LANES):
                def rows_body(i, vec):
                    r = g * SC_LANES + i * RG

                    def j_body(j, accs):
                        c = pl.multiple_of(j * SC_LANES, SC_LANES)
                        x_lo, x_hi = x_v[s, pl.ds(c, SC_LANES)], x_v[s, pl.ds(W + c, SC_LANES)]
                        out = []
                        for k in range(RG):
                            lo, hi = _sc_halves(buf[r + k, pl.ds(c, SC_LANES)])
                            out.append(accs[k] + lo * x_lo + hi * x_hi)
                        return tuple(out)

                    accs = lax.fori_loop(0, W // SC_LANES, j_body,
                                         tuple(jnp.zeros((SC_LANES,), f32) for _ in range(RG)))
                    for k in range(RG):
                        vec = jnp.where(lane == i * RG + k, jnp.sum(accs[k]), vec)
                    return vec

                vec = lax.fori_loop(0, SC_LANES // RG, rows_body, jnp.zeros((SC_LANES,), f32))
                act_v[s, pl.ds(pl.multiple_of(bt * SC_ROWS + g * SC_LANES, SC_LANES), SC_LANES)] = vec

            @pl.when(bt == bpt - 1)
            def _():
                act_out(tl, s).start()

        _sc_batches(tab_hbm, idx_v, bufs, sems, tpw * bpt, compute)
        for tl in range(max(tpw - 2, 0), tpw):
            act_out(tl, tl % 2).wait()

    return act_kernel(table_u, idx.reshape(-1), hp).reshape(T, E)


def _sc_peer_sum(cfg, table_v, idx, w):
    T, E = idx.shape
    W = table_v.shape[1]
    mesh, workers = _sc_mesh()
    tpw = T // workers
    bpt = E // SC_ROWS
    assert tpw * workers == T and bpt * SC_ROWS == E and (tpw * bpt) % SC_BUFS == 0
    NQ = 4
    qw = W // NQ

    @functools.partial(
        pl.kernel, out_type=jax.ShapeDtypeStruct((T, 2 * W), f32), mesh=mesh,
        scratch_types=[pltpu.VMEM((tpw * E,), i32), pltpu.VMEM((2, E), f32), pltpu.VMEM((2, 2 * W), f32),
                       pltpu.SemaphoreType.DMA((SC_BUFS,)), pltpu.SemaphoreType.DMA((2,)),
                       pltpu.SemaphoreType.DMA((2,))] + [pltpu.VMEM((SC_ROWS, W), jnp.uint32)] * SC_BUFS,
        compiler_params=pltpu.CompilerParams(needs_layout_passes=False))
    def sum_kernel(tab_hbm, idx_hbm, w_hbm, y_hbm, idx_v, w_v, y_v, sems, wsem, osem, *bufs):
        t0 = _sc_worker(mesh) * tpw
        pltpu.sync_copy(idx_hbm.at[pl.ds(t0 * E, tpw * E)], idx_v)
        nq = qw // SC_LANES

        def w_in(tl, s):
            return pltpu.make_async_copy(w_hbm.at[pl.ds((t0 + tl) * E, E)], w_v.at[s], wsem.at[s])

        def y_out(tl, s):
            return pltpu.make_async_copy(y_v.at[s], y_hbm.at[t0 + tl], osem.at[s])

        w_in(0, 0).start()

        def compute(b, buf):
            tl, bt = b // bpt, b % bpt
            s = tl % 2

            @pl.when(bt == 0)
            def _():
                w_in(tl, s).wait()

                @pl.when(tl + 1 < tpw)
                def _():
                    w_in(tl + 1, 1 - s).start()

                @pl.when(tl >= 2)
                def _():
                    y_out(tl - 2, s).wait()

            half = jnp.full((SC_LANES,), s, i32)
            for q in range(NQ):
                cols = [q * qw + n * SC_LANES for n in range(nq)]
                zero = jnp.zeros((SC_LANES,), f32)
                init = tuple(jnp.where(bt == 0, zero, y_v[s, pl.ds(c, SC_LANES)]) for c in cols) + \
                    tuple(jnp.where(bt == 0, zero, y_v[s, pl.ds(W + c, SC_LANES)]) for c in cols)

                def r_body(r, accs):
                    wr = plsc.load_gather(w_v, [half, jnp.full((SC_LANES,), bt * SC_ROWS + r, i32)])
                    lo_acc, hi_acc = list(accs[:nq]), list(accs[nq:])
                    for n, c in enumerate(cols):
                        lo, hi = _sc_halves(buf[r, pl.ds(c, SC_LANES)])
                        lo_acc[n] = lo_acc[n] + lo * wr
                        hi_acc[n] = hi_acc[n] + hi * wr
                    return tuple(lo_acc) + tuple(hi_acc)

                accs = lax.fori_loop(0, SC_ROWS, r_body, init)
                for n, c in enumerate(cols):
                    y_v[s, pl.ds(c, SC_LANES)] = accs[n]
                    y_v[s, pl.ds(W + c, SC_LANES)] = accs[nq + n]

            @pl.when(bt == bpt - 1)
            def _():
                y_out(tl, s).start()

        _sc_batches(tab_hbm, idx_v, bufs, sems, tpw * bpt, compute)
        for tl in range(max(tpw - 2, 0), tpw):
            y_out(tl, tl % 2).wait()

    return sum_kernel(table_v, idx.reshape(-1), w.reshape(-1))


def _pack_bf16_halves(a):
    h = a.shape[1] // 2
    bits = lambda t: lax.bitcast_convert_type(t.astype(bf16), jnp.uint16).astype(jnp.uint32)
    return bits(a[:, :h]) | (bits(a[:, h:]) << 16)


def _split_bf16(a):
    hi = a.astype(bf16)
    lo = (a - hi.astype(f32)).astype(bf16)
    return jnp.concatenate([hi, lo], axis=0)


def _fold_rows(a):
    n = a.shape[0] // 2
    return a[:n] + a[n:]


def _apply_group(h2, gates, words, sel, selt, gn, gf):
    R, D = h2.shape
    nw = D // 2
    nz = 2 * words.shape[0]
    lane = lax.broadcasted_iota(i32, (2 * R, nz), 1)
    rowi = lax.broadcasted_iota(i32, (2 * R, nz), 0)
    mine = ((lane // 2) % R == rowi % R) & (lane % 2 == rowi // R)
    zu = pltpu.bitcast(words[:, :nw], bf16)
    zv = pltpu.bitcast(words[:, nw:], bf16)
    x = _rms(h2, gn)
    xs = _split_bf16(jnp.concatenate([x[:, :nw], x[:, nw:]], axis=0))
    a = _fold_rows(lax.dot_general(xs, zu, (((1,), (1,)), ((), ())), preferred_element_type=f32))
    a = jnp.where(mine, a, 0.0)
    a = a[:R] + a[R:]
    act = _fold_rows(jnp.dot(_split_bf16(a), sel, preferred_element_type=f32))
    w = gates * _gelu(act)
    w_rows = _fold_rows(jnp.dot(_split_bf16(w), selt, preferred_element_type=f32))
    ws = _split_bf16(jnp.where(mine, jnp.concatenate([w_rows, w_rows], axis=0), 0.0))
    y2 = _fold_rows(jnp.dot(ws, zv, preferred_element_type=f32))
    y = jnp.concatenate([y2[:R], y2[R:]], axis=1)
    return _rms(h2 + y, gf)


def _peer_direct_kernel(idx_ref, idxn_ref, h_ref, gt_ref, tab_ref, sel_ref, selt_ref, gn_ref, gf_ref, o_ref,
                        buf_a, buf_b, sem_ref, *, cfg):
    E, R = cfg.slots, SUBLANES
    n_rows = E * R
    i = pl.program_id(0)

    def start_row(src_idx_ref, base, r, buf, s, priority=0):
        pltpu.async_copy(tab_ref.at[pl.ds(src_idx_ref[base + r], 1)], buf.at[pl.ds(r, 1)], sem_ref.at[s],
                         priority=priority)

    def wait_rows(buf, s):
        pltpu.make_async_copy(tab_ref.at[pl.ds(0, n_rows)], buf, sem_ref.at[s]).wait()

    def group(gi, buf):
        tok = slice(gi * R, (gi + 1) * R)
        o_ref[tok, :] = _apply_group(h_ref[tok, :], gt_ref[tok, :], buf[...], sel_ref[...], selt_ref[...],
                                     gn_ref[...], gf_ref[...])

    @pl.when(i == 0)
    def _():
        def body(r, carry):
            start_row(idx_ref, 0, r, buf_a, 0)
            return carry
        lax.fori_loop(0, n_rows, body, 0, unroll=8)

    wait_rows(buf_a, 0)
    for r in range(n_rows):
        start_row(idx_ref, n_rows, r, buf_b, 1, priority=r % 2)
    group(0, buf_a)
    wait_rows(buf_b, 1)
    for r in range(n_rows):
        start_row(idxn_ref, 0, r, buf_a, 0, priority=r % 2)
    group(1, buf_b)

    @pl.when(i == pl.num_programs(0) - 1)
    def _():
        wait_rows(buf_a, 0)


def _peer_direct(cfg, h2, gates, idx, table, gn, gf):
    D = h2.shape[1]
    E, R = cfg.slots, SUBLANES
    T = idx.shape[0] // E
    n_rows = E * R
    nz = 2 * n_rows
    n = T // (2 * R)
    sel = (jnp.arange(nz)[:, None] // (2 * R) == jnp.arange(E)[None, :]).astype(bf16)
    row = lambda i: (i, 0)
    smem = functools.partial(pl.BlockSpec, (2 * n_rows,), memory_space=pltpu.SMEM)
    return pl.pallas_call(
        functools.partial(_peer_direct_kernel, cfg=cfg),
        grid=(n,),
        in_specs=[smem(lambda i: (i,)), smem(lambda i: (jnp.minimum(i + 1, n - 1),)),
                  pl.BlockSpec((2 * R, D), row), pl.BlockSpec((2 * R, E), row), pl.BlockSpec(memory_space=pl.ANY),
                  _resident((nz, E)), _resident((E, nz)), _resident((1, D)), _resident((1, D))],
        out_specs=pl.BlockSpec((2 * R, D), row),
        out_shape=jax.ShapeDtypeStruct((T, D), f32),
        scratch_shapes=[pltpu.VMEM((n_rows, D), table.dtype), pltpu.VMEM((n_rows, D), table.dtype),
                        pltpu.SemaphoreType.DMA((2,))],
        compiler_params=_cparams("arbitrary"),
        name="peer_direct",
    )(idx, idx, h2, gates, table, sel, sel.T, gn, gf)


def _rowwise(name, body, out_cols, T, row0, *arrays, rows_per_step=128):
    tm = min(rows_per_step, T)
    assert T % tm == 0 and row0 % tm == 0

    def spec(a):
        if a.shape[0] == 1:
            return _resident(a.shape)
        off = row0 // tm if a.shape[0] > T else 0
        return pl.BlockSpec((tm, a.shape[1]), lambda i: (i + off, 0))

    def kern(*refs):
        refs[-1][...] = body(*(r[...] for r in refs[:-1]))

    return pl.pallas_call(
        kern, grid=(T // tm,),
        in_specs=[spec(a) for a in arrays],
        out_specs=pl.BlockSpec((tm, out_cols), lambda i: (i, 0)),
        out_shape=jax.ShapeDtypeStruct((T, out_cols), f32),
        compiler_params=_cparams("parallel"), name=name,
    )(*arrays)


def _peer_pre(cfg, h2, row0, gn):
    return _rowwise("peer_pre", _rms, h2.shape[1], h2.shape[0] - row0, row0, h2, gn)


def _peer_mid(cfg, gates, row0, act):
    return _rowwise("peer_mid", lambda g, a: g * _gelu(a), gates.shape[1], act.shape[0], row0, gates, act)


def _peer_post(cfg, h2, row0, y, gf):
    return _rowwise("peer_post", lambda h, yy, g: _rms(h + yy, g), h2.shape[1], y.shape[0], row0, h2, y, gf)


def _block(cfg, sc_fns, x, mem, positions, mix_norm_g, w_in, lam_q1, lam_k1, lam_q2, lam_k2, attn_head_g,
           ssm_a_re, ssm_a_im, ssm_log_dt, ssm_b_re, ssm_b_im, ssm_c_re, ssm_c_im, ssm_d, glu_w, glu_b,
           ssm_out_g, w_out, xattn_norm_g, mem_norm_g, xattn_wq, xattn_wkv, xattn_wo, ffn_norm_g,
           peer_wq, peer_k1, peer_k2, peer_u, peer_v, final_norm_g):
    B, S, D = x.shape
    T = B * S
    l = 0
    row = lambda a: a.reshape(1, -1)
    x2 = x.reshape(T, D)
    pos = positions.reshape(T, 1).astype(f32)
    freqs = cfg.rope_theta ** (-jnp.arange(0, cfg.rot_dim, 2, dtype=f32) / cfg.rot_dim)
    lane = jnp.arange(LANES) % cfg.diff_qkdim
    freq_row = jnp.where(lane < cfg.rot_dim, freqs[lane % (cfg.rot_dim // 2)], 0.0).reshape(1, LANES)

    lam_p = jnp.stack([lam_q1[l], lam_k1[l], lam_q2[l], lam_k2[l]])
    s5_params = _s5_params(cfg, ssm_a_re[l], ssm_a_im[l], ssm_log_dt[l], ssm_b_re[l], ssm_b_im[l],
                           ssm_c_re[l], ssm_c_im[l], ssm_d[l])
    ns, _, wn = s5_params[0].shape
    zero_state = jnp.zeros((ns, B, wn), f32)
    state = (zero_state, zero_state)
    w_in_b, glu_w_b, w_out_b = w_in[l].astype(bf16), glu_w[l].astype(bf16), w_out[l].astype(bf16)
    wq_b, wo_b, pw_b = xattn_wq[l].astype(bf16), xattn_wo[l].astype(bf16), peer_wq[l].astype(bf16)
    k1_b, k2_b = peer_k1[l].astype(bf16), peer_k2[l].astype(bf16)
    kv = _mem_kv(cfg, mem.reshape(B * cfg.n_mem, D), row(mem_norm_g[l]), xattn_wkv[l].astype(bf16))
    table_u, table_v = _pack_bf16_halves(peer_u[l]), _pack_bf16_halves(peer_v[l])
    table = jnp.concatenate([table_u, table_v], axis=1)
    E, R = cfg.slots, SUBLANES
    Lc = cfg.chunk
    Tc = B * Lc
    td = Tc * cfg.direct_sixteenths // 16
    sc_act, sc_sum = sc_fns

    def after(a, tokens):
        return lax.optimization_barrier((a,) + tuple(tokens))[0] if tokens else a

    def dense(c, qks, vs, state, tokens):
        qk, v, u = _in_proj(cfg, c, x2, pos, after(row(mix_norm_g[l]), tokens), freq_row, w_in_b)
        qks, vs = qks + [qk], vs + [v]
        att = _diff_attn(cfg, lam_p, qks, vs, row(attn_head_g[l]))
        y, state = _s5_scan(cfg, u.reshape(Tc, cfg.ssm_width), s5_params, state)
        ssm = _s5_glu(cfg, y, glu_w_b, row(glu_b[l]), row(ssm_out_g[l]))
        h1, hn = _out_proj(cfg, c, x2, att, ssm.reshape(Lc, B * cfg.ssm_width), w_out_b, row(xattn_norm_g[l]))
        h2, qp = _xattn(cfg, h1, hn, kv, wq_b, wo_b, row(ffn_norm_g[l]), pw_b)
        experts_t, gates_t = _peer_route(cfg, qp, k1_b, k2_b)
        return qks, vs, state, h2, experts_t, gates_t.T

    gn, gf = row(ffn_norm_g[l]), row(final_norm_g)
    qks, vs, outs, tokens, pending = [], [], [], [], None
    for c in range(cfg.n_chunks):
        qks, vs, state, h2, experts_t, gates = dense(c, qks, vs, state, tokens)
        parts, tokens, before_direct = [], [], []
        if td < Tc:
            idx_s = experts_t.T[td:]
            hp = _peer_pre(cfg, h2, td, gn)
            before_direct.append(hp)
        if pending is not None:
            p_h2, p_y, p_parts = pending
            p_parts.append(_peer_post(cfg, p_h2, td, p_y, after(gf, before_direct)))
            before_direct.append(p_parts[-1])
            outs.append(jnp.concatenate(p_parts, axis=0).reshape(B, Lc, D))
            pending = None
        if td:
            idx_d = experts_t[:, :td].reshape(E, td // R, R).transpose(1, 0, 2).reshape(-1)
            parts.append(_peer_direct(cfg, h2, gates, idx_d, table, after(gn, before_direct), gf))
        if td < Tc:
            act = sc_act(table_u, idx_s, hp)
            w = _peer_mid(cfg, after(gates, parts), td, act)
            tokens = [w]
            pending = (h2, sc_sum(table_v, idx_s, w), parts)
        else:
            outs.append(jnp.concatenate(parts, axis=0).reshape(B, Lc, D))
    if pending is not None:
        p_h2, p_y, p_parts = pending
        p_parts.append(_peer_post(cfg, p_h2, td, p_y, gf))
        outs.append(jnp.concatenate(p_parts, axis=0).reshape(B, Lc, D))
    return jnp.concatenate(outs, axis=1)


def kernel(x, mem, positions, mix_norm_g, w_in, lam_q1, lam_k1, lam_q2, lam_k2, attn_head_g, ssm_a_re, ssm_a_im, ssm_log_dt, ssm_b_re, ssm_b_im, ssm_c_re, ssm_c_im, ssm_d, glu_w, glu_b, ssm_out_g, w_out, xattn_norm_g, mem_norm_g, xattn_wq, xattn_wkv, xattn_wo, ffn_norm_g, peer_wq, peer_k1, peer_k2, peer_u, peer_v, final_norm_g):
    cfg = Cfg()
    return _block(cfg, (functools.partial(_sc_peer_act, cfg), functools.partial(_sc_peer_sum, cfg)), x, mem, positions, mix_norm_g, w_in, lam_q1, lam_k1,
                  lam_q2, lam_k2, attn_head_g, ssm_a_re, ssm_a_im, ssm_log_dt, ssm_b_re, ssm_b_im, ssm_c_re,
                  ssm_c_im, ssm_d, glu_w, glu_b, ssm_out_g, w_out, xattn_norm_g, mem_norm_g, xattn_wq,
                  xattn_wkv, xattn_wo, ffn_norm_g, peer_wq, peer_k1, peer_k2, peer_u, peer_v, final_norm_g)
```

```python
import dataclasses
import functools
import math

import jax
import jax.numpy as jnp
from jax import lax
from jax.experimental import pallas as pl
from jax.experimental.pallas import tpu as pltpu
from jax.experimental.pallas import tpu_sc as plsc

f32 = jnp.float32
bf16 = jnp.bfloat16
i32 = jnp.int32

LANES = 128
SUBLANES = 8
VMEM_LIMIT = 56 * 1024 * 1024
NEG = -1e30
EPS = 1e-6


@dataclasses.dataclass(frozen=True)
class Cfg:
    d_model: int = 2048
    batch: int = 8
    seq: int = 2048
    n_mem: int = 256
    diff_heads: int = 8
    ssm_group: int = 16
    ssm_state: int = 64
    xattn_heads: int = 4
    xattn_head_dim: int = 128
    peer_heads: int = 8
    peer_keys: int = 128
    peer_qdim: int = 256
    peer_topk: int = 16
    rope_theta: float = 500000.0
    lam_init: float = 0.8 - 0.6 * math.exp(-0.3 * 0)
    chunk: int = 256
    scan_chunk: int = 128
    route_tm: int = 256
    direct_sixteenths: int = 4

    @property
    def n_chunks(self):
        return self.seq // self.chunk

    @property
    def attn_width(self):
        return self.d_model // 2

    @property
    def ssm_width(self):
        return self.d_model - self.attn_width

    @property
    def diff_vdim(self):
        return self.attn_width // self.diff_heads

    @property
    def diff_qkdim(self):
        return self.diff_vdim // 2

    @property
    def rot_dim(self):
        return self.diff_qkdim // 4

    @property
    def ssm_groups(self):
        return self.ssm_width // self.ssm_group

    @property
    def xattn_width(self):
        return self.xattn_heads * self.xattn_head_dim

    @property
    def tokens(self):
        return self.batch * self.seq

    @property
    def slots(self):
        return self.peer_heads * self.peer_topk


def _cparams(*sem):
    return pltpu.CompilerParams(dimension_semantics=sem, vmem_limit_bytes=VMEM_LIMIT)


def _resident(shape):
    nd = len(shape)
    return pl.BlockSpec(shape, lambda *_: (0,) * nd, pipeline_mode=pl.Buffered(1))


def _rms(x, g):
    return x * lax.rsqrt(jnp.mean(x * x, axis=-1, keepdims=True) + EPS) * g


def _gelu(x):
    return 0.5 * x * (1.0 + lax.erf(x * (2.0 ** -0.5)))


def _in_proj_kernel(x_ref, pos_ref, g_ref, freq_ref, w_ref, qk_ref, v_ref, u_ref, *, cfg):
    n_qk, n_v = 2 * cfg.attn_width, cfg.attn_width
    half = cfg.rot_dim // 2
    xn = _rms(x_ref[...], g_ref[...]).astype(bf16)
    ang = pos_ref[...] * freq_ref[...]
    cos, sin = jnp.cos(ang), jnp.sin(ang)
    lane = lax.broadcasted_iota(i32, (1, LANES), 1) % cfg.diff_qkdim
    sin_lo = jnp.where(lane < half, -sin, 0.0)
    sin_hi = jnp.where((lane >= half) & (lane < 2 * half), sin, 0.0)
    cw = 2 * LANES
    for c in range((n_qk + n_v + cfg.ssm_width) // cw):
        col = c * cw
        z = jnp.dot(xn, w_ref[:, col:col + cw], preferred_element_type=f32)
        if col < n_qk:
            for k in range(cw // LANES):
                zk = z[:, k * LANES:(k + 1) * LANES]
                zk = zk * cos + pltpu.roll(zk, LANES - half, 1) * sin_lo + pltpu.roll(zk, half, 1) * sin_hi
                qk_ref[:, col + k * LANES:col + (k + 1) * LANES] = zk.astype(bf16)
        elif col < n_qk + n_v:
            v_ref[:, col - n_qk:col - n_qk + cw] = z.astype(bf16)
        else:
            u_ref[:, col - n_qk - n_v:col - n_qk - n_v + cw] = z


def _chunk_rows(cfg, c):
    return lambda b: (b * cfg.n_chunks + c, 0)


def _in_proj(cfg, c, x2, pos, g, freq, w):
    D = x2.shape[1]
    B, tm = cfg.batch, cfg.chunk
    n_qk, n_v, n_u = 2 * cfg.attn_width, cfg.attn_width, cfg.ssm_width
    row = lambda b: (b, 0)
    return pl.pallas_call(
        functools.partial(_in_proj_kernel, cfg=cfg),
        grid=(B,),
        in_specs=[pl.BlockSpec((tm, D), _chunk_rows(cfg, c)), pl.BlockSpec((tm, 1), _chunk_rows(cfg, c)),
                  _resident((1, D)), _resident((1, LANES)), _resident(w.shape)],
        out_specs=[pl.BlockSpec((tm, n_qk), row), pl.BlockSpec((tm, n_v), row),
                   pl.BlockSpec((tm, n_u), lambda b: (0, b))],
        out_shape=[jax.ShapeDtypeStruct((B * tm, n_qk), bf16), jax.ShapeDtypeStruct((B * tm, n_v), bf16),
                   jax.ShapeDtypeStruct((tm, B * n_u), f32)],
        compiler_params=_cparams("parallel"),
        name="in_proj",
    )(x2, pos, g, freq, w)


def _diff_attn_kernel(lam_ref, q_ref, *rest, cfg, n_kv):
    k_refs, v_refs = rest[:n_kv], rest[n_kv:2 * n_kv]
    g_ref, o_ref = rest[2 * n_kv:]
    tq = cfg.chunk
    d = cfg.diff_qkdim
    lane = lax.broadcasted_iota(i32, (1, LANES), 1)
    q = q_ref[...].astype(f32) * (d ** -0.5)
    qs = (jnp.where(lane < d, q, 0.0).astype(bf16), jnp.where(lane >= d, q, 0.0).astype(bf16))
    causal = (lax.broadcasted_iota(i32, (tq, tq), 1) <= lax.broadcasted_iota(i32, (tq, tq), 0))
    tiles = lambda a: [a[:, t * LANES:(t + 1) * LANES] for t in range(tq // LANES)]

    def scores(c, j):
        s = lax.dot_general(qs[c], k_refs[j][...], (((1,), (1,)), ((), ())), preferred_element_type=f32)
        return jnp.where(causal, s, NEG) if j == n_kv - 1 else s

    outs = []
    for c in range(2):
        mm = jnp.full((tq, LANES), NEG, f32)
        for j in range(n_kv):
            for st in tiles(scores(c, j)):
                mm = jnp.maximum(mm, st)
        mb = jnp.broadcast_to(jnp.max(mm, axis=1, keepdims=True), (tq, LANES))
        ls = jnp.zeros((tq, LANES), f32)
        acc = jnp.zeros((tq, LANES), f32)
        for j in range(n_kv):
            ps = [jnp.exp(st - mb) for st in tiles(scores(c, j))]
            for p in ps:
                ls = ls + p
            acc = acc + jnp.dot(jnp.concatenate(ps, axis=1).astype(bf16), v_refs[j][...],
                                preferred_element_type=f32)
        outs.append(acc / jnp.sum(ls, axis=1, keepdims=True))

    lp = lam_ref[...]
    lam = (jnp.exp(jnp.sum(lp[0:1] * lp[1:2], axis=1, keepdims=True))
           - jnp.exp(jnp.sum(lp[2:3] * lp[3:4], axis=1, keepdims=True)) + cfg.lam_init)
    o = outs[0] - lam * outs[1]
    o_ref[...] = (_rms(o, g_ref[...]) * (1.0 - cfg.lam_init)).astype(o_ref.dtype)


def _diff_attn(cfg, lam_p, qks, vs, g):
    n_kv = len(qks)
    B, H, tq = cfg.batch, cfg.diff_heads, cfg.chunk
    head = lambda b, h: (b, h)
    key = lambda b, h: (b, H + h)
    return pl.pallas_call(
        functools.partial(_diff_attn_kernel, cfg=cfg, n_kv=n_kv),
        grid=(B, H),
        in_specs=([_resident(lam_p.shape), pl.BlockSpec((tq, LANES), head)]
                  + [pl.BlockSpec((tq, LANES), key)] * n_kv + [pl.BlockSpec((tq, LANES), head)] * n_kv
                  + [_resident((1, LANES))]),
        out_specs=pl.BlockSpec((tq, LANES), head),
        out_shape=jax.ShapeDtypeStruct((B * tq, cfg.attn_width), bf16),
        compiler_params=_cparams("parallel", "parallel"),
        name="diff_attn",
    )(lam_p, qks[-1], *qks, *vs, g)


def _s5_scan_kernel(u_ref, are_ref, aim_ref, ldt_ref, bre_ref, bim_ref, cre_ref, cim_ref, d_ref,
                    sin_re_ref, sin_im_ref, y_ref, sout_re_ref, sout_im_ref,
                    ab_ref, bbar_re_ref, bbar_im_ref, bu_re_ref, bu_im_ref, *, cfg):
    B, L = cfg.batch, cfg.scan_chunk
    t = pl.program_id(1)
    st_re_ref, st_im_ref = sout_re_ref.at[0], sout_im_ref.at[0]

    @pl.when(t == 0)
    def _():
        dt = jnp.exp(ldt_ref[0])
        lre, lim = are_ref[0], aim_ref[0]
        mag = jnp.exp(lre * dt)
        ang = lim * dt
        ab_re, ab_im = mag * jnp.cos(ang), mag * jnp.sin(ang)
        den = lre * lre + lim * lim
        f_re = ((ab_re - 1.0) * lre + ab_im * lim) / den
        f_im = (ab_im * lre - (ab_re - 1.0) * lim) / den
        ab_ref[0] = jnp.broadcast_to(ab_re, ab_ref.shape[1:])
        ab_ref[1] = jnp.broadcast_to(ab_im, ab_ref.shape[1:])
        br, bi = bre_ref[0], bim_ref[0]
        bbar_re_ref[...] = (f_re * br - f_im * bi).astype(bf16)
        bbar_im_ref[...] = (f_re * bi + f_im * br).astype(bf16)
        st_re_ref[...] = sin_re_ref[0]
        st_im_ref[...] = sin_im_ref[0]

    u = u_ref[...]
    ub = u.astype(bf16)
    bu_re_ref[...] = jnp.dot(ub, bbar_re_ref[...], preferred_element_type=f32)
    bu_im_ref[...] = jnp.dot(ub, bbar_im_ref[...], preferred_element_type=f32)
    a_re, a_im = ab_ref[0], ab_ref[1]

    def body(s, carry):
        xr, xi = carry
        rows = pl.ds(pl.multiple_of(s * B, B), B)
        nr = a_re * xr - a_im * xi + bu_re_ref[rows, :]
        ni = a_re * xi + a_im * xr + bu_im_ref[rows, :]
        bu_re_ref[rows, :] = nr
        bu_im_ref[rows, :] = ni
        return nr, ni

    xr, xi = lax.fori_loop(0, L, body, (st_re_ref[...], st_im_ref[...]), unroll=8)
    st_re_ref[...] = xr
    st_im_ref[...] = xi

    y_ref[...] = (jnp.dot(bu_re_ref[...].astype(bf16), cre_ref[0].astype(bf16), preferred_element_type=f32)
                  - jnp.dot(bu_im_ref[...].astype(bf16), cim_ref[0].astype(bf16), preferred_element_type=f32)
                  + d_ref[0] * u)


def _block_diag(w, blocks):
    G, r, c = w.shape
    w4 = w.reshape(G // blocks, blocks, r, c)
    eye = jnp.eye(blocks, dtype=w.dtype)
    out = w4[:, :, :, None, :] * eye[None, :, None, :, None]
    return out.reshape(G // blocks, blocks * r, blocks * c)


def _s5_params(cfg, a_re, a_im, log_dt, b_re, b_im, c_re, c_im, d_skip):
    G, P, Hc = cfg.ssm_groups, cfg.ssm_state, cfg.ssm_group
    gps = min(G, 2 * LANES // Hc)
    ns = G // gps
    wu, wn = gps * Hc, gps * P
    return (a_re.reshape(ns, 1, wn), a_im.reshape(ns, 1, wn), jnp.repeat(log_dt, P).reshape(ns, 1, wn),
            _block_diag(b_re.transpose(0, 2, 1), gps), _block_diag(b_im.transpose(0, 2, 1), gps),
            _block_diag(c_re.transpose(0, 2, 1), gps), _block_diag(c_im.transpose(0, 2, 1), gps),
            d_skip.reshape(ns, 1, wu))


def _s5_scan(cfg, u2, params, state):
    B, L = cfg.batch, cfg.scan_chunk
    rows, W = u2.shape
    ns, _, wn = params[0].shape
    wu = params[-1].shape[2]
    slab = lambda s, t: (s, 0, 0)
    st = pl.BlockSpec((1, B, wn), slab)
    st_shape = jax.ShapeDtypeStruct((ns, B, wn), f32)
    y, s_re, s_im = pl.pallas_call(
        functools.partial(_s5_scan_kernel, cfg=cfg),
        grid=(ns, rows // (L * B)),
        in_specs=[pl.BlockSpec((L * B, wu), lambda s, t: (t, s)),
                  pl.BlockSpec((1, 1, wn), slab), pl.BlockSpec((1, 1, wn), slab), pl.BlockSpec((1, 1, wn), slab),
                  pl.BlockSpec((1, wu, wn), slab), pl.BlockSpec((1, wu, wn), slab),
                  pl.BlockSpec((1, wn, wu), slab), pl.BlockSpec((1, wn, wu), slab),
                  pl.BlockSpec((1, 1, wu), slab), st, st],
        out_specs=[pl.BlockSpec((L * B, wu), lambda s, t: (t, s)), st, st],
        out_shape=[jax.ShapeDtypeStruct((rows, W), f32), st_shape, st_shape],
        scratch_shapes=[pltpu.VMEM((2, B, wn), f32),
                        pltpu.VMEM((wu, wn), bf16), pltpu.VMEM((wu, wn), bf16),
                        pltpu.VMEM((L * B, wn), f32), pltpu.VMEM((L * B, wn), f32)],
        compiler_params=_cparams("arbitrary", "arbitrary"),
        name="s5_scan",
    )(u2, *params, *state)
    return y, (s_re, s_im)


def _s5_glu_kernel(y_ref, w_ref, b_ref, g_ref, o_ref):
    g = _gelu(y_ref[...])
    z = jnp.dot(g.astype(bf16), w_ref[...], preferred_element_type=f32) + b_ref[...]
    g = g * jax.nn.sigmoid(z)
    o_ref[...] = _rms(g, g_ref[...]).astype(o_ref.dtype)


def _s5_glu(cfg, y2, w, b, g):
    T, W = y2.shape
    tm = cfg.chunk
    row = lambda i: (i, 0)
    return pl.pallas_call(
        _s5_glu_kernel,
        grid=(T // tm,),
        in_specs=[pl.BlockSpec((tm, W), row), _resident(w.shape), _resident((1, W)), _resident((1, W))],
        out_specs=pl.BlockSpec((tm, W), row),
        out_shape=jax.ShapeDtypeStruct((T, W), bf16),
        compiler_params=_cparams("parallel"),
        name="s5_glu",
    )(y2, w, b, g)


def _out_proj_kernel(x_ref, att_ref, ssm_ref, w_ref, g_ref, h_ref, hn_ref, *, cfg):
    aw = cfg.attn_width
    h = (x_ref[...]
         + jnp.dot(att_ref[...], w_ref[:aw, :], preferred_element_type=f32)
         + jnp.dot(ssm_ref[...], w_ref[aw:, :], preferred_element_type=f32))
    h_ref[...] = h
    hn_ref[...] = _rms(h, g_ref[...]).astype(hn_ref.dtype)


def _out_proj(cfg, c, x2, att, ssm, w, g):
    D = x2.shape[1]
    tm = cfg.chunk
    T = cfg.batch * tm
    row = lambda i: (i, 0)
    return pl.pallas_call(
        functools.partial(_out_proj_kernel, cfg=cfg),
        grid=(T // tm,),
        in_specs=[pl.BlockSpec((tm, D), _chunk_rows(cfg, c)), pl.BlockSpec((tm, cfg.attn_width), row),
                  pl.BlockSpec((tm, cfg.ssm_width), lambda b: (0, b)), _resident(w.shape), _resident((1, D))],
        out_specs=[pl.BlockSpec((tm, D), row), pl.BlockSpec((tm, D), row)],
        out_shape=[jax.ShapeDtypeStruct((T, D), f32), jax.ShapeDtypeStruct((T, D), bf16)],
        compiler_params=_cparams("parallel"),
        name="out_proj",
    )(x2, att, ssm, w, g)


def _mem_kv_kernel(m_ref, g_ref, w_ref, o_ref):
    mn = _rms(m_ref[...], g_ref[...]).astype(bf16)
    o_ref[...] = jnp.dot(mn, w_ref[...], preferred_element_type=f32).astype(o_ref.dtype)


def _mem_kv(cfg, mem2, g, w):
    R, D = mem2.shape
    tm = cfg.n_mem
    row = lambda i: (i, 0)
    return pl.pallas_call(
        _mem_kv_kernel,
        grid=(R // tm,),
        in_specs=[pl.BlockSpec((tm, D), row), _resident((1, D)), _resident(w.shape)],
        out_specs=pl.BlockSpec((tm, w.shape[1]), row),
        out_shape=jax.ShapeDtypeStruct((R, w.shape[1]), bf16),
        compiler_params=_cparams("parallel"),
        name="mem_kv",
    )(mem2, g, w)


def _xattn_kernel(h_ref, hn_ref, kv_ref, wq_ref, wo_ref, g_ref, pw_ref, h2_ref, qp_ref, *, cfg):
    nh, hd, xw = cfg.xattn_heads, cfg.xattn_head_dim, cfg.xattn_width
    q = jnp.dot(hn_ref[...], wq_ref[...], preferred_element_type=f32).astype(bf16)
    outs = []
    for h in range(nh):
        qh = q[:, h * hd:(h + 1) * hd]
        kh = kv_ref[:, h * hd:(h + 1) * hd]
        vh = kv_ref[:, xw + h * hd:xw + (h + 1) * hd]
        s = lax.dot_general(qh, kh, (((1,), (1,)), ((), ())), preferred_element_type=f32) * (hd ** -0.5)
        p = jnp.exp(s - jnp.max(s, axis=1, keepdims=True))
        p = p / jnp.sum(p, axis=1, keepdims=True)
        outs.append(jnp.dot(p.astype(bf16), vh, preferred_element_type=f32).astype(bf16))
    o = jnp.concatenate(outs, axis=1)
    h2 = h_ref[...] + jnp.dot(o, wo_ref[...], preferred_element_type=f32)
    h2_ref[...] = h2
    hp = _rms(h2, g_ref[...]).astype(bf16)
    qp_ref[...] = jnp.dot(hp, pw_ref[...], preferred_element_type=f32).astype(qp_ref.dtype)


def _xattn(cfg, h1, hn, kv, wq, wo, g, pw):
    T, D = h1.shape
    tm, M = cfg.chunk, cfg.n_mem
    row = lambda i: (i, 0)
    return pl.pallas_call(
        functools.partial(_xattn_kernel, cfg=cfg),
        grid=(T // tm,),
        in_specs=[pl.BlockSpec((tm, D), row), pl.BlockSpec((tm, D), row),
                  pl.BlockSpec((M, kv.shape[1]), row),
                  _resident(wq.shape), _resident(wo.shape), _resident((1, D)), _resident(pw.shape)],
        out_specs=[pl.BlockSpec((tm, D), row), pl.BlockSpec((tm, pw.shape[1]), row)],
        out_shape=[jax.ShapeDtypeStruct((T, D), f32), jax.ShapeDtypeStruct((T, pw.shape[1]), bf16)],
        compiler_params=_cparams("parallel"),
        name="xattn",
    )(h1, hn, kv, wq, wo, g, pw)


def _top_rows(s, k, payload=None):
    n = s.shape[0]
    rows = lax.broadcasted_iota(i32, s.shape, 0)
    vals, picks = [], []
    for _ in range(k):
        m = jnp.max(s, axis=0, keepdims=True)
        idx = jnp.min(jnp.where(s == m, rows, n), axis=0, keepdims=True)
        sel = rows == idx
        vals.append(m)
        picks.append(idx if payload is None else jnp.max(jnp.where(sel, payload, -1), axis=0, keepdims=True))
        s = jnp.where(sel, NEG, s)
    return vals, picks


def _peer_route_kernel(qp_ref, k1_ref, k2_ref, e_ref, g_ref, *, cfg):
    for t in range(qp_ref.shape[0] // LANES):
        tok = slice(t * LANES, (t + 1) * LANES)
        _route_tile(qp_ref[tok, :], k1_ref[0], k2_ref[0], e_ref.at[:, tok], g_ref.at[:, tok], cfg)


def _route_tile(qp, k1, k2, e_ref, g_ref, cfg):
    K, nk, half = cfg.peer_topk, cfg.peer_keys, cfg.peer_qdim // 2
    dn = (((1,), (1,)), ((), ()))
    s1 = lax.dot_general(k1, qp[:, :half], dn, preferred_element_type=f32)
    s2 = lax.dot_general(k2, qp[:, half:], dn, preferred_element_type=f32)
    t1, i1 = _top_rows(s1, K)
    t2, i2 = _top_rows(s2, K)
    t1, i1 = jnp.concatenate(t1, axis=0), jnp.concatenate(i1, axis=0)
    t2, i2 = jnp.concatenate(t2, axis=0), jnp.concatenate(i2, axis=0)
    cand = jnp.concatenate([t1[0:1] + t2] + [t1[a:a + 1] + t2[:K // 2] for a in range(1, K)], axis=0)
    expert = jnp.concatenate([i1[0:1] * nk + i2] + [i1[a:a + 1] * nk + i2[:K // 2] for a in range(1, K)], axis=0)
    ts, es = _top_rows(cand, K, payload=expert)
    ts = jnp.concatenate(ts, axis=0)
    p = jnp.exp(ts - ts[0:1])
    g_ref[...] = p / jnp.sum(p, axis=0, keepdims=True)
    e_ref[...] = jnp.concatenate(es, axis=0)


def _peer_route(cfg, qp, k1, k2):
    T = qp.shape[0]
    tm, H, K, Q = cfg.route_tm, cfg.peer_heads, cfg.peer_topk, cfg.peer_qdim
    out = pl.BlockSpec((K, tm), lambda i, h: (h, i))
    return pl.pallas_call(
        functools.partial(_peer_route_kernel, cfg=cfg),
        grid=(T // tm, H),
        in_specs=[pl.BlockSpec((tm, Q), lambda i, h: (i, h)),
                  pl.BlockSpec((1,) + k1.shape[1:], lambda i, h: (h, 0, 0)),
                  pl.BlockSpec((1,) + k2.shape[1:], lambda i, h: (h, 0, 0))],
        out_specs=[out, out],
        out_shape=[jax.ShapeDtypeStruct((H * K, T), i32), jax.ShapeDtypeStruct((H * K, T), f32)],
        compiler_params=_cparams("parallel", "parallel"),
        name="peer_route",
    )(qp, k1, k2)


SC_LANES = 16
SC_ROWS = 32
SC_BUFS = 3


def _sc_mesh():
    mesh = plsc.VectorSubcoreMesh(core_axis_name="core", subcore_axis_name="subcore")
    return mesh, mesh.num_cores * mesh.num_subcores


def _sc_worker(mesh):
    return lax.axis_index("core") * mesh.num_subcores + lax.axis_index("subcore")


def _sc_halves(words):
    return plsc.bitcast(words << 16, f32), plsc.bitcast(words & jnp.uint32(0xFFFF0000), f32)


def _sc_batches(tab_hbm, idx_v, bufs, sems, n_batches, compute):
    nb = len(bufs)

    def gather(b, s):
        return pltpu.make_async_copy(tab_hbm.at[idx_v.at[pl.ds(b * SC_ROWS, SC_ROWS)]], bufs[s], sems.at[s])

    def step(b, s, when):
        gather(b, s).wait()
        when(b + nb - 1 < n_batches, lambda: gather(b + nb - 1, (s - 1) % nb).start())
        compute(b, bufs[s])

    for s in range(min(nb - 1, n_batches)):
        gather(s, s).start()
    main = n_batches // nb * nb

    @pl.loop(0, main, step=nb)
    def _(b0):
        for s in range(nb):
            step(b0 + s, s, lambda cond, fn: pl.when(cond)(fn))

    for b in range(main, n_batches):
        step(b, b % nb, lambda cond, fn: fn() if cond else None)


def _sc_peer_act(cfg, table_u, idx, hp):
    T, E = idx.shape
    W = table_u.shape[1]
    mesh, workers = _sc_mesh()
    tpw = T // workers
    bpt = E // SC_ROWS
    assert tpw * workers == T and bpt * SC_ROWS == E and (tpw * bpt) % SC_BUFS == 0
    RG = 4

    @functools.partial(
        pl.kernel, out_type=jax.ShapeDtypeStruct((T * E,), f32), mesh=mesh,
        scratch_types=[pltpu.VMEM((tpw * E,), i32), pltpu.VMEM((2, 2 * W), f32),
                       pltpu.VMEM((2, E), f32), pltpu.SemaphoreType.DMA((SC_BUFS,)),
                       pltpu.SemaphoreType.DMA((2,)), pltpu.SemaphoreType.DMA((2,))]
        + [pltpu.VMEM((SC_ROWS, W), jnp.uint32)] * SC_BUFS,
        compiler_params=pltpu.CompilerParams(needs_layout_passes=False))
    def act_kernel(tab_hbm, idx_hbm, hp_hbm, act_hbm, idx_v, x_v, act_v, sems, xsem, osem, *bufs):
        t0 = _sc_worker(mesh) * tpw
        pltpu.sync_copy(idx_hbm.at[pl.ds(t0 * E, tpw * E)], idx_v)
        lane = lax.iota(i32, SC_LANES)

        def x_in(tl, s):
            return pltpu.make_async_copy(hp_hbm.at[t0 + tl], x_v.at[s], xsem.at[s])

        def act_out(tl, s):
            return pltpu.make_async_copy(act_v.at[s], act_hbm.at[pl.ds((t0 + tl) * E, E)], osem.at[s])

        x_in(0, 0).start()

        def compute(b, buf):
            tl, bt = b // bpt, b % bpt
            s = tl % 2

            @pl.when(bt == 0)
            def _():
                x_in(tl, s).wait()

                @pl.when(tl + 1 < tpw)
                def _():
                    x_in(tl + 1, 1 - s).start()

                @pl.when(tl >= 2)
                def _():
                    act_out(tl - 2, s).wait()

            for g in range(SC_ROWS // SC_LANES):
                def rows_body(i, vec):
                    r = g * SC_LANES + i * RG

                    def j_body(j, accs):
                        c = pl.multiple_of(j * SC_LANES, SC_LANES)
                        x_lo, x_hi = x_v[s, pl.ds(c, SC_LANES)], x_v[s, pl.ds(W + c, SC_LANES)]
                        out = []
                        for k in range(RG):
                            lo, hi = _sc_halves(buf[r + k, pl.ds(c, SC_LANES)])
                            out.append(accs[k] + lo * x_lo + hi * x_hi)
                        return tuple(out)

                    accs = lax.fori_loop(0, W // SC_LANES, j_body,
                                         tuple(jnp.zeros((SC_LANES,), f32) for _ in range(RG)))
                    for k in range(RG):
                        vec = jnp.where(lane == i * RG + k, jnp.sum(accs[k]), vec)
                    return vec

                vec = lax.fori_loop(0, SC_LANES // RG, rows_body, jnp.zeros((SC_LANES,), f32))
                act_v[s, pl.ds(pl.multiple_of(bt * SC_ROWS + g * SC_LANES, SC_LANES), SC_LANES)] = vec

            @pl.when(bt == bpt - 1)
            def _():
                act_out(tl, s).start()

        _sc_batches(tab_hbm, idx_v, bufs, sems, tpw * bpt, compute)
        for tl in range(max(tpw - 2, 0), tpw):
            act_out(tl, tl % 2).wait()

    return act_kernel(table_u, idx.reshape(-1), hp).reshape(T, E)


def _sc_peer_sum(cfg, table_v, idx, w):
    T, E = idx.shape
    W = table_v.shape[1]
    mesh, workers = _sc_mesh()
    tpw = T // workers
    bpt = E // SC_ROWS
    assert tpw * workers == T and bpt * SC_ROWS == E and (tpw * bpt) % SC_BUFS == 0
    NQ = 4
    qw = W // NQ

    @functools.partial(
        pl.kernel, out_type=jax.ShapeDtypeStruct((T, 2 * W), f32), mesh=mesh,
        scratch_types=[pltpu.VMEM((tpw * E,), i32), pltpu.VMEM((2, E), f32), pltpu.VMEM((2, 2 * W), f32),
                       pltpu.SemaphoreType.DMA((SC_BUFS,)), pltpu.SemaphoreType.DMA((2,)),
                       pltpu.SemaphoreType.DMA((2,))] + [pltpu.VMEM((SC_ROWS, W), jnp.uint32)] * SC_BUFS,
        compiler_params=pltpu.CompilerParams(needs_layout_passes=False))
    def sum_kernel(tab_hbm, idx_hbm, w_hbm, y_hbm, idx_v, w_v, y_v, sems, wsem, osem, *bufs):
        t0 = _sc_worker(mesh) * tpw
        pltpu.sync_copy(idx_hbm.at[pl.ds(t0 * E, tpw * E)], idx_v)
        nq = qw // SC_LANES

        def w_in(tl, s):
            return pltpu.make_async_copy(w_hbm.at[pl.ds((t0 + tl) * E, E)], w_v.at[s], wsem.at[s])

        def y_out(tl, s):
            return pltpu.make_async_copy(y_v.at[s], y_hbm.at[t0 + tl], osem.at[s])

        w_in(0, 0).start()

        def compute(b, buf):
            tl, bt = b // bpt, b % bpt
            s = tl % 2

            @pl.when(bt == 0)
            def _():
                w_in(tl, s).wait()

                @pl.when(tl + 1 < tpw)
                def _():
                    w_in(tl + 1, 1 - s).start()

                @pl.when(tl >= 2)
                def _():
                    y_out(tl - 2, s).wait()

            half = jnp.full((SC_LANES,), s, i32)
            for q in range(NQ):
                cols = [q * qw + n * SC_LANES for n in range(nq)]
                zero = jnp.zeros((SC_LANES,), f32)
                init = tuple(jnp.where(bt == 0, zero, y_v[s, pl.ds(c, SC_LANES)]) for c in cols) + \
                    tuple(jnp.where(bt == 0, zero, y_v[s, pl.ds(W + c, SC_LANES)]) for c in cols)

                def r_body(r, accs):
                    wr = plsc.load_gather(w_v, [half, jnp.full((SC_LANES,), bt * SC_ROWS + r, i32)])
                    lo_acc, hi_acc = list(accs[:nq]), list(accs[nq:])
                    for n, c in enumerate(cols):
                        lo, hi = _sc_halves(buf[r, pl.ds(c, SC_LANES)])
                        lo_acc[n] = lo_acc[n] + lo * wr
                        hi_acc[n] = hi_acc[n] + hi * wr
                    return tuple(lo_acc) + tuple(hi_acc)

                accs = lax.fori_loop(0, SC_ROWS, r_body, init)
                for n, c in enumerate(cols):
                    y_v[s, pl.ds(c, SC_LANES)] = accs[n]
                    y_v[s, pl.ds(W + c, SC_LANES)] = accs[nq + n]

            @pl.when(bt == bpt - 1)
            def _():
                y_out(tl, s).start()

        _sc_batches(tab_hbm, idx_v, bufs, sems, tpw * bpt, compute)
        for tl in range(max(tpw - 2, 0), tpw):
            y_out(tl, tl % 2).wait()

    return sum_kernel(table_v, idx.reshape(-1), w.reshape(-1))


def _pack_bf16_halves(a):
    h = a.shape[1] // 2
    bits = lambda t: lax.bitcast_convert_type(t.astype(bf16), jnp.uint16).astype(jnp.uint32)
    return bits(a[:, :h]) | (bits(a[:, h:]) << 16)


def _split_bf16(a):
    hi = a.astype(bf16)
    lo = (a - hi.astype(f32)).astype(bf16)
    return jnp.concatenate([hi, lo], axis=0)


def _fold_rows(a):
    n = a.shape[0] // 2
    return a[:n] + a[n:]


def _apply_group(h2, gates, words, sel, selt, gn, gf):
    R, D = h2.shape
    nw = D // 2
    nz = 2 * words.shape[0]
    lane = lax.broadcasted_iota(i32, (2 * R, nz), 1)
    rowi = lax.broadcasted_iota(i32, (2 * R, nz), 0)
    mine = ((lane // 2) % R == rowi % R) & (lane % 2 == rowi // R)
    zu = pltpu.bitcast(words[:, :nw], bf16)
    zv = pltpu.bitcast(words[:, nw:], bf16)
    x = _rms(h2, gn)
    xs = _split_bf16(jnp.concatenate([x[:, :nw], x[:, nw:]], axis=0))
    a = _fold_rows(lax.dot_general(xs, zu, (((1,), (1,)), ((), ())), preferred_element_type=f32))
    a = jnp.where(mine, a, 0.0)
    a = a[:R] + a[R:]
    act = _fold_rows(jnp.dot(_split_bf16(a), sel, preferred_element_type=f32))
    w = gates * _gelu(act)
    w_rows = _fold_rows(jnp.dot(_split_bf16(w), selt, preferred_element_type=f32))
    ws = _split_bf16(jnp.where(mine, jnp.concatenate([w_rows, w_rows], axis=0), 0.0))
    y2 = _fold_rows(jnp.dot(ws, zv, preferred_element_type=f32))
    y = jnp.concatenate([y2[:R], y2[R:]], axis=1)
    return _rms(h2 + y, gf)


def _peer_direct_kernel(idx_ref, idxn_ref, h_ref, gt_ref, tab_ref, sel_ref, selt_ref, gn_ref, gf_ref, o_ref,
                        buf_a, buf_b, sem_ref, *, cfg):
    E, R = cfg.slots, SUBLANES
    n_rows = E * R
    i = pl.program_id(0)

    def start_row(src_idx_ref, base, r, buf, s, priority=0):
        pltpu.async_copy(tab_ref.at[pl.ds(src_idx_ref[base + r], 1)], buf.at[pl.ds(r, 1)], sem_ref.at[s],
                         priority=priority)

    def wait_rows(buf, s):
        pltpu.make_async_copy(tab_ref.at[pl.ds(0, n_rows)], buf, sem_ref.at[s]).wait()

    def group(gi, buf):
        tok = slice(gi * R, (gi + 1) * R)
        o_ref[tok, :] = _apply_group(h_ref[tok, :], gt_ref[tok, :], buf[...], sel_ref[...], selt_ref[...],
                                     gn_ref[...], gf_ref[...])

    @pl.when(i == 0)
    def _():
        def body(r, carry):
            start_row(idx_ref, 0, r, buf_a, 0)
            return carry
        lax.fori_loop(0, n_rows, body, 0, unroll=8)

    wait_rows(buf_a, 0)
    for r in range(n_rows):
        start_row(idx_ref, n_rows, r, buf_b, 1, priority=r % 2)
    group(0, buf_a)
    wait_rows(buf_b, 1)
    for r in range(n_rows):
        start_row(idxn_ref, 0, r, buf_a, 0, priority=r % 2)
    group(1, buf_b)

    @pl.when(i == pl.num_programs(0) - 1)
    def _():
        wait_rows(buf_a, 0)


def _peer_direct(cfg, h2, gates, idx, table, gn, gf):
    D = h2.shape[1]
    E, R = cfg.slots, SUBLANES
    T = idx.shape[0] // E
    n_rows = E * R
    nz = 2 * n_rows
    n = T // (2 * R)
    sel = (jnp.arange(nz)[:, None] // (2 * R) == jnp.arange(E)[None, :]).astype(bf16)
    row = lambda i: (i, 0)
    smem = functools.partial(pl.BlockSpec, (2 * n_rows,), memory_space=pltpu.SMEM)
    return pl.pallas_call(
        functools.partial(_peer_direct_kernel, cfg=cfg),
        grid=(n,),
        in_specs=[smem(lambda i: (i,)), smem(lambda i: (jnp.minimum(i + 1, n - 1),)),
                  pl.BlockSpec((2 * R, D), row), pl.BlockSpec((2 * R, E), row), pl.BlockSpec(memory_space=pl.ANY),
                  _resident((nz, E)), _resident((E, nz)), _resident((1, D)), _resident((1, D))],
        out_specs=pl.BlockSpec((2 * R, D), row),
        out_shape=jax.ShapeDtypeStruct((T, D), f32),
        scratch_shapes=[pltpu.VMEM((n_rows, D), table.dtype), pltpu.VMEM((n_rows, D), table.dtype),
                        pltpu.SemaphoreType.DMA((2,))],
        compiler_params=_cparams("arbitrary"),
        name="peer_direct",
    )(idx, idx, h2, gates, table, sel, sel.T, gn, gf)


def _rowwise(name, body, out_cols, T, row0, *arrays, rows_per_step=128):
    tm = min(rows_per_step, T)
    assert T % tm == 0 and row0 % tm == 0

    def spec(a):
        if a.shape[0] == 1:
            return _resident(a.shape)
        off = row0 // tm if a.shape[0] > T else 0
        return pl.BlockSpec((tm, a.shape[1]), lambda i: (i + off, 0))

    def kern(*refs):
        refs[-1][...] = body(*(r[...] for r in refs[:-1]))

    return pl.pallas_call(
        kern, grid=(T // tm,),
        in_specs=[spec(a) for a in arrays],
        out_specs=pl.BlockSpec((tm, out_cols), lambda i: (i, 0)),
        out_shape=jax.ShapeDtypeStruct((T, out_cols), f32),
        compiler_params=_cparams("parallel"), name=name,
    )(*arrays)


def _peer_pre(cfg, h2, row0, gn):
    return _rowwise("peer_pre", _rms, h2.shape[1], h2.shape[0] - row0, row0, h2, gn)


def _peer_mid(cfg, gates, row0, act):
    return _rowwise("peer_mid", lambda g, a: g * _gelu(a), gates.shape[1], act.shape[0], row0, gates, act)


def _peer_post(cfg, h2, row0, y, gf):
    return _rowwise("peer_post", lambda h, yy, g: _rms(h + yy, g), h2.shape[1], y.shape[0], row0, h2, y, gf)


def _block(cfg, sc_fns, x, mem, positions, mix_norm_g, w_in, lam_q1, lam_k1, lam_q2, lam_k2, attn_head_g,
           ssm_a_re, ssm_a_im, ssm_log_dt, ssm_b_re, ssm_b_im, ssm_c_re, ssm_c_im, ssm_d, glu_w, glu_b,
           ssm_out_g, w_out, xattn_norm_g, mem_norm_g, xattn_wq, xattn_wkv, xattn_wo, ffn_norm_g,
           peer_wq, peer_k1, peer_k2, peer_u, peer_v, final_norm_g):
    B, S, D = x.shape
    T = B * S
    l = 0
    row = lambda a: a.reshape(1, -1)
    x2 = x.reshape(T, D)
    pos = positions.reshape(T, 1).astype(f32)
    freqs = cfg.rope_theta ** (-jnp.arange(0, cfg.rot_dim, 2, dtype=f32) / cfg.rot_dim)
    lane = jnp.arange(LANES) % cfg.diff_qkdim
    freq_row = jnp.where(lane < cfg.rot_dim, freqs[lane % (cfg.rot_dim // 2)], 0.0).reshape(1, LANES)

    lam_p = jnp.stack([lam_q1[l], lam_k1[l], lam_q2[l], lam_k2[l]])
    s5_params = _s5_params(cfg, ssm_a_re[l], ssm_a_im[l], ssm_log_dt[l], ssm_b_re[l], ssm_b_im[l],
                           ssm_c_re[l], ssm_c_im[l], ssm_d[l])
    ns, _, wn = s5_params[0].shape
    zero_state = jnp.zeros((ns, B, wn), f32)
    state = (zero_state, zero_state)
    w_in_b, glu_w_b, w_out_b = w_in[l].astype(bf16), glu_w[l].astype(bf16), w_out[l].astype(bf16)
    wq_b, wo_b, pw_b = xattn_wq[l].astype(bf16), xattn_wo[l].astype(bf16), peer_wq[l].astype(bf16)
    k1_b, k2_b = peer_k1[l].astype(bf16), peer_k2[l].astype(bf16)
    kv = _mem_kv(cfg, mem.reshape(B * cfg.n_mem, D), row(mem_norm_g[l]), xattn_wkv[l].astype(bf16))
    table_u, table_v = _pack_bf16_halves(peer_u[l]), _pack_bf16_halves(peer_v[l])
    table = jnp.concatenate([table_u, table_v], axis=1)
    E, R = cfg.slots, SUBLANES
    Lc = cfg.chunk
    Tc = B * Lc
    td = Tc * cfg.direct_sixteenths // 16
    sc_act, sc_sum = sc_fns

    def after(a, tokens):
        return lax.optimization_barrier((a,) + tuple(tokens))[0] if tokens else a

    def dense(c, qks, vs, state, tokens):
        qk, v, u = _in_proj(cfg, c, x2, pos, after(row(mix_norm_g[l]), tokens), freq_row, w_in_b)
        qks, vs = qks + [qk], vs + [v]
        att = _diff_attn(cfg, lam_p, qks, vs, row(attn_head_g[l]))
        y, state = _s5_scan(cfg, u.reshape(Tc, cfg.ssm_width), s5_params, state)
        ssm = _s5_glu(cfg, y, glu_w_b, row(glu_b[l]), row(ssm_out_g[l]))
        h1, hn = _out_proj(cfg, c, x2, att, ssm.reshape(Lc, B * cfg.ssm_width), w_out_b, row(xattn_norm_g[l]))
        h2, qp = _xattn(cfg, h1, hn, kv, wq_b, wo_b, row(ffn_norm_g[l]), pw_b)
        experts_t, gates_t = _peer_route(cfg, qp, k1_b, k2_b)
        return qks, vs, state, h2, experts_t, gates_t.T

    gn, gf = row(ffn_norm_g[l]), row(final_norm_g)
    qks, vs, outs, tokens, pending = [], [], [], [], None
    for c in range(cfg.n_chunks):
        qks, vs, state, h2, experts_t, gates = dense(c, qks, vs, state, tokens)
        parts, tokens, before_direct = [], [], []
        if td < Tc:
            idx_s = experts_t.T[td:]
            hp = _peer_pre(cfg, h2, td, gn)
            before_direct.append(hp)
        if pending is not None:
            p_h2, p_y, p_parts = pending
            p_parts.append(_peer_post(cfg, p_h2, td, p_y, after(gf, before_direct)))
            before_direct.append(p_parts[-1])
            outs.append(jnp.concatenate(p_parts, axis=0).reshape(B, Lc, D))
            pending = None
        if td:
            idx_d = experts_t[:, :td].reshape(E, td // R, R).transpose(1, 0, 2).reshape(-1)
            parts.append(_peer_direct(cfg, h2, gates, idx_d, table, after(gn, before_direct), gf))
        if td < Tc:
            act = sc_act(table_u, idx_s, hp)
            w = _peer_mid(cfg, after(gates, parts), td, act)
            tokens = [w]
            pending = (h2, sc_sum(table_v, idx_s, w), parts)
        else:
            outs.append(jnp.concatenate(parts, axis=0).reshape(B, Lc, D))
    if pending is not None:
        p_h2, p_y, p_parts = pending
        p_parts.append(_peer_post(cfg, p_h2, td, p_y, gf))
        outs.append(jnp.concatenate(p_parts, axis=0).reshape(B, Lc, D))
    return jnp.concatenate(outs, axis=1)


def kernel(x, mem, positions, mix_norm_g, w_in, lam_q1, lam_k1, lam_q2, lam_k2, attn_head_g, ssm_a_re, ssm_a_im, ssm_log_dt, ssm_b_re, ssm_b_im, ssm_c_re, ssm_c_im, ssm_d, glu_w, glu_b, ssm_out_g, w_out, xattn_norm_g, mem_norm_g, xattn_wq, xattn_wkv, xattn_wo, ffn_norm_g, peer_wq, peer_k1, peer_k2, peer_u, peer_v, final_norm_g):
    cfg = Cfg()
    return _block(cfg, (functools.partial(_sc_peer_act, cfg), functools.partial(_sc_peer_sum, cfg)), x, mem, positions, mix_norm_g, w_in, lam_q1, lam_k1,
                  lam_q2, lam_k2, attn_head_g, ssm_a_re, ssm_a_im, ssm_log_dt, ssm_b_re, ssm_b_im, ssm_c_re,
                  ssm_c_im, ssm_d, glu_w, glu_b, ssm_out_g, w_out, xattn_norm_g, mem_norm_g, xattn_wq,
                  xattn_wkv, xattn_wo, ffn_norm_g, peer_wq, peer_k1, peer_k2, peer_u, peer_v, final_norm_g)
```
